```python
import jax, jax.numpy as jnp
from jax import lax
import numpy as np

D_MODEL = 1024
BATCH = 8
SEQ = 4096
DEPTH = 2

CHUNK = 64
N_MIXERS = 2
CONV_WIDTH = 31
GMLP_BLOCK = 128
GMLP_INNER = 2 * D_MODEL
GMLP_HEADS = 8
GMLP_HEAD_DIM = GMLP_INNER // GMLP_HEADS
N_GROUPS = 4
EXPERTS_PER_GROUP = 8
N_EXPERTS = N_GROUPS * EXPERTS_PER_GROUP
GROUP_TOP_K = 1
TOP_K = 2
D_EXPERT = D_MODEL // 2
ROW_BLOCK = 128
EPS = 1e-6

kernel_name = "hybrid_conv_gmlp_hier_moe"


def rmsnorm(x, g):
    xf = x.astype(jnp.float32)
    y = xf * lax.rsqrt(jnp.mean(xf * xf, axis=-1, keepdims=True) + EPS)
    return (y * g.astype(jnp.float32)).astype(x.dtype)


def layernorm(x, g, b):
    xf = x.astype(jnp.float32)
    mu = jnp.mean(xf, axis=-1, keepdims=True)
    xc = xf - mu
    y = xc * lax.rsqrt(jnp.mean(xc * xc, axis=-1, keepdims=True) + EPS)
    return (y * g.astype(jnp.float32) + b.astype(jnp.float32)).astype(x.dtype)


def conv_module(h, w_pw1, b_pw1, w_dw, b_dw, ln_g, ln_b, w_pw2, b_pw2):
    a, gate = jnp.split(h @ w_pw1 + b_pw1, 2, axis=-1)
    z = a * jax.nn.sigmoid(gate)
    z = lax.conv_general_dilated(
        z, w_dw[:, None, :].astype(z.dtype), window_strides=(1,),
        padding=[(CONV_WIDTH - 1, 0)],
        dimension_numbers=("NWC", "WIO", "NWC"),
        feature_group_count=z.shape[-1]) + b_dw
    z = jax.nn.silu(layernorm(z, ln_g, ln_b))
    return z @ w_pw2 + b_pw2


def chunk_causal_mask():
    i = jnp.arange(GMLP_BLOCK)
    return (i[None, :] // CHUNK) <= (i[:, None] // CHUNK)


def gmlp_module(h, w_in, b_in, v_norm_g, w_s, b_s, w_out, b_out):
    B, S, _ = h.shape
    z = jax.nn.gelu(h @ w_in + b_in)
    u, v = jnp.split(z, 2, axis=-1)
    v = rmsnorm(v, v_norm_g)
    v = v.reshape(B, S // GMLP_BLOCK, GMLP_BLOCK, GMLP_HEADS, GMLP_HEAD_DIM)
    ws = jnp.where(chunk_causal_mask()[None], w_s, jnp.zeros_like(w_s)).astype(v.dtype)
    sv = jnp.einsum("hij,bnjhc->bnihc", ws, v) + jnp.transpose(b_s)[:, :, None].astype(v.dtype)
    return (u * sv.reshape(B, S, GMLP_INNER)) @ w_out + b_out


def hier_moe(x2, w_group, b_group, w_expert, b_expert, w_gate, w_up, w_down):
    T, D = x2.shape
    lg = (x2 @ w_group + b_group).astype(jnp.float32)
    p_group = jax.nn.softmax(lg, axis=-1)
    g_sel = jnp.argmax(lg, axis=-1).astype(jnp.int32)
    p_g = jnp.take_along_axis(p_group, g_sel[:, None], axis=-1)
    le = (x2 @ w_expert + b_expert).astype(jnp.float32).reshape(T, N_GROUPS, EXPERTS_PER_GROUP)
    le = jnp.take_along_axis(le, g_sel[:, None, None], axis=1)[:, 0]
    top_v, top_i = lax.top_k(le, TOP_K)
    gates = p_g * jax.nn.softmax(top_v, axis=-1)
    expert = g_sel[:, None] * EXPERTS_PER_GROUP + top_i.astype(jnp.int32)

    A = T * TOP_K
    e_flat = expert.reshape(A)
    w_flat = gates.reshape(A)
    tok_flat = jnp.arange(A, dtype=jnp.int32) // TOP_K
    order = jnp.argsort(e_flat)
    e_s, tok_s, w_s = e_flat[order], tok_flat[order], w_flat[order]
    counts = jnp.bincount(e_flat, length=N_EXPERTS).astype(jnp.int32)
    padded = ((counts + ROW_BLOCK - 1) // ROW_BLOCK) * ROW_BLOCK
    start = jnp.cumsum(counts) - counts
    pend = jnp.cumsum(padded)
    pstart = pend - padded
    dest = pstart[e_s] + (jnp.arange(A, dtype=jnp.int32) - start[e_s])
    n_blocks = (A + ROW_BLOCK - 1) // ROW_BLOCK + N_EXPERTS
    R = n_blocks * ROW_BLOCK
    rows = jnp.zeros((R, D), x2.dtype).at[dest].set(x2[tok_s])
    row_tok = jnp.zeros((R,), jnp.int32).at[dest].set(tok_s)
    row_w = jnp.zeros((R,), jnp.float32).at[dest].set(w_s)
    block_expert = jnp.minimum(
        jnp.searchsorted(pend, jnp.arange(n_blocks, dtype=jnp.int32) * ROW_BLOCK, side="right"),
        N_EXPERTS - 1).astype(jnp.int32)

    def expert_block(args):
        xb, e = args
        hb = jax.nn.silu(xb @ w_gate[e]) * (xb @ w_up[e])
        return hb @ w_down[e]

    out = lax.map(expert_block, (rows.reshape(n_blocks, ROW_BLOCK, D), block_expert))
    out = out.reshape(R, D) * row_w[:, None].astype(out.dtype)
    return jnp.zeros((T, D), x2.dtype).at[row_tok].add(out)


def setup_inputs(seed: int = 0) -> dict:
    key = jax.random.key(seed)
    ks = iter(jax.random.split(key, 40))

    def nrm(shape, scale):
        return scale * jax.random.normal(next(ks), shape, jnp.float32)

    D = D_MODEL
    n_a = (DEPTH + 1) // 2
    n_b = DEPTH // 2
    return {
        "x": nrm((BATCH, SEQ, D), 1.0),
        "norm_mix_g": 1.0 + nrm((DEPTH, D), 0.02),
        "norm_ffn_g": 1.0 + nrm((DEPTH, D), 0.02),
        "cv_w_pw1": nrm((n_a, D, 2 * D), D ** -0.5),
        "cv_b_pw1": nrm((n_a, 2 * D), 0.02),
        "cv_w_dw": nrm((n_a, CONV_WIDTH, D), CONV_WIDTH ** -0.5),
        "cv_b_dw": nrm((n_a, D), 0.02),
        "cv_ln_g": 1.0 + nrm((n_a, D), 0.02),
        "cv_ln_b": nrm((n_a, D), 0.02),
        "cv_w_pw2": nrm((n_a, D, D), D ** -0.5),
        "cv_b_pw2": nrm((n_a, D), 0.02),
        "gm_w_in": nrm((n_b, D, 2 * GMLP_INNER), D ** -0.5),
        "gm_b_in": nrm((n_b, 2 * GMLP_INNER), 0.02),
        "gm_v_norm_g": 1.0 + nrm((n_b, GMLP_INNER), 0.02),
        "gm_w_s": nrm((n_b, GMLP_HEADS, GMLP_BLOCK, GMLP_BLOCK), GMLP_BLOCK ** -0.5),
        "gm_b_s": 1.0 + nrm((n_b, GMLP_HEADS, GMLP_BLOCK), 0.02),
        "gm_w_out": nrm((n_b, GMLP_INNER, D), GMLP_INNER ** -0.5),
        "gm_b_out": nrm((n_b, D), 0.02),
        "moe_w_group": nrm((DEPTH, D, N_GROUPS), D ** -0.5),
        "moe_b_group": nrm((DEPTH, N_GROUPS), 0.01),
        "moe_w_expert": nrm((DEPTH, D, N_EXPERTS), D ** -0.5),
        "moe_b_expert": nrm((DEPTH, N_EXPERTS), 0.01),
        "moe_w_gate": nrm((DEPTH, N_EXPERTS, D, D_EXPERT), D ** -0.5),
        "moe_w_up": nrm((DEPTH, N_EXPERTS, D, D_EXPERT), D ** -0.5),
        "moe_w_down": nrm((DEPTH, N_EXPERTS, D_EXPERT, D), D_EXPERT ** -0.5),
        "final_g": 1.0 + nrm((D,), 0.02),
    }


def reference(x, norm_mix_g, norm_ffn_g,
              cv_w_pw1, cv_b_pw1, cv_w_dw, cv_b_dw, cv_ln_g, cv_ln_b, cv_w_pw2, cv_b_pw2,
              gm_w_in, gm_b_in, gm_v_norm_g, gm_w_s, gm_b_s, gm_w_out, gm_b_out,
              moe_w_group, moe_b_group, moe_w_expert, moe_b_expert,
              moe_w_gate, moe_w_up, moe_w_down, final_g):
    h = x
    B, S, D = h.shape
    for i in range(DEPTH):
        hn = rmsnorm(h, norm_mix_g[i])
        j = i // N_MIXERS
        if i % N_MIXERS == 0:
            m = conv_module(hn, cv_w_pw1[j], cv_b_pw1[j], cv_w_dw[j], cv_b_dw[j],
                            cv_ln_g[j], cv_ln_b[j], cv_w_pw2[j], cv_b_pw2[j])
        else:
            m = gmlp_module(hn, gm_w_in[j], gm_b_in[j], gm_v_norm_g[j], gm_w_s[j],
                            gm_b_s[j], gm_w_out[j], gm_b_out[j])
        h = h + m
        hn = rmsnorm(h, norm_ffn_g[i])
        y = hier_moe(hn.reshape(B * S, D), moe_w_group[i], moe_b_group[i],
                     moe_w_expert[i], moe_b_expert[i], moe_w_gate[i], moe_w_up[i], moe_w_down[i])
        h = h + y.reshape(B, S, D)
    return rmsnorm(h, final_g)
```

```python
import functools

import jax
import jax.numpy as jnp
from jax import lax
from jax.experimental import pallas as pl
from jax.experimental.pallas import tpu as pltpu

D_MODEL = 1024
CONV_WIDTH = 31
GMLP_BLOCK = 128
GMLP_CHUNK = 64
GMLP_INNER = 2 * D_MODEL
GMLP_HEADS = 8
GMLP_HEAD_DIM = GMLP_INNER // GMLP_HEADS
N_GROUPS = 4
EXPERTS_PER_GROUP = 8
N_EXPERTS = N_GROUPS * EXPERTS_PER_GROUP
D_EXPERT = D_MODEL // 2
EPS = 1e-6

LANES = 128
SUBLANES = 8
HIST = 32
TS_CONV = 256
TS_GMLP = 256
TS_FINAL = 512
BM = 256
CONV_RC = 32
CONV_CW = 256
GROUP_LANE0 = N_EXPERTS
VMEM_LIMIT = 56 * 1024 * 1024

R_E1, R_E2, R_RANK1, R_RANK2, R_GATE1, R_GATE2 = range(6)

F32 = jnp.float32
BF16 = jnp.bfloat16


def _rms(xf, g):
    return xf * lax.rsqrt(jnp.mean(xf * xf, axis=-1, keepdims=True) + EPS) * g


def _route_tail(hnew, gffn_ref, wr_ref, br_ref, run_ref, hn_out_ref, route_ref, cnt_ref):
    ts = hnew.shape[0]
    hn2 = _rms(hnew, gffn_ref[...])
    hn_out_ref[...] = hn2
    logits = jnp.dot(hn2, wr_ref[...], precision=lax.Precision.HIGHEST,
                     preferred_element_type=F32) + br_ref[...]
    lane = lax.broadcasted_iota(jnp.int32, (ts, LANES), 1).astype(F32)
    ninf = jnp.float32(-jnp.inf)
    big = jnp.float32(1e9)
    is_g = (lane >= GROUP_LANE0) & (lane < GROUP_LANE0 + N_GROUPS)
    lg = jnp.where(is_g, logits, ninf)
    gmax = jnp.max(lg, axis=-1, keepdims=True)
    g_lane = jnp.min(jnp.where(lg == gmax, lane, big), axis=-1, keepdims=True)
    p_g = 1.0 / jnp.sum(jnp.where(is_g, jnp.exp(lg - gmax), 0.0), axis=-1, keepdims=True)
    lo = (g_lane - GROUP_LANE0) * EXPERTS_PER_GROUP
    in_grp = (lane >= lo) & (lane < lo + EXPERTS_PER_GROUP)
    le = jnp.where(in_grp, logits, ninf)
    v1 = jnp.max(le, axis=-1, keepdims=True)
    i1 = jnp.min(jnp.where(le == v1, lane, big), axis=-1, keepdims=True)
    oh1 = lane == i1
    le2 = jnp.where(oh1, ninf, le)
    v2 = jnp.max(le2, axis=-1, keepdims=True)
    i2 = jnp.min(jnp.where(le2 == v2, lane, big), axis=-1, keepdims=True)
    oh2 = lane == i2
    e = jnp.exp(v2 - v1)
    den = 1.0 + e
    gate1 = p_g * (1.0 / den)
    gate2 = p_g * (e / den)

    oh = jnp.where(oh1 | oh2, 1.0, 0.0)
    ri = lax.broadcasted_iota(jnp.int32, (ts, ts), 0)
    ci = lax.broadcasted_iota(jnp.int32, (ts, ts), 1)
    tril = jnp.where(ci < ri, 1.0, 0.0).astype(BF16)
    c = jnp.dot(tril, oh.astype(BF16), preferred_element_type=F32) + run_ref[...]
    rank1 = jnp.sum(jnp.where(oh1, c, 0.0), axis=-1, keepdims=True)
    rank2 = jnp.sum(jnp.where(oh2, c, 0.0), axis=-1, keepdims=True)
    run = run_ref[...] + jnp.sum(oh, axis=0, keepdims=True)
    run_ref[...] = run
    cnt_ref[...] = jnp.broadcast_to(run, cnt_ref.shape)

    rec = jnp.where(lane == R_E1, i1, 0.0)
    rec = jnp.where(lane == R_E2, i2, rec)
    rec = jnp.where(lane == R_RANK1, rank1, rec)
    rec = jnp.where(lane == R_RANK2, rank2, rec)
    rec = jnp.where(lane == R_GATE1, gate1, rec)
    rec = jnp.where(lane == R_GATE2, gate2, rec)
    route_ref[...] = rec


def _conv_layer_body(tiles_per_seq,
                     x_ref, gmix_ref, wpw1_ref, bpw1_ref, wdw_ref, bdw_ref, lng_ref, lnb_ref,
                     wpw2_ref, bpw2_ref, gffn_ref, wr_ref, br_ref,
                     h_out_ref, hn_out_ref, route_ref, cnt_ref,
                     zext_ref, zs_ref, y_ref, run_ref):
    i = pl.program_id(0)
    ts = x_ref.shape[0]
    d = x_ref.shape[1]

    @pl.when(i == 0)
    def _():
        run_ref[...] = jnp.zeros_like(run_ref)

    @pl.when(i % tiles_per_seq == 0)
    def _():
        zext_ref[0:HIST, :] = jnp.zeros((HIST, d), F32)

    x = x_ref[...]
    hn = _rms(x, gmix_ref[...])
    p = jnp.dot(hn.astype(BF16), wpw1_ref[...], preferred_element_type=F32) + bpw1_ref[...]
    z = p[:, :d] * jax.nn.sigmoid(p[:, d:])
    zext_ref[HIST:HIST + ts, :] = z

    span = ts + HIST - SUBLANES
    for r in range(1, SUBLANES):
        zs_ref[r - 1, 0:span, :] = zext_ref[r:r + span, :]

    first = HIST - (CONV_WIDTH - 1)

    def chunk(ci, carry):
        r0 = pl.multiple_of(ci * CONV_RC, CONV_RC)
        for c0 in range(0, d, CONV_CW):
            acc = jnp.broadcast_to(bdw_ref[:, c0:c0 + CONV_CW], (CONV_RC, CONV_CW))
            for k in range(CONV_WIDTH):
                q, r = divmod(first + k, SUBLANES)
                if r == 0:
                    slab = zext_ref[pl.ds(r0 + q * SUBLANES, CONV_RC), c0:c0 + CONV_CW]
                else:
                    slab = zs_ref[r - 1, pl.ds(r0 + q * SUBLANES, CONV_RC), c0:c0 + CONV_CW]
                acc = acc + wdw_ref[k:k + 1, c0:c0 + CONV_CW] * slab
            y_ref[pl.ds(r0, CONV_RC), c0:c0 + CONV_CW] = acc
        return carry

    lax.fori_loop(0, ts // CONV_RC, chunk, 0)
    zext_ref[0:HIST, :] = zext_ref[ts:ts + HIST, :]

    y = y_ref[...]
    mu = jnp.mean(y, axis=-1, keepdims=True)
    yc = y - mu
    yn = yc * lax.rsqrt(jnp.mean(yc * yc, axis=-1, keepdims=True) + EPS)
    yn = yn * lng_ref[...] + lnb_ref[...]
    a = yn * jax.nn.sigmoid(yn)
    m = jnp.dot(a.astype(BF16), wpw2_ref[...], preferred_element_type=F32) + bpw2_ref[...]
    hnew = x + m
    h_out_ref[...] = hnew
    _route_tail(hnew, gffn_ref, wr_ref, br_ref, run_ref, hn_out_ref, route_ref, cnt_ref)


def _gmlp_layer_body(h_ref, g1_ref, g2_ref, rprev_ref, gmix_ref, win_ref, bin_ref, vg_ref,
                     ws_ref, bst_ref, wout_ref, bout_ref, gffn_ref, wr_ref, br_ref,
                     h_out_ref, hn_out_ref, route_ref, cnt_ref,
                     gated_ref, run_ref):
    i = pl.program_id(0)
    ts = h_ref.shape[0]

    @pl.when(i == 0)
    def _():
        run_ref[...] = jnp.zeros_like(run_ref)

    rp = rprev_ref[...]
    h = (h_ref[...] + rp[:, R_GATE1:R_GATE1 + 1] * g1_ref[...]
         + rp[:, R_GATE2:R_GATE2 + 1] * g2_ref[...])
    hn = _rms(h, gmix_ref[...])
    z = jax.nn.gelu(jnp.dot(hn.astype(BF16), win_ref[...], preferred_element_type=F32)
                    + bin_ref[...])
    u = z[:, :GMLP_INNER]
    v = _rms(z[:, GMLP_INNER:], vg_ref[...]).astype(BF16)
    for blk in range(ts // GMLP_BLOCK):
        rows = slice(blk * GMLP_BLOCK, (blk + 1) * GMLP_BLOCK)
        for hd in range(GMLP_HEADS):
            cols = slice(hd * GMLP_HEAD_DIM, (hd + 1) * GMLP_HEAD_DIM)
            sv = jnp.dot(ws_ref[hd], v[rows, cols], preferred_element_type=F32)
            sv = sv + bst_ref[:, hd:hd + 1]
            gated_ref[rows, cols] = (u[rows, cols] * sv).astype(BF16)
    out = jnp.dot(gated_ref[...], wout_ref[...], preferred_element_type=F32) + bout_ref[...]
    hnew = h + out
    h_out_ref[...] = hnew
    _route_tail(hnew, gffn_ref, wr_ref, br_ref, run_ref, hn_out_ref, route_ref, cnt_ref)


def _expert_body(be_ref, nused_ref, rows_ref, wg_ref, wu_ref, wd_ref, y_ref, wgu_s, wd_s):
    b = pl.program_id(0)
    prev = be_ref[jnp.maximum(b - 1, 0)]
    changed = (b == 0) | (be_ref[b] != prev)

    @pl.when(changed)
    def _():
        wgu_s[:, :D_EXPERT] = wg_ref[0].astype(BF16)
        wgu_s[:, D_EXPERT:] = wu_ref[0].astype(BF16)
        wd_s[...] = wd_ref[0].astype(BF16)

    @pl.when(b < nused_ref[0])
    def _():
        xb = rows_ref[...].astype(BF16)
        gu = jnp.dot(xb, wgu_s[...], preferred_element_type=F32)
        g = gu[:, :D_EXPERT]
        hb = (g * jax.nn.sigmoid(g)) * gu[:, D_EXPERT:]
        y_ref[...] = jnp.dot(hb.astype(BF16), wd_s[...], preferred_element_type=F32)


def _final_body(h_ref, g1_ref, g2_ref, rprev_ref, gfin_ref, o_ref):
    rp = rprev_ref[...]
    h = (h_ref[...] + rp[:, R_GATE1:R_GATE1 + 1] * g1_ref[...]
         + rp[:, R_GATE2:R_GATE2 + 1] * g2_ref[...])
    o_ref[...] = _rms(h, gfin_ref[...])


def _const_spec(shape):
    return pl.BlockSpec(shape, lambda i: (0,) * len(shape))


def _row_spec(ts, width):
    return pl.BlockSpec((ts, width), lambda i: (i, 0))


def _layer_out(t, d, ts):
    shapes = [jax.ShapeDtypeStruct((t, d), F32),
              jax.ShapeDtypeStruct((t, d), F32),
              jax.ShapeDtypeStruct((t, LANES), F32),
              jax.ShapeDtypeStruct((SUBLANES, LANES), F32)]
    specs = [_row_spec(ts, d), _row_spec(ts, d), _row_spec(ts, LANES),
             _const_spec((SUBLANES, LANES))]
    return shapes, specs


def _conv_layer(x2, seq, gmix, wpw1, bpw1, wdw, bdw, lng, lnb, wpw2, bpw2, gffn, wr, br):
    t, d = x2.shape
    ts = TS_CONV
    shapes, out_specs = _layer_out(t, d, ts)
    body = functools.partial(_conv_layer_body, seq // ts)
    return pl.pallas_call(
        body,
        grid=(t // ts,),
        in_specs=[_row_spec(ts, d), _const_spec((1, d)), _const_spec((d, 2 * d)),
                  _const_spec((1, 2 * d)), _const_spec((CONV_WIDTH, d)), _const_spec((1, d)),
                  _const_spec((1, d)), _const_spec((1, d)), _const_spec((d, d)),
                  _const_spec((1, d)), _const_spec((1, d)), _const_spec((d, LANES)),
                  _const_spec((1, LANES))],
        out_specs=out_specs,
        out_shape=shapes,
        scratch_shapes=[pltpu.VMEM((ts + HIST, d), F32),
                        pltpu.VMEM((SUBLANES - 1, ts + HIST - SUBLANES, d), F32),
                        pltpu.VMEM((ts, d), F32),
                        pltpu.VMEM((1, LANES), F32)],
        compiler_params=pltpu.CompilerParams(dimension_semantics=("arbitrary",),
                                             vmem_limit_bytes=VMEM_LIMIT),
        name="conv_layer",
    )(x2, gmix, wpw1, bpw1, wdw, bdw, lng, lnb, wpw2, bpw2, gffn, wr, br)


def _gmlp_layer(h, g1, g2, rprev, gmix, win, bin_, vg, ws, bst, wout, bout, gffn, wr, br):
    t, d = h.shape
    ts = TS_GMLP
    shapes, out_specs = _layer_out(t, d, ts)
    return pl.pallas_call(
        _gmlp_layer_body,
        grid=(t // ts,),
        in_specs=[_row_spec(ts, d), _row_spec(ts, d), _row_spec(ts, d), _row_spec(ts, LANES),
                  _const_spec((1, d)), _const_spec((d, 2 * GMLP_INNER)),
                  _const_spec((1, 2 * GMLP_INNER)), _const_spec((1, GMLP_INNER)),
                  _const_spec((GMLP_HEADS, GMLP_BLOCK, GMLP_BLOCK)),
                  _const_spec((GMLP_BLOCK, GMLP_HEADS)), _const_spec((GMLP_INNER, d)),
                  _const_spec((1, d)), _const_spec((1, d)), _const_spec((d, LANES)),
                  _const_spec((1, LANES))],
        out_specs=out_specs,
        out_shape=shapes,
        scratch_shapes=[pltpu.VMEM((ts, GMLP_INNER), BF16),
                        pltpu.VMEM((1, LANES), F32)],
        compiler_params=pltpu.CompilerParams(dimension_semantics=("arbitrary",),
                                             vmem_limit_bytes=VMEM_LIMIT),
        name="gmlp_layer",
    )(h, g1, g2, rprev, gmix, win, bin_, vg, ws, bst, wout, bout, gffn, wr, br)


def _experts(rows, blk_expert, n_used, wg, wu, wd):
    r, d = rows.shape
    n_blk = r // BM

    def row_map(b, be, nu):
        return (jnp.minimum(b, nu[0] - 1), 0)

    def w_map(b, be, nu):
        return (be[b], 0, 0)

    grid_spec = pltpu.PrefetchScalarGridSpec(
        num_scalar_prefetch=2,
        grid=(n_blk,),
        in_specs=[pl.BlockSpec((BM, d), row_map),
                  pl.BlockSpec((1, d, D_EXPERT), w_map),
                  pl.BlockSpec((1, d, D_EXPERT), w_map),
                  pl.BlockSpec((1, D_EXPERT, d), w_map)],
        out_specs=pl.BlockSpec((BM, d), row_map),
        scratch_shapes=[pltpu.VMEM((d, 2 * D_EXPERT), BF16),
                        pltpu.VMEM((D_EXPERT, d), BF16)],
    )
    return pl.pallas_call(
        _expert_body,
        grid_spec=grid_spec,
        out_shape=jax.ShapeDtypeStruct((r, d), F32),
        compiler_params=pltpu.CompilerParams(dimension_semantics=("arbitrary",),
                                             vmem_limit_bytes=VMEM_LIMIT),
        name="experts",
    )(blk_expert, n_used, rows, wg, wu, wd)


def _final(h, g1, g2, rprev, gfin):
    t, d = h.shape
    ts = TS_FINAL
    return pl.pallas_call(
        _final_body,
        grid=(t // ts,),
        in_specs=[_row_spec(ts, d), _row_spec(ts, d), _row_spec(ts, d), _row_spec(ts, LANES),
                  _const_spec((1, d))],
        out_specs=_row_spec(ts, d),
        out_shape=jax.ShapeDtypeStruct((t, d), F32),
        compiler_params=pltpu.CompilerParams(dimension_semantics=("arbitrary",),
                                             vmem_limit_bytes=VMEM_LIMIT),
        name="final_norm",
    )(h, g1, g2, rprev, gfin)


def _gather_rows(table, idx):
    return jnp.take(table, idx, axis=0)


def _plan(route, cnt, n_blk):
    t = route.shape[0]
    e1 = route[:, R_E1].astype(jnp.int32)
    e2 = route[:, R_E2].astype(jnp.int32)
    rank1 = route[:, R_RANK1].astype(jnp.int32)
    rank2 = route[:, R_RANK2].astype(jnp.int32)
    counts = cnt[0, :N_EXPERTS].astype(jnp.int32)
    nb = (counts + BM - 1) // BM
    pend = jnp.cumsum(nb)
    pstart = (pend - nb) * BM
    eids = jnp.arange(N_EXPERTS, dtype=jnp.int32)
    dest1 = jnp.sum(jnp.where(e1[:, None] == eids[None, :], pstart[None, :], 0), axis=1) + rank1
    dest2 = jnp.sum(jnp.where(e2[:, None] == eids[None, :], pstart[None, :], 0), axis=1) + rank2
    n_used = pend[-1:].astype(jnp.int32)
    blk = jnp.minimum(jnp.arange(n_blk, dtype=jnp.int32), n_used[0] - 1)
    blk_expert = jnp.sum((blk[:, None] >= pend[None, :]).astype(jnp.int32), axis=1)
    blk_expert = jnp.minimum(blk_expert, N_EXPERTS - 1).astype(jnp.int32)
    tok = jnp.arange(t, dtype=jnp.int32)
    row_tok = jnp.zeros((n_blk * BM,), jnp.int32).at[dest1].set(tok).at[dest2].set(tok)
    return dest1, dest2, row_tok, blk_expert, n_used


def _moe(hn, route, cnt, wg, wu, wd):
    t = hn.shape[0]
    n_blk = (2 * t) // BM + N_EXPERTS
    dest1, dest2, row_tok, blk_expert, n_used = _plan(route, cnt, n_blk)
    rows = _gather_rows(hn, row_tok)
    y = _experts(rows, blk_expert, n_used, wg, wu, wd)
    return _gather_rows(y, dest1), _gather_rows(y, dest2)


def _router_weights(w_group, b_group, w_expert, b_expert):
    d = w_group.shape[0]
    wr = jnp.zeros((d, LANES), F32)
    wr = wr.at[:, :N_EXPERTS].set(w_expert).at[:, GROUP_LANE0:GROUP_LANE0 + N_GROUPS].set(w_group)
    br = jnp.zeros((1, LANES), F32)
    br = br.at[0, :N_EXPERTS].set(b_expert).at[0, GROUP_LANE0:GROUP_LANE0 + N_GROUPS].set(b_group)
    return wr, br


def kernel(x, norm_mix_g, norm_ffn_g, cv_w_pw1, cv_b_pw1, cv_w_dw, cv_b_dw, cv_ln_g, cv_ln_b, cv_w_pw2, cv_b_pw2, gm_w_in, gm_b_in, gm_v_norm_g, gm_w_s, gm_b_s, gm_w_out, gm_b_out, moe_w_group, moe_b_group, moe_w_expert, moe_b_expert, moe_w_gate, moe_w_up, moe_w_down, final_g):
    bsz, seq, d = x.shape
    t = bsz * seq
    x2 = x.reshape(t, d)
    row = lambda a: a.reshape(1, -1)

    wr0, br0 = _router_weights(moe_w_group[0], moe_b_group[0], moe_w_expert[0], moe_b_expert[0])
    h1, hn1, route0, cnt0 = _conv_layer(
        x2, seq, row(norm_mix_g[0]), cv_w_pw1[0].astype(BF16), row(cv_b_pw1[0]), cv_w_dw[0],
        row(cv_b_dw[0]), row(cv_ln_g[0]), row(cv_ln_b[0]), cv_w_pw2[0].astype(BF16),
        row(cv_b_pw2[0]), row(norm_ffn_g[0]), wr0, br0)
    ga0, gb0 = _moe(hn1, route0, cnt0, moe_w_gate[0], moe_w_up[0], moe_w_down[0])

    idx = jnp.arange(GMLP_BLOCK)
    mask = (idx[None, :] // GMLP_CHUNK) <= (idx[:, None] // GMLP_CHUNK)
    ws = jnp.where(mask[None], gm_w_s[0], 0.0).astype(BF16)
    wr1, br1 = _router_weights(moe_w_group[1], moe_b_group[1], moe_w_expert[1], moe_b_expert[1])
    h2, hn2, route1, cnt1 = _gmlp_layer(
        h1, ga0, gb0, route0, row(norm_mix_g[1]), gm_w_in[0].astype(BF16), row(gm_b_in[0]),
        row(gm_v_norm_g[0]), ws, jnp.transpose(gm_b_s[0]), gm_w_out[0].astype(BF16),
        row(gm_b_out[0]), row(norm_ffn_g[1]), wr1, br1)
    ga1, gb1 = _moe(hn2, route1, cnt1, moe_w_gate[1], moe_w_up[1], moe_w_down[1])

    out = _final(h2, ga1, gb1, route1, row(final_g))
    return out.reshape(bsz, seq, d)
```

```python
import functools

import jax
import jax.numpy as jnp
from jax import lax
from jax.experimental import pallas as pl
from jax.experimental.pallas import tpu as pltpu
from jax.experimental.pallas import tpu_sc as plsc

D_MODEL = 1024
CONV_WIDTH = 31
GMLP_BLOCK = 128
GMLP_CHUNK = 64
GMLP_INNER = 2 * D_MODEL
GMLP_HEADS = 8
GMLP_HEAD_DIM = GMLP_INNER // GMLP_HEADS
N_GROUPS = 4
EXPERTS_PER_GROUP = 8
N_EXPERTS = N_GROUPS * EXPERTS_PER_GROUP
D_EXPERT = D_MODEL // 2
EPS = 1e-6

LANES = 128
SUBLANES = 8
HIST = 32
TS_CONV = 256
TS_GMLP = 256
TS_FINAL = 512
BM = 256
CONV_RC = 32
CONV_CW = 256
GROUP_LANE0 = N_EXPERTS
VMEM_LIMIT = 56 * 1024 * 1024
SC_DISPATCH_CHUNK = 128
SC_COMBINE_CHUNK = 64

R_E1, R_E2, R_RANK1, R_RANK2, R_GATE1, R_GATE2 = range(6)

F32 = jnp.float32
BF16 = jnp.bfloat16


def _rms(xf, g):
    return xf * lax.rsqrt(jnp.mean(xf * xf, axis=-1, keepdims=True) + EPS) * g


def _pack_bf16_pairs(xf):
    w = xf.shape[1] // 2
    bits = lax.bitcast_convert_type(xf.astype(BF16).astype(F32), jnp.int32)
    return lax.shift_right_logical(bits[:, :w], 16) | bits[:, w:]


def _unpack_bf16_pairs(p):
    lo = lax.bitcast_convert_type(lax.shift_left(p, 16), F32).astype(BF16)
    hi = lax.bitcast_convert_type(p & jnp.int32(-65536), F32).astype(BF16)
    return lo, hi


def _route_tail(hnew, gffn_ref, wr_ref, br_ref, run_ref, hn_out_ref, route_ref, route_t_ref,
                cnt_ref):
    ts = hnew.shape[0]
    hn2 = _rms(hnew, gffn_ref[...])
    hn_out_ref[...] = _pack_bf16_pairs(hn2)
    logits = jnp.dot(hn2, wr_ref[...], precision=lax.Precision.HIGHEST,
                     preferred_element_type=F32) + br_ref[...]
    lane = lax.broadcasted_iota(jnp.int32, (ts, LANES), 1).astype(F32)
    ninf = jnp.float32(-jnp.inf)
    big = jnp.float32(1e9)
    is_g = (lane >= GROUP_LANE0) & (lane < GROUP_LANE0 + N_GROUPS)
    lg = jnp.where(is_g, logits, ninf)
    gmax = jnp.max(lg, axis=-1, keepdims=True)
    g_lane = jnp.min(jnp.where(lg == gmax, lane, big), axis=-1, keepdims=True)
    p_g = 1.0 / jnp.sum(jnp.where(is_g, jnp.exp(lg - gmax), 0.0), axis=-1, keepdims=True)
    lo = (g_lane - GROUP_LANE0) * EXPERTS_PER_GROUP
    in_grp = (lane >= lo) & (lane < lo + EXPERTS_PER_GROUP)
    le = jnp.where(in_grp, logits, ninf)
    v1 = jnp.max(le, axis=-1, keepdims=True)
    i1 = jnp.min(jnp.where(le == v1, lane, big), axis=-1, keepdims=True)
    oh1 = lane == i1
    le2 = jnp.where(oh1, ninf, le)
    v2 = jnp.max(le2, axis=-1, keepdims=True)
    i2 = jnp.min(jnp.where(le2 == v2, lane, big), axis=-1, keepdims=True)
    oh2 = lane == i2
    e = jnp.exp(v2 - v1)
    den = 1.0 + e
    gate1 = p_g * (1.0 / den)
    gate2 = p_g * (e / den)

    oh = jnp.where(oh1 | oh2, 1.0, 0.0)
    ri = lax.broadcasted_iota(jnp.int32, (ts, ts), 0)
    ci = lax.broadcasted_iota(jnp.int32, (ts, ts), 1)
    tril = jnp.where(ci < ri, 1.0, 0.0).astype(BF16)
    c = jnp.dot(tril, oh.astype(BF16), preferred_element_type=F32) + run_ref[...]
    rank1 = jnp.sum(jnp.where(oh1, c, 0.0), axis=-1, keepdims=True)
    rank2 = jnp.sum(jnp.where(oh2, c, 0.0), axis=-1, keepdims=True)
    run = run_ref[...] + jnp.sum(oh, axis=0, keepdims=True)
    run_ref[...] = run
    cnt_ref[...] = jnp.broadcast_to(run, cnt_ref.shape)

    rec = jnp.where(lane == R_E1, i1, 0.0)
    rec = jnp.where(lane == R_E2, i2, rec)
    rec = jnp.where(lane == R_RANK1, rank1, rec)
    rec = jnp.where(lane == R_RANK2, rank2, rec)
    rec = jnp.where(lane == R_GATE1, gate1, rec)
    rec = jnp.where(lane == R_GATE2, gate2, rec)
    route_ref[...] = rec

    eye = ci == ri
    as_row = lambda col: jnp.sum(jnp.where(eye, col, 0.0), axis=0, keepdims=True)
    route_t_ref[...] = jnp.concatenate(
        [as_row(i1), as_row(i2), as_row(rank1), as_row(rank2),
         jnp.zeros((SUBLANES - 4, ts), F32)], axis=0)


def _conv_layer_body(tiles_per_seq,
                     x_ref, gmix_ref, wpw1_ref, bpw1_ref, wdw_ref, bdw_ref, lng_ref, lnb_ref,
                     wpw2_ref, bpw2_ref, gffn_ref, wr_ref, br_ref,
                     h_out_ref, hn_out_ref, route_ref, route_t_ref, cnt_ref,
                     zext_ref, zs_ref, y_ref, run_ref):
    i = pl.program_id(0)
    ts = x_ref.shape[0]
    d = x_ref.shape[1]

    @pl.when(i == 0)
    def _():
        run_ref[...] = jnp.zeros_like(run_ref)

    @pl.when(i % tiles_per_seq == 0)
    def _():
        zext_ref[0:HIST, :] = jnp.zeros((HIST, d), F32)

    x = x_ref[...]
    hn = _rms(x, gmix_ref[...])
    p = jnp.dot(hn.astype(BF16), wpw1_ref[...], preferred_element_type=F32) + bpw1_ref[...]
    z = p[:, :d] * jax.nn.sigmoid(p[:, d:])
    zext_ref[HIST:HIST + ts, :] = z

    span = ts + HIST - SUBLANES
    for r in range(1, SUBLANES):
        zs_ref[r - 1, 0:span, :] = zext_ref[r:r + span, :]

    first = HIST - (CONV_WIDTH - 1)

    def chunk(ci, carry):
        r0 = pl.multiple_of(ci * CONV_RC, CONV_RC)
        for c0 in range(0, d, CONV_CW):
            acc = jnp.broadcast_to(bdw_ref[:, c0:c0 + CONV_CW], (CONV_RC, CONV_CW))
            for k in range(CONV_WIDTH):
                q, r = divmod(first + k, SUBLANES)
                if r == 0:
                    slab = zext_ref[pl.ds(r0 + q * SUBLANES, CONV_RC), c0:c0 + CONV_CW]
                else:
                    slab = zs_ref[r - 1, pl.ds(r0 + q * SUBLANES, CONV_RC), c0:c0 + CONV_CW]
                acc = acc + wdw_ref[k:k + 1, c0:c0 + CONV_CW] * slab
            y_ref[pl.ds(r0, CONV_RC), c0:c0 + CONV_CW] = acc
        return carry

    lax.fori_loop(0, ts // CONV_RC, chunk, 0)
    zext_ref[0:HIST, :] = zext_ref[ts:ts + HIST, :]

    y = y_ref[...]
    mu = jnp.mean(y, axis=-1, keepdims=True)
    yc = y - mu
    yn = yc * lax.rsqrt(jnp.mean(yc * yc, axis=-1, keepdims=True) + EPS)
    yn = yn * lng_ref[...] + lnb_ref[...]
    a = yn * jax.nn.sigmoid(yn)
    m = jnp.dot(a.astype(BF16), wpw2_ref[...], preferred_element_type=F32) + bpw2_ref[...]
    hnew = x + m
    h_out_ref[...] = hnew
    _route_tail(hnew, gffn_ref, wr_ref, br_ref, run_ref, hn_out_ref, route_ref, route_t_ref,
                cnt_ref)


def _gmlp_layer_body(h_ref, g1_ref, g2_ref, rprev_ref, gmix_ref, win_ref, bin_ref, vg_ref,
                     ws_ref, bst_ref, wout_ref, bout_ref, gffn_ref, wr_ref, br_ref,
                     h_out_ref, hn_out_ref, route_ref, route_t_ref, cnt_ref,
                     gated_ref, run_ref):
    i = pl.program_id(0)
    ts = h_ref.shape[0]

    @pl.when(i == 0)
    def _():
        run_ref[...] = jnp.zeros_like(run_ref)

    rp = rprev_ref[...]
    h = (h_ref[...] + rp[:, R_GATE1:R_GATE1 + 1] * g1_ref[...]
         + rp[:, R_GATE2:R_GATE2 + 1] * g2_ref[...])
    hn = _rms(h, gmix_ref[...])
    z = jax.nn.gelu(jnp.dot(hn.astype(BF16), win_ref[...], preferred_element_type=F32)
                    + bin_ref[...])
    u = z[:, :GMLP_INNER]
    v = _rms(z[:, GMLP_INNER:], vg_ref[...]).astype(BF16)
    for blk in range(ts // GMLP_BLOCK):
        rows = slice(blk * GMLP_BLOCK, (blk + 1) * GMLP_BLOCK)
        for hd in range(GMLP_HEADS):
            cols = slice(hd * GMLP_HEAD_DIM, (hd + 1) * GMLP_HEAD_DIM)
            sv = jnp.dot(ws_ref[hd], v[rows, cols], preferred_element_type=F32)
            sv = sv + bst_ref[:, hd:hd + 1]
            gated_ref[rows, cols] = (u[rows, cols] * sv).astype(BF16)
    out = jnp.dot(gated_ref[...], wout_ref[...], preferred_element_type=F32) + bout_ref[...]
    hnew = h + out
    h_out_ref[...] = hnew
    _route_tail(hnew, gffn_ref, wr_ref, br_ref, run_ref, hn_out_ref, route_ref, route_t_ref,
                cnt_ref)


def _expert_body(be_ref, nused_ref, rows_ref, wg_ref, wu_ref, wd_ref, y_ref, wgu_s, wd_s):
    b = pl.program_id(0)
    prev = be_ref[jnp.maximum(b - 1, 0)]
    changed = (b == 0) | (be_ref[b] != prev)

    @pl.when(changed)
    def _():
        wgu_s[:, :D_EXPERT] = wg_ref[0].astype(BF16)
        wgu_s[:, D_EXPERT:] = wu_ref[0].astype(BF16)
        wd_s[...] = wd_ref[0].astype(BF16)

    @pl.when(b < nused_ref[0])
    def _():
        lo, hi = _unpack_bf16_pairs(rows_ref[...])
        half = lo.shape[1]
        gu = (jnp.dot(lo, wgu_s[:half, :], preferred_element_type=F32)
              + jnp.dot(hi, wgu_s[half:, :], preferred_element_type=F32))
        g = gu[:, :D_EXPERT]
        hb = (g * jax.nn.sigmoid(g)) * gu[:, D_EXPERT:]
        y_ref[...] = jnp.dot(hb.astype(BF16), wd_s[...], preferred_element_type=F32)


def _final_body(h_ref, g1_ref, g2_ref, rprev_ref, gfin_ref, o_ref):
    rp = rprev_ref[...]
    h = (h_ref[...] + rp[:, R_GATE1:R_GATE1 + 1] * g1_ref[...]
         + rp[:, R_GATE2:R_GATE2 + 1] * g2_ref[...])
    o_ref[...] = _rms(h, gfin_ref[...])


def _const_spec(shape):
    return pl.BlockSpec(shape, lambda i: (0,) * len(shape))


def _row_spec(ts, width):
    return pl.BlockSpec((ts, width), lambda i: (i, 0))


def _layer_out(t, d, ts):
    shapes = [jax.ShapeDtypeStruct((t, d), F32),
              jax.ShapeDtypeStruct((t, d // 2), jnp.int32),
              jax.ShapeDtypeStruct((t, LANES), F32),
              jax.ShapeDtypeStruct((SUBLANES, t), F32),
              jax.ShapeDtypeStruct((SUBLANES, LANES), F32)]
    specs = [_row_spec(ts, d), _row_spec(ts, d // 2), _row_spec(ts, LANES),
             pl.BlockSpec((SUBLANES, ts), lambda i: (0, i)),
             _const_spec((SUBLANES, LANES))]
    return shapes, specs


def _conv_layer(x2, seq, gmix, wpw1, bpw1, wdw, bdw, lng, lnb, wpw2, bpw2, gffn, wr, br):
    t, d = x2.shape
    ts = TS_CONV
    shapes, out_specs = _layer_out(t, d, ts)
    body = functools.partial(_conv_layer_body, seq // ts)
    return pl.pallas_call(
        body,
        grid=(t // ts,),
        in_specs=[_row_spec(ts, d), _const_spec((1, d)), _const_spec((d, 2 * d)),
                  _const_spec((1, 2 * d)), _const_spec((CONV_WIDTH, d)), _const_spec((1, d)),
                  _const_spec((1, d)), _const_spec((1, d)), _const_spec((d, d)),
                  _const_spec((1, d)), _const_spec((1, d)), _const_spec((d, LANES)),
                  _const_spec((1, LANES))],
        out_specs=out_specs,
        out_shape=shapes,
        scratch_shapes=[pltpu.VMEM((ts + HIST, d), F32),
                        pltpu.VMEM((SUBLANES - 1, ts + HIST - SUBLANES, d), F32),
                        pltpu.VMEM((ts, d), F32),
                        pltpu.VMEM((1, LANES), F32)],
        compiler_params=pltpu.CompilerParams(dimension_semantics=("arbitrary",),
                                             vmem_limit_bytes=VMEM_LIMIT),
        name="conv_layer",
    )(x2, gmix, wpw1, bpw1, wdw, bdw, lng, lnb, wpw2, bpw2, gffn, wr, br)


def _gmlp_layer(h, g1, g2, rprev, gmix, win, bin_, vg, ws, bst, wout, bout, gffn, wr, br):
    t, d = h.shape
    ts = TS_GMLP
    shapes, out_specs = _layer_out(t, d, ts)
    return pl.pallas_call(
        _gmlp_layer_body,
        grid=(t // ts,),
        in_specs=[_row_spec(ts, d), _row_spec(ts, d), _row_spec(ts, d), _row_spec(ts, LANES),
                  _const_spec((1, d)), _const_spec((d, 2 * GMLP_INNER)),
                  _const_spec((1, 2 * GMLP_INNER)), _const_spec((1, GMLP_INNER)),
                  _const_spec((GMLP_HEADS, GMLP_BLOCK, GMLP_BLOCK)),
                  _const_spec((GMLP_BLOCK, GMLP_HEADS)), _const_spec((GMLP_INNER, d)),
                  _const_spec((1, d)), _const_spec((1, d)), _const_spec((d, LANES)),
                  _const_spec((1, LANES))],
        out_specs=out_specs,
        out_shape=shapes,
        scratch_shapes=[pltpu.VMEM((ts, GMLP_INNER), BF16),
                        pltpu.VMEM((1, LANES), F32)],
        compiler_params=pltpu.CompilerParams(dimension_semantics=("arbitrary",),
                                             vmem_limit_bytes=VMEM_LIMIT),
        name="gmlp_layer",
    )(h, g1, g2, rprev, gmix, win, bin_, vg, ws, bst, wout, bout, gffn, wr, br)


def _experts(rows, blk_expert, n_used, wg, wu, wd):
    r = rows.shape[0]
    d = wg.shape[1]
    n_blk = r // BM

    def row_map(b, be, nu):
        return (jnp.minimum(b, nu[0] - 1), 0)

    def w_map(b, be, nu):
        return (be[b], 0, 0)

    grid_spec = pltpu.PrefetchScalarGridSpec(
        num_scalar_prefetch=2,
        grid=(n_blk,),
        in_specs=[pl.BlockSpec((BM, d // 2), row_map),
                  pl.BlockSpec((1, d, D_EXPERT), w_map),
                  pl.BlockSpec((1, d, D_EXPERT), w_map),
                  pl.BlockSpec((1, D_EXPERT, d), w_map)],
        out_specs=pl.BlockSpec((BM, d), row_map),
        scratch_shapes=[pltpu.VMEM((d, 2 * D_EXPERT), BF16),
                        pltpu.VMEM((D_EXPERT, d), BF16)],
    )
    return pl.pallas_call(
        _expert_body,
        grid_spec=grid_spec,
        out_shape=jax.ShapeDtypeStruct((r, d), F32),
        compiler_params=pltpu.CompilerParams(dimension_semantics=("arbitrary",),
                                             vmem_limit_bytes=VMEM_LIMIT),
        name="experts",
    )(blk_expert, n_used, rows, wg, wu, wd)


def _final(h, g1, g2, rprev, gfin):
    t, d = h.shape
    ts = TS_FINAL
    return pl.pallas_call(
        _final_body,
        grid=(t // ts,),
        in_specs=[_row_spec(ts, d), _row_spec(ts, d), _row_spec(ts, d), _row_spec(ts, LANES),
                  _const_spec((1, d))],
        out_specs=_row_spec(ts, d),
        out_shape=jax.ShapeDtypeStruct((t, d), F32),
        compiler_params=pltpu.CompilerParams(dimension_semantics=("arbitrary",),
                                             vmem_limit_bytes=VMEM_LIMIT),
        name="final_norm",
    )(h, g1, g2, rprev, gfin)


def _plan_body(n_blk, cnt_ref, rt_ref, d1_ref, d2_ref, be_ref, nu_ref):
    e1 = rt_ref[0:1, :]
    e2 = rt_ref[1:2, :]
    d1 = rt_ref[2:3, :]
    d2 = rt_ref[3:4, :]
    pb = jnp.int32(0)
    last = jnp.int32(0)
    for e in range(N_EXPERTS):
        nb = lax.shift_right_logical(cnt_ref[e] + (BM - 1), BM.bit_length() - 1)
        ps = (pb * BM).astype(F32)
        d1 = d1 + jnp.where(e1 == e, ps, 0.0)
        d2 = d2 + jnp.where(e2 == e, ps, 0.0)

        def fill(j, carry, e=e):
            be_ref[j] = jnp.int32(e)
            return carry

        lax.fori_loop(pb, pb + nb, fill, 0)
        last = jnp.where(nb > 0, jnp.int32(e), last)
        pb = pb + nb
    nu_ref[0] = pb

    def fill_tail(j, carry):
        be_ref[j] = last
        return carry

    lax.fori_loop(pb, n_blk, fill_tail, 0)
    d1_ref[...] = d1.astype(jnp.int32)
    d2_ref[...] = d2.astype(jnp.int32)


def _plan(route_t, counts, n_blk):
    t = route_t.shape[1]
    smem = pl.BlockSpec(memory_space=pltpu.SMEM)
    vmem = pl.BlockSpec(memory_space=pltpu.VMEM)
    return pl.pallas_call(
        functools.partial(_plan_body, n_blk),
        in_specs=[smem, vmem],
        out_specs=[vmem, vmem, smem, smem],
        out_shape=[jax.ShapeDtypeStruct((1, t), jnp.int32),
                   jax.ShapeDtypeStruct((1, t), jnp.int32),
                   jax.ShapeDtypeStruct((n_blk,), jnp.int32),
                   jax.ShapeDtypeStruct((1,), jnp.int32)],
        name="moe_plan",
    )(counts, route_t)


def _sc_workers():
    info = plsc.get_sparse_core_info()
    return info.num_cores, info.num_cores * info.num_subcores


def _sc_mesh():
    return plsc.VectorSubcoreMesh(core_axis_name="c", subcore_axis_name="s")


def _sc_worker_id(num_cores):
    return lax.axis_index("s") * num_cores + lax.axis_index("c")


def _dispatch(hn, dest1, dest2, n_rows):
    t, w = hn.shape
    num_cores, n_workers = _sc_workers()
    per_w = t // n_workers
    chunk = SC_DISPATCH_CHUNK

    def body(hn_hbm, d1_hbm, d2_hbm, rows_hbm, buf, i1, i2):
        base_w = _sc_worker_id(num_cores) * per_w

        @pl.loop(0, per_w // chunk)
        def _(j):
            base = pl.multiple_of(base_w + j * chunk, chunk)
            pltpu.sync_copy(hn_hbm.at[pl.ds(base, chunk)], buf)
            pltpu.sync_copy(d1_hbm.at[:, pl.ds(base, chunk)], i1)
            pltpu.sync_copy(d2_hbm.at[:, pl.ds(base, chunk)], i2)
            pltpu.sync_copy(buf, rows_hbm.at[i1.at[0]])
            pltpu.sync_copy(buf, rows_hbm.at[i2.at[0]])

    return pl.kernel(
        body,
        out_type=jax.ShapeDtypeStruct((n_rows, w), hn.dtype),
        mesh=_sc_mesh(),
        scratch_types=[pltpu.VMEM((chunk, w), hn.dtype),
                       pltpu.VMEM((1, chunk), jnp.int32),
                       pltpu.VMEM((1, chunk), jnp.int32)],
        name="moe_dispatch",
    )(hn, dest1, dest2)


def _combine_gather(y, dest1, dest2):
    d = y.shape[1]
    t = dest1.shape[1]
    num_cores, n_workers = _sc_workers()
    per_w = t // n_workers
    chunk = SC_COMBINE_CHUNK

    def body(y_hbm, d1_hbm, d2_hbm, g1_hbm, g2_hbm, buf, idx):
        base_w = _sc_worker_id(num_cores) * per_w

        @pl.loop(0, per_w // LANES)
        def _(j):
            base = pl.multiple_of(base_w + j * LANES, LANES)
            for d_hbm, g_hbm in ((d1_hbm, g1_hbm), (d2_hbm, g2_hbm)):
                pltpu.sync_copy(d_hbm.at[:, pl.ds(base, LANES)], idx)
                for c0 in range(0, LANES, chunk):
                    pltpu.sync_copy(y_hbm.at[idx.at[0, pl.ds(c0, chunk)]], buf)
                    pltpu.sync_copy(buf, g_hbm.at[pl.ds(base + c0, chunk)])

    out = jax.ShapeDtypeStruct((t, d), y.dtype)
    return pl.kernel(
        body,
        out_type=(out, out),
        mesh=_sc_mesh(),
        scratch_types=[pltpu.VMEM((chunk, d), y.dtype),
                       pltpu.VMEM((1, LANES), jnp.int32)],
        name="moe_combine_gather",
    )(y, dest1, dest2)


def _moe(hn, route_t, cnt, wg, wu, wd):
    t = hn.shape[0]
    n_blk = (2 * t) // BM + N_EXPERTS
    counts = cnt[0, :N_EXPERTS].astype(jnp.int32)
    dest1, dest2, blk_expert, n_used = _plan(route_t, counts, n_blk)
    rows = _dispatch(hn, dest1, dest2, n_blk * BM)
    y = _experts(rows, blk_expert, n_used, wg, wu, wd)
    return _combine_gather(y, dest1, dest2)


def _router_weights(w_group, b_group, w_expert, b_expert):
    d = w_group.shape[0]
    wr = jnp.zeros((d, LANES), F32)
    wr = wr.at[:, :N_EXPERTS].set(w_expert).at[:, GROUP_LANE0:GROUP_LANE0 + N_GROUPS].set(w_group)
    br = jnp.zeros((1, LANES), F32)
    br = br.at[0, :N_EXPERTS].set(b_expert).at[0, GROUP_LANE0:GROUP_LANE0 + N_GROUPS].set(b_group)
    return wr, br


def kernel(x, norm_mix_g, norm_ffn_g, cv_w_pw1, cv_b_pw1, cv_w_dw, cv_b_dw, cv_ln_g, cv_ln_b, cv_w_pw2, cv_b_pw2, gm_w_in, gm_b_in, gm_v_norm_g, gm_w_s, gm_b_s, gm_w_out, gm_b_out, moe_w_group, moe_b_group, moe_w_expert, moe_b_expert, moe_w_gate, moe_w_up, moe_w_down, final_g):
    bsz, seq, d = x.shape
    t = bsz * seq
    x2 = x.reshape(t, d)
    row = lambda a: a.reshape(1, -1)

    wr0, br0 = _router_weights(moe_w_group[0], moe_b_group[0], moe_w_expert[0], moe_b_expert[0])
    h1, hn1, route0, route_t0, cnt0 = _conv_layer(
        x2, seq, row(norm_mix_g[0]), cv_w_pw1[0].astype(BF16), row(cv_b_pw1[0]), cv_w_dw[0],
        row(cv_b_dw[0]), row(cv_ln_g[0]), row(cv_ln_b[0]), cv_w_pw2[0].astype(BF16),
        row(cv_b_pw2[0]), row(norm_ffn_g[0]), wr0, br0)
    ga0, gb0 = _moe(hn1, route_t0, cnt0,moe_w_gate[0], moe_w_up[0], moe_w_down[0])

    idx = jnp.arange(GMLP_BLOCK)
    mask = (idx[None, :] // GMLP_CHUNK) <= (idx[:, None] // GMLP_CHUNK)
    ws = jnp.where(mask[None], gm_w_s[0], 0.0).astype(BF16)
    wr1, br1 = _router_weights(moe_w_group[1], moe_b_group[1], moe_w_expert[1], moe_b_expert[1])
    h2, hn2, route1, route_t1, cnt1 = _gmlp_layer(
        h1, ga0, gb0, route0, row(norm_mix_g[1]), gm_w_in[0].astype(BF16), row(gm_b_in[0]),
        row(gm_v_norm_g[0]), ws, jnp.transpose(gm_b_s[0]), gm_w_out[0].astype(BF16),
        row(gm_b_out[0]), row(norm_ffn_g[1]), wr1, br1)
    ga1, gb1 = _moe(hn2, route_t1, cnt1,moe_w_gate[1], moe_w_up[1], moe_w_down[1])

    out = _final(h2, ga1, gb1, route1, row(final_g))
    return out.reshape(bsz, seq, d)
```

```python
import functools

import jax
import jax.numpy as jnp
from jax import lax
from jax.experimental import pallas as pl
from jax.experimental.pallas import tpu as pltpu
from jax.experimental.pallas import tpu_sc as plsc

D_MODEL = 1024
CONV_WIDTH = 31
GMLP_BLOCK = 128
GMLP_CHUNK = 64
GMLP_INNER = 2 * D_MODEL
GMLP_HEADS = 8
GMLP_HEAD_DIM = GMLP_INNER // GMLP_HEADS
N_GROUPS = 4
EXPERTS_PER_GROUP = 8
N_EXPERTS = N_GROUPS * EXPERTS_PER_GROUP
D_EXPERT = D_MODEL // 2
EPS = 1e-6

LANES = 128
SUBLANES = 8
HIST = 32
TS_CONV = 512
TS_GMLP = 512
TS_FINAL = 512
BM = 256
RC = 32
CONV_CW = 256
GLU_CW = 256
GMLP_CW = 512
GROUP_LANE0 = N_EXPERTS
VMEM_LIMIT = 56 * 1024 * 1024
SC_DISPATCH_CHUNK = 128
SC_COMBINE_CHUNK = 64

R_E1, R_E2, R_RANK1, R_RANK2, R_GATE1, R_GATE2 = range(6)

F32 = jnp.float32
BF16 = jnp.bfloat16


def _rms(xf, g):
    return xf * lax.rsqrt(jnp.mean(xf * xf, axis=-1, keepdims=True) + EPS) * g


def _unpack_bf16_pairs(p):
    lo = lax.bitcast_convert_type(lax.shift_left(p, 16), F32).astype(BF16)
    hi = lax.bitcast_convert_type(p & jnp.int32(-65536), F32).astype(BF16)
    return lo, hi


def _row_loop(n_rows, fn, unroll=True):
    def step(ci, carry):
        fn(pl.ds(pl.multiple_of(ci * RC, RC), RC))
        return carry

    lax.fori_loop(0, n_rows // RC, step, 0, unroll=unroll)


def _route_tail(h_ref, gffn_ref, wrc_ref, wrh_ref, br_ref, tril_ref, run_ref,
                hn_out_ref, route_ref, route_t_ref, cnt_ref, hi_s, lo_s):
    ts = h_ref.shape[0]
    w = h_ref.shape[1] // 2

    def norm_rows(rows):
        hn2 = _rms(h_ref[rows, :], gffn_ref[...])
        hi = hn2.astype(BF16)
        hf = hi.astype(F32)
        hi_s[rows, :] = hi
        lo_s[rows, :] = (hn2 - hf).astype(BF16)
        bits = lax.bitcast_convert_type(hf, jnp.int32)
        hn_out_ref[rows, :] = lax.shift_right_logical(bits[:, :w], 16) | bits[:, w:]

    _row_loop(ts, norm_rows)

    a = jnp.dot(hi_s[...], wrc_ref[...], preferred_element_type=F32)
    logits = (a[:, :LANES] + a[:, LANES:]
              + jnp.dot(lo_s[...], wrh_ref[...], preferred_element_type=F32) + br_ref[...])
    lane = lax.broadcasted_iota(jnp.int32, (ts, LANES), 1).astype(F32)
    ninf = jnp.float32(-jnp.inf)
    big = jnp.float32(1e9)
    is_g = (lane >= GROUP_LANE0) & (lane < GROUP_LANE0 + N_GROUPS)
    lg = jnp.where(is_g, logits, ninf)
    gmax = jnp.max(lg, axis=-1, keepdims=True)
    g_lane = jnp.min(jnp.where(lg == gmax, lane, big), axis=-1, keepdims=True)
    p_g = 1.0 / jnp.sum(jnp.where(is_g, jnp.exp(lg - gmax), 0.0), axis=-1, keepdims=True)
    lo = (g_lane - GROUP_LANE0) * EXPERTS_PER_GROUP
    in_grp = (lane >= lo) & (lane < lo + EXPERTS_PER_GROUP)
    le = jnp.where(in_grp, logits, ninf)
    v1 = jnp.max(le, axis=-1, keepdims=True)
    i1 = jnp.min(jnp.where(le == v1, lane, big), axis=-1, keepdims=True)
    oh1 = lane == i1
    le2 = jnp.where(oh1, ninf, le)
    v2 = jnp.max(le2, axis=-1, keepdims=True)
    i2 = jnp.min(jnp.where(le2 == v2, lane, big), axis=-1, keepdims=True)
    oh2 = lane == i2
    e = jnp.exp(v2 - v1)
    den = 1.0 + e
    gate1 = p_g * (1.0 / den)
    gate2 = p_g * (e / den)

    oh = jnp.where(oh1 | oh2, 1.0, 0.0)
    c = jnp.dot(tril_ref[...], oh.astype(BF16), preferred_element_type=F32) + run_ref[...]
    rank1 = jnp.sum(jnp.where(oh1, c, 0.0), axis=-1, keepdims=True)
    rank2 = jnp.sum(jnp.where(oh2, c, 0.0), axis=-1, keepdims=True)
    run = run_ref[...] + jnp.sum(oh, axis=0, keepdims=True)
    run_ref[...] = run
    cnt_ref[...] = jnp.broadcast_to(run, cnt_ref.shape)

    rec = jnp.where(lane == R_E1, i1, 0.0)
    rec = jnp.where(lane == R_E2, i2, rec)
    rec = jnp.where(lane == R_RANK1, rank1, rec)
    rec = jnp.where(lane == R_RANK2, rank2, rec)
    rec = jnp.where(lane == R_GATE1, gate1, rec)
    rec = jnp.where(lane == R_GATE2, gate2, rec)
    route_ref[...] = rec
    route_t_ref[...] = rec.T[:SUBLANES, :]


def _conv_layer_body(tiles_per_seq,
                     x_ref, gmix_ref, wpw1_ref, bpw1_ref, wdw_ref, bdw_ref, lng_ref, lnb_ref,
                     wpw2_ref, bpw2_ref, gffn_ref, wrc_ref, wrh_ref, br_ref, tril_ref,
                     h_out_ref, hn_out_ref, route_ref, route_t_ref, cnt_ref,
                     hn_s, zext_ref, zs_ref, y_s, a_s, hi_s, lo_s, run_ref):
    i = pl.program_id(0)
    ts, d = x_ref.shape

    @pl.when(i == 0)
    def _():
        run_ref[...] = jnp.zeros_like(run_ref)

    @pl.when(i % tiles_per_seq == 0)
    def _():
        zext_ref[0:HIST, :] = jnp.zeros((HIST, d), F32)

    def norm_rows(rows):
        hn_s[rows, :] = _rms(x_ref[rows, :], gmix_ref[...]).astype(BF16)

    _row_loop(ts, norm_rows)

    hn = hn_s[...]
    for c0 in range(0, d, GLU_CW):
        ca = slice(c0, c0 + GLU_CW)
        cg = slice(d + c0, d + c0 + GLU_CW)
        pa = jnp.dot(hn, wpw1_ref[:, ca], preferred_element_type=F32) + bpw1_ref[:, ca]
        pg = jnp.dot(hn, wpw1_ref[:, cg], preferred_element_type=F32) + bpw1_ref[:, cg]
        zext_ref[HIST:HIST + ts, ca] = pa * jax.nn.sigmoid(pg)

    span = ts + HIST - SUBLANES
    for r in range(1, SUBLANES):
        zs_ref[r - 1, 0:span, :] = zext_ref[r:r + span, :]

    first = HIST - (CONV_WIDTH - 1)

    def conv_rows(rows):
        r0 = rows.start
        groups = RC // SUBLANES
        for c0 in range(0, d, CONV_CW):
            cols = slice(c0, c0 + CONV_CW)
            accs = [bdw_ref[:, cols]] * groups
            for k in range(CONV_WIDTH):
                q, r = divmod(first + k, SUBLANES)
                w8 = wdw_ref[k, :, cols]
                for g in range(groups):
                    src = pl.ds(r0 + (q + g) * SUBLANES, SUBLANES)
                    slab = zext_ref[src, cols] if r == 0 else zs_ref[r - 1, src, cols]
                    accs[g] = accs[g] + w8 * slab
            for g in range(groups):
                y_s[pl.ds(r0 + g * SUBLANES, SUBLANES), cols] = accs[g]
        y = y_s[rows, :]
        mu = jnp.mean(y, axis=-1, keepdims=True)
        yc = y - mu
        yn = yc * lax.rsqrt(jnp.mean(yc * yc, axis=-1, keepdims=True) + EPS)
        yn = yn * lng_ref[...] + lnb_ref[...]
        a_s[rows, :] = (yn * jax.nn.sigmoid(yn)).astype(BF16)

    _row_loop(ts, conv_rows, unroll=2)
    zext_ref[0:HIST, :] = zext_ref[ts:ts + HIST, :]

    m = jnp.dot(a_s[...], wpw2_ref[...], preferred_element_type=F32) + bpw2_ref[...]
    h_out_ref[...] = x_ref[...] + m
    _route_tail(h_out_ref, gffn_ref, wrc_ref, wrh_ref, br_ref, tril_ref, run_ref,
                hn_out_ref, route_ref, route_t_ref, cnt_ref, hi_s, lo_s)


def _gmlp_layer_body(h_ref, g1_ref, g2_ref, rprev_ref, gmix_ref, win_ref, bin_ref, vg_ref,
                     ws_ref, bst_ref, wout_ref, bout_ref, gffn_ref, wrc_ref, wrh_ref, br_ref,
                     tril_ref,
                     h_out_ref, hn_out_ref, route_ref, route_t_ref, cnt_ref,
                     hn_s, u_s, v_s, gated_s, hi_s, lo_s, run_ref):
    i = pl.program_id(0)
    ts = h_ref.shape[0]

    @pl.when(i == 0)
    def _():
        run_ref[...] = jnp.zeros_like(run_ref)

    def norm_rows(rows):
        rp = rprev_ref[rows, :]
        h = (h_ref[rows, :] + rp[:, R_GATE1:R_GATE1 + 1] * g1_ref[rows, :]
             + rp[:, R_GATE2:R_GATE2 + 1] * g2_ref[rows, :])
        h_out_ref[rows, :] = h
        hn_s[rows, :] = _rms(h, gmix_ref[...]).astype(BF16)

    _row_loop(ts, norm_rows)

    hn = hn_s[...]
    ssq = jnp.zeros((ts, 1), F32)
    for c0 in range(0, 2 * GMLP_INNER, GMLP_CW):
        cols = slice(c0, c0 + GMLP_CW)
        zc = jax.nn.gelu(jnp.dot(hn, win_ref[:, cols], preferred_element_type=F32)
                         + bin_ref[:, cols])
        if c0 < GMLP_INNER:
            u_s[:, cols] = zc
        else:
            v_s[:, c0 - GMLP_INNER:c0 - GMLP_INNER + GMLP_CW] = zc
            ssq = ssq + jnp.sum(zc * zc, axis=-1, keepdims=True)
    rs = lax.rsqrt(ssq * (1.0 / GMLP_INNER) + EPS)

    for blk in range(ts // GMLP_BLOCK):
        rows = slice(blk * GMLP_BLOCK, (blk + 1) * GMLP_BLOCK)
        for hd in range(GMLP_HEADS):
            cols = slice(hd * GMLP_HEAD_DIM, (hd + 1) * GMLP_HEAD_DIM)
            vv = (v_s[rows, cols] * rs[rows] * vg_ref[:, cols]).astype(BF16)
            sv = jnp.dot(ws_ref[hd], vv, preferred_element_type=F32) + bst_ref[:, hd:hd + 1]
            gated_s[rows, cols] = (u_s[rows, cols] * sv).astype(BF16)

    out = jnp.dot(gated_s[...], wout_ref[...], preferred_element_type=F32) + bout_ref[...]
    h_out_ref[...] = h_out_ref[...] + out
    _route_tail(h_out_ref, gffn_ref, wrc_ref, wrh_ref, br_ref, tril_ref, run_ref,
                hn_out_ref, route_ref, route_t_ref, cnt_ref, hi_s, lo_s)


def _expert_body(be_ref, nused_ref, rows_ref, wg_ref, wu_ref, wd_ref, y_ref, wgu_s, wd_s):
    b = pl.program_id(0)
    prev = be_ref[jnp.maximum(b - 1, 0)]
    changed = (b == 0) | (be_ref[b] != prev)

    @pl.when(changed)
    def _():
        wgu_s[:, :D_EXPERT] = wg_ref[0, 0].astype(BF16)
        wgu_s[:, D_EXPERT:] = wu_ref[0, 0].astype(BF16)
        wd_s[...] = wd_ref[0, 0].astype(BF16)

    @pl.when(b < nused_ref[0])
    def _():
        lo, hi = _unpack_bf16_pairs(rows_ref[...])
        half = lo.shape[1]
        gu = (jnp.dot(lo, wgu_s[:half, :], preferred_element_type=F32)
              + jnp.dot(hi, wgu_s[half:, :], preferred_element_type=F32))
        g = gu[:, :D_EXPERT]
        hb = (g * jax.nn.sigmoid(g)) * gu[:, D_EXPERT:]
        y_ref[...] = jnp.dot(hb.astype(BF16), wd_s[...], preferred_element_type=F32)


def _final_body(h_ref, g1_ref, g2_ref, rprev_ref, gfin_ref, o_ref):
    rp = rprev_ref[...]
    h = (h_ref[...] + rp[:, R_GATE1:R_GATE1 + 1] * g1_ref[...]
         + rp[:, R_GATE2:R_GATE2 + 1] * g2_ref[...])
    o_ref[...] = _rms(h, gfin_ref[...])


def _const_spec(shape):
    return pl.BlockSpec(shape, lambda i: (0,) * len(shape), pipeline_mode=pl.Buffered(1))


def _row_spec(ts, width):
    return pl.BlockSpec((ts, width), lambda i: (i, 0))


def _layer_out(t, d, ts):
    shapes = [jax.ShapeDtypeStruct((t, d), F32),
              jax.ShapeDtypeStruct((t, d // 2), jnp.int32),
              jax.ShapeDtypeStruct((t, LANES), F32),
              jax.ShapeDtypeStruct((SUBLANES, t), F32),
              jax.ShapeDtypeStruct((SUBLANES, LANES), F32)]
    specs = [_row_spec(ts, d), _row_spec(ts, d // 2), _row_spec(ts, LANES),
             pl.BlockSpec((SUBLANES, ts), lambda i: (0, i)),
             pl.BlockSpec((SUBLANES, LANES), lambda i: (0, 0))]
    return shapes, specs


def _router_specs(d, ts):
    return [_const_spec((1, d)), _const_spec((d, 2 * LANES)), _const_spec((d, LANES)),
            _const_spec((1, LANES)), _const_spec((ts, ts))]


def _router_scratch(ts, d):
    return [pltpu.VMEM((ts, d), BF16), pltpu.VMEM((ts, d), BF16), pltpu.VMEM((1, LANES), F32)]


def _layer_params():
    return pltpu.CompilerParams(dimension_semantics=("arbitrary",), vmem_limit_bytes=VMEM_LIMIT)


def _conv_layer(x2, seq, gmix, wpw1, bpw1, wdw, bdw, lng, lnb, wpw2, bpw2, router):
    t, d = x2.shape
    ts = TS_CONV
    shapes, out_specs = _layer_out(t, d, ts)
    body = functools.partial(_conv_layer_body, seq // ts)
    return pl.pallas_call(
        body,
        grid=(t // ts,),
        in_specs=[_row_spec(ts, d), _const_spec((1, d)), _const_spec((d, 2 * d)),
                  _const_spec((1, 2 * d)), _const_spec((CONV_WIDTH, SUBLANES, d)),
                  _const_spec((SUBLANES, d)),
                  _const_spec((1, d)), _const_spec((1, d)), _const_spec((d, d)),
                  _const_spec((1, d))] + _router_specs(d, ts),
        out_specs=out_specs,
        out_shape=shapes,
        scratch_shapes=[pltpu.VMEM((ts, d), BF16),
                        pltpu.VMEM((ts + HIST, d), F32),
                        pltpu.VMEM((SUBLANES - 1, ts + HIST - SUBLANES, d), F32),
                        pltpu.VMEM((ts, d), F32),
                        pltpu.VMEM((ts, d), BF16)] + _router_scratch(ts, d),
        compiler_params=_layer_params(),
        name="conv_layer",
    )(x2, gmix, wpw1, bpw1, wdw, bdw, lng, lnb, wpw2, bpw2, *router)


def _gmlp_layer(h, g1, g2, rprev, gmix, win, bin_, vg, ws, bst, wout, bout, router):
    t, d = h.shape
    ts = TS_GMLP
    shapes, out_specs = _layer_out(t, d, ts)
    return pl.pallas_call(
        _gmlp_layer_body,
        grid=(t // ts,),
        in_specs=[_row_spec(ts, d), _row_spec(ts, d), _row_spec(ts, d), _row_spec(ts, LANES),
                  _const_spec((1, d)), _const_spec((d, 2 * GMLP_INNER)),
                  _const_spec((1, 2 * GMLP_INNER)), _const_spec((1, GMLP_INNER)),
                  _const_spec((GMLP_HEADS, GMLP_BLOCK, GMLP_BLOCK)),
                  _const_spec((GMLP_BLOCK, GMLP_HEADS)), _const_spec((GMLP_INNER, d)),
                  _const_spec((1, d))] + _router_specs(d, ts),
        out_specs=out_specs,
        out_shape=shapes,
        scratch_shapes=[pltpu.VMEM((ts, d), BF16),
                        pltpu.VMEM((ts, GMLP_INNER), F32),
                        pltpu.VMEM((ts, GMLP_INNER), F32),
                        pltpu.VMEM((ts, GMLP_INNER), BF16)] + _router_scratch(ts, d),
        compiler_params=_layer_params(),
        name="gmlp_layer",
    )(h, g1, g2, rprev, gmix, win, bin_, vg, ws, bst, wout, bout, *router)


def _experts(rows, blk_expert, n_used, wg, wu, wd, layer):
    r = rows.shape[0]
    d = wg.shape[2]
    n_blk = r // BM

    def row_map(b, be, nu):
        return (jnp.maximum(jnp.minimum(b, nu[0] - 1), 0), 0)

    def w_map(b, be, nu):
        return (layer, be[b], 0, 0)

    grid_spec = pltpu.PrefetchScalarGridSpec(
        num_scalar_prefetch=2,
        grid=(n_blk,),
        in_specs=[pl.BlockSpec((BM, d // 2), row_map),
                  pl.BlockSpec((1, 1, d, D_EXPERT), w_map),
                  pl.BlockSpec((1, 1, d, D_EXPERT), w_map),
                  pl.BlockSpec((1, 1, D_EXPERT, d), w_map)],
        out_specs=pl.BlockSpec((BM, d), row_map),
        scratch_shapes=[pltpu.VMEM((d, 2 * D_EXPERT), BF16),
                        pltpu.VMEM((D_EXPERT, d), BF16)],
    )
    return pl.pallas_call(
        _expert_body,
        grid_spec=grid_spec,
        out_shape=jax.ShapeDtypeStruct((r, d), F32),
        compiler_params=_layer_params(),
        name="experts",
    )(blk_expert, n_used, rows, wg, wu, wd)


def _final(h, g1, g2, rprev, gfin):
    t, d = h.shape
    ts = TS_FINAL
    return pl.pallas_call(
        _final_body,
        grid=(t // ts,),
        in_specs=[_row_spec(ts, d), _row_spec(ts, d), _row_spec(ts, d), _row_spec(ts, LANES),
                  _const_spec((1, d))],
        out_specs=_row_spec(ts, d),
        out_shape=jax.ShapeDtypeStruct((t, d), F32),
        compiler_params=_layer_params(),
        name="final_norm",
    )(h, g1, g2, rprev, gfin)


def _plan_body(n_blk, cnt_ref, rt_ref, d1_ref, d2_ref, be_ref, nu_ref):
    e1 = rt_ref[R_E1:R_E1 + 1, :]
    e2 = rt_ref[R_E2:R_E2 + 1, :]
    d1 = rt_ref[R_RANK1:R_RANK1 + 1, :]
    d2 = rt_ref[R_RANK2:R_RANK2 + 1, :]
    pb = jnp.int32(0)
    last = jnp.int32(0)
    for e in range(N_EXPERTS):
        nb = lax.shift_right_logical(cnt_ref[e] + (BM - 1), BM.bit_length() - 1)
        ps = (pb * BM).astype(F32)
        d1 = d1 + jnp.where(e1 == e, ps, 0.0)
        d2 = d2 + jnp.where(e2 == e, ps, 0.0)

        def fill(j, carry, e=e):
            be_ref[j] = jnp.int32(e)
            return carry

        lax.fori_loop(pb, pb + nb, fill, 0)
        last = jnp.where(nb > 0, jnp.int32(e), last)
        pb = pb + nb
    nu_ref[0] = pb

    def fill_tail(j, carry):
        be_ref[j] = last
        return carry

    lax.fori_loop(pb, n_blk, fill_tail, 0)
    d1_ref[...] = d1.astype(jnp.int32)
    d2_ref[...] = d2.astype(jnp.int32)


def _plan(route_t, counts, n_blk):
    t = route_t.shape[1]
    smem = pl.BlockSpec(memory_space=pltpu.SMEM)
    vmem = pl.BlockSpec(memory_space=pltpu.VMEM)
    return pl.pallas_call(
        functools.partial(_plan_body, n_blk),
        in_specs=[smem, vmem],
        out_specs=[vmem, vmem, smem, smem],
        out_shape=[jax.ShapeDtypeStruct((1, t), jnp.int32),
                   jax.ShapeDtypeStruct((1, t), jnp.int32),
                   jax.ShapeDtypeStruct((n_blk,), jnp.int32),
                   jax.ShapeDtypeStruct((1,), jnp.int32)],
        name="moe_plan",
    )(counts, route_t)


def _sc_workers():
    info = plsc.get_sparse_core_info()
    return info.num_cores, info.num_cores * info.num_subcores


def _sc_mesh():
    return plsc.VectorSubcoreMesh(core_axis_name="c", subcore_axis_name="s")


def _sc_worker_id(num_cores):
    return lax.axis_index("s") * num_cores + lax.axis_index("c")


def _dispatch(hn, dest1, dest2, n_rows):
    t, w = hn.shape
    num_cores, n_workers = _sc_workers()
    per_w = t // n_workers
    chunk = SC_DISPATCH_CHUNK

    def body(hn_hbm, d1_hbm, d2_hbm, rows_hbm, buf, i1, i2):
        base_w = _sc_worker_id(num_cores) * per_w

        @pl.loop(0, per_w // chunk)
        def _(j):
            base = pl.multiple_of(base_w + j * chunk, chunk)
            pltpu.sync_copy(hn_hbm.at[pl.ds(base, chunk)], buf)
            pltpu.sync_copy(d1_hbm.at[:, pl.ds(base, chunk)], i1)
            pltpu.sync_copy(d2_hbm.at[:, pl.ds(base, chunk)], i2)
            pltpu.sync_copy(buf, rows_hbm.at[i1.at[0]])
            pltpu.sync_copy(buf, rows_hbm.at[i2.at[0]])

    return pl.kernel(
        body,
        out_type=jax.ShapeDtypeStruct((n_rows, w), hn.dtype),
        mesh=_sc_mesh(),
        scratch_types=[pltpu.VMEM((chunk, w), hn.dtype),
                       pltpu.VMEM((1, chunk), jnp.int32),
                       pltpu.VMEM((1, chunk), jnp.int32)],
        name="moe_dispatch",
    )(hn, dest1, dest2)


def _combine_gather(y, dest1, dest2):
    d = y.shape[1]
    t = dest1.shape[1]
    num_cores, n_workers = _sc_workers()
    per_w = t // n_workers
    chunk = SC_COMBINE_CHUNK

    def body(y_hbm, d1_hbm, d2_hbm, g1_hbm, g2_hbm, buf, idx):
        base_w = _sc_worker_id(num_cores) * per_w

        @pl.loop(0, per_w // LANES)
        def _(j):
            base = pl.multiple_of(base_w + j * LANES, LANES)
            for d_hbm, g_hbm in ((d1_hbm, g1_hbm), (d2_hbm, g2_hbm)):
                pltpu.sync_copy(d_hbm.at[:, pl.ds(base, LANES)], idx)
                for c0 in range(0, LANES, chunk):
                    pltpu.sync_copy(y_hbm.at[idx.at[0, pl.ds(c0, chunk)]], buf)
                    pltpu.sync_copy(buf, g_hbm.at[pl.ds(base + c0, chunk)])

    out = jax.ShapeDtypeStruct((t, d), y.dtype)
    return pl.kernel(
        body,
        out_type=(out, out),
        mesh=_sc_mesh(),
        scratch_types=[pltpu.VMEM((chunk, d), y.dtype),
                       pltpu.VMEM((1, LANES), jnp.int32)],
        name="moe_combine_gather",
    )(y, dest1, dest2)


def _moe(hn, route_t, cnt, wg, wu, wd, layer):
    t = hn.shape[0]
    n_blk = (2 * t) // BM + N_EXPERTS
    counts = cnt[0, :N_EXPERTS].astype(jnp.int32)
    dest1, dest2, blk_expert, n_used = _plan(route_t, counts, n_blk)
    rows = _dispatch(hn, dest1, dest2, n_blk * BM)
    y = _experts(rows, blk_expert, n_used, wg, wu, wd, layer)
    return _combine_gather(y, dest1, dest2)


def _router_inputs(gffn, w_group, b_group, w_expert, b_expert, ts):
    d = w_group.shape[0]
    wr = jnp.zeros((d, LANES), F32)
    wr = wr.at[:, :N_EXPERTS].set(w_expert).at[:, GROUP_LANE0:GROUP_LANE0 + N_GROUPS].set(w_group)
    br = jnp.zeros((1, LANES), F32)
    br = br.at[0, :N_EXPERTS].set(b_expert).at[0, GROUP_LANE0:GROUP_LANE0 + N_GROUPS].set(b_group)
    w_hi = wr.astype(BF16)
    w_lo = (wr - w_hi.astype(F32)).astype(BF16)
    idx = jnp.arange(ts)
    tril = (idx[None, :] < idx[:, None]).astype(BF16)
    return (gffn.reshape(1, -1), jnp.concatenate([w_hi, w_lo], axis=1), w_hi, br, tril)


def kernel(x, norm_mix_g, norm_ffn_g, cv_w_pw1, cv_b_pw1, cv_w_dw, cv_b_dw, cv_ln_g, cv_ln_b, cv_w_pw2, cv_b_pw2, gm_w_in, gm_b_in, gm_v_norm_g, gm_w_s, gm_b_s, gm_w_out, gm_b_out, moe_w_group, moe_b_group, moe_w_expert, moe_b_expert, moe_w_gate, moe_w_up, moe_w_down, final_g):
    bsz, seq, d = x.shape
    t = bsz * seq
    x2 = x.reshape(t, d)
    row = lambda a: a.reshape(1, -1)

    router0 = _router_inputs(norm_ffn_g[0], moe_w_group[0], moe_b_group[0], moe_w_expert[0],
                             moe_b_expert[0], TS_CONV)
    h1, hn1, route0, route_t0, cnt0 = _conv_layer(
        x2, seq, row(norm_mix_g[0]), cv_w_pw1[0].astype(BF16), row(cv_b_pw1[0]),
        jnp.broadcast_to(cv_w_dw[0][:, None, :], (CONV_WIDTH, SUBLANES, d)),
        jnp.broadcast_to(cv_b_dw[0][None, :], (SUBLANES, d)),
        row(cv_ln_g[0]), row(cv_ln_b[0]), cv_w_pw2[0].astype(BF16),
        row(cv_b_pw2[0]), router0)
    ga0, gb0 = _moe(hn1, route_t0, cnt0, moe_w_gate, moe_w_up, moe_w_down, 0)

    idx = jnp.arange(GMLP_BLOCK)
    mask = (idx[None, :] // GMLP_CHUNK) <= (idx[:, None] // GMLP_CHUNK)
    ws = jnp.where(mask[None], gm_w_s[0], 0.0).astype(BF16)
    router1 = _router_inputs(norm_ffn_g[1], moe_w_group[1], moe_b_group[1], moe_w_expert[1],
                             moe_b_expert[1], TS_GMLP)
    h2, hn2, route1, route_t1, cnt1 = _gmlp_layer(
        h1, ga0, gb0, route0, row(norm_mix_g[1]), gm_w_in[0].astype(BF16), row(gm_b_in[0]),
        row(gm_v_norm_g[0]), ws, jnp.transpose(gm_b_s[0]), gm_w_out[0].astype(BF16),
        row(gm_b_out[0]), router1)
    ga1, gb1 = _moe(hn2, route_t1, cnt1, moe_w_gate, moe_w_up, moe_w_down, 1)

    out = _final(h2, ga1, gb1, route1, row(final_g))
    return out.reshape(bsz, seq, d)
```

```python
import functools

import jax
import jax.numpy as jnp
from jax import lax
from jax.experimental import pallas as pl
from jax.experimental.pallas import tpu as pltpu
from jax.experimental.pallas import tpu_sc as plsc

D_MODEL = 1024
CONV_WIDTH = 31
GMLP_BLOCK = 128
GMLP_CHUNK = 64
GMLP_INNER = 2 * D_MODEL
GMLP_HEADS = 8
GMLP_HEAD_DIM = GMLP_INNER // GMLP_HEADS
N_GROUPS = 4
EXPERTS_PER_GROUP = 8
N_EXPERTS = N_GROUPS * EXPERTS_PER_GROUP
D_EXPERT = D_MODEL // 2
EPS = 1e-6

LANES = 128
SUBLANES = 8
HIST = 32
TS_CONV = 512
TS_GMLP = 512
TS_FINAL = 512
BM = 256
RC = 32
CONV_CW = 256
GLU_CW = 256
GMLP_CW = 512
GROUP_LANE0 = N_EXPERTS
VMEM_LIMIT = 56 * 1024 * 1024
SC_DISPATCH_CHUNK = 128
SC_COMBINE_CHUNK = 128

R_E1, R_E2, R_RANK1, R_RANK2, R_GATE1, R_GATE2 = range(6)

F32 = jnp.float32
BF16 = jnp.bfloat16


def _rms(xf, g):
    return xf * lax.rsqrt(jnp.mean(xf * xf, axis=-1, keepdims=True) + EPS) * g


def _pack_bf16_pairs(xb):
    w = xb.shape[1] // 2
    bits = lax.bitcast_convert_type(xb.astype(F32), jnp.int32)
    return lax.shift_right_logical(bits[:, :w], 16) | bits[:, w:]


def _unpack_pairs_f32(p):
    lo = lax.bitcast_convert_type(lax.shift_left(p, 16), F32)
    hi = lax.bitcast_convert_type(p & jnp.int32(-65536), F32)
    return lo, hi


def _moe_combine(h, rp, p1, p2):
    w = h.shape[1] // 2
    g1 = rp[:, R_GATE1:R_GATE1 + 1]
    g2 = rp[:, R_GATE2:R_GATE2 + 1]
    lo1, hi1 = _unpack_pairs_f32(p1)
    lo2, hi2 = _unpack_pairs_f32(p2)
    return jnp.concatenate([h[:, :w] + g1 * lo1 + g2 * lo2,
                            h[:, w:] + g1 * hi1 + g2 * hi2], axis=1)


def _row_loop(n_rows, fn, unroll=True):
    def step(ci, carry):
        fn(pl.ds(pl.multiple_of(ci * RC, RC), RC))
        return carry

    lax.fori_loop(0, n_rows // RC, step, 0, unroll=unroll)


def _route_tail(h_ref, gffn_ref, wrc_ref, wrh_ref, br_ref, tril_ref, run_ref,
                hn_out_ref, route_ref, route_t_ref, cnt_ref, hi_s, lo_s):
    ts = h_ref.shape[0]
    w = h_ref.shape[1] // 2

    def norm_rows(rows):
        hn2 = _rms(h_ref[rows, :], gffn_ref[...])
        hi = hn2.astype(BF16)
        hf = hi.astype(F32)
        hi_s[rows, :] = hi
        lo_s[rows, :] = (hn2 - hf).astype(BF16)
        hn_out_ref[rows, :] = _pack_bf16_pairs(hi)

    _row_loop(ts, norm_rows)

    a = jnp.dot(hi_s[...], wrc_ref[...], preferred_element_type=F32)
    logits = (a[:, :LANES] + a[:, LANES:]
              + jnp.dot(lo_s[...], wrh_ref[...], preferred_element_type=F32) + br_ref[...])
    lane = lax.broadcasted_iota(jnp.int32, (ts, LANES), 1).astype(F32)
    ninf = jnp.float32(-jnp.inf)
    big = jnp.float32(1e9)
    is_g = (lane >= GROUP_LANE0) & (lane < GROUP_LANE0 + N_GROUPS)
    lg = jnp.where(is_g, logits, ninf)
    gmax = jnp.max(lg, axis=-1, keepdims=True)
    g_lane = jnp.min(jnp.where(lg == gmax, lane, big), axis=-1, keepdims=True)
    p_g = 1.0 / jnp.sum(jnp.where(is_g, jnp.exp(lg - gmax), 0.0), axis=-1, keepdims=True)
    lo = (g_lane - GROUP_LANE0) * EXPERTS_PER_GROUP
    in_grp = (lane >= lo) & (lane < lo + EXPERTS_PER_GROUP)
    le = jnp.where(in_grp, logits, ninf)
    v1 = jnp.max(le, axis=-1, keepdims=True)
    i1 = jnp.min(jnp.where(le == v1, lane, big), axis=-1, keepdims=True)
    oh1 = lane == i1
    le2 = jnp.where(oh1, ninf, le)
    v2 = jnp.max(le2, axis=-1, keepdims=True)
    i2 = jnp.min(jnp.where(le2 == v2, lane, big), axis=-1, keepdims=True)
    oh2 = lane == i2
    e = jnp.exp(v2 - v1)
    den = 1.0 + e
    gate1 = p_g * (1.0 / den)
    gate2 = p_g * (e / den)

    oh = jnp.where(oh1 | oh2, 1.0, 0.0)
    c = jnp.dot(tril_ref[...], oh.astype(BF16), preferred_element_type=F32) + run_ref[...]
    rank1 = jnp.sum(jnp.where(oh1, c, 0.0), axis=-1, keepdims=True)
    rank2 = jnp.sum(jnp.where(oh2, c, 0.0), axis=-1, keepdims=True)
    run = run_ref[...] + jnp.sum(oh, axis=0, keepdims=True)
    run_ref[...] = run
    cnt_ref[...] = jnp.broadcast_to(run, cnt_ref.shape)

    rec = jnp.where(lane == R_E1, i1, 0.0)
    rec = jnp.where(lane == R_E2, i2, rec)
    rec = jnp.where(lane == R_RANK1, rank1, rec)
    rec = jnp.where(lane == R_RANK2, rank2, rec)
    rec = jnp.where(lane == R_GATE1, gate1, rec)
    rec = jnp.where(lane == R_GATE2, gate2, rec)
    route_ref[...] = rec
    route_t_ref[...] = rec.T[:SUBLANES, :]


def _conv_layer_body(tiles_per_seq,
                     x_ref, gmix_ref, wpw1_ref, bpw1_ref, wdw_ref, bdw_ref, lng_ref, lnb_ref,
                     wpw2_ref, bpw2_ref, gffn_ref, wrc_ref, wrh_ref, br_ref, tril_ref,
                     h_out_ref, hn_out_ref, route_ref, route_t_ref, cnt_ref,
                     hn_s, zext_ref, zs_ref, y_s, a_s, hi_s, lo_s, run_ref):
    i = pl.program_id(0)
    ts, d = x_ref.shape

    @pl.when(i == 0)
    def _():
        run_ref[...] = jnp.zeros_like(run_ref)

    @pl.when(i % tiles_per_seq == 0)
    def _():
        zext_ref[0:HIST, :] = jnp.zeros((HIST, d), F32)

    def norm_rows(rows):
        hn_s[rows, :] = _rms(x_ref[rows, :], gmix_ref[...]).astype(BF16)

    _row_loop(ts, norm_rows)

    hn = hn_s[...]
    for c0 in range(0, d, GLU_CW):
        ca = slice(c0, c0 + GLU_CW)
        cg = slice(d + c0, d + c0 + GLU_CW)
        pa = jnp.dot(hn, wpw1_ref[:, ca], preferred_element_type=F32) + bpw1_ref[:, ca]
        pg = jnp.dot(hn, wpw1_ref[:, cg], preferred_element_type=F32) + bpw1_ref[:, cg]
        zext_ref[HIST:HIST + ts, ca] = pa * jax.nn.sigmoid(pg)

    span = ts + HIST - SUBLANES
    for r in range(1, SUBLANES):
        zs_ref[r - 1, 0:span, :] = zext_ref[r:r + span, :]

    first = HIST - (CONV_WIDTH - 1)

    def conv_rows(rows):
        r0 = rows.start
        groups = RC // SUBLANES
        for c0 in range(0, d, CONV_CW):
            cols = slice(c0, c0 + CONV_CW)
            accs = [bdw_ref[:, cols]] * groups
            for k in range(CONV_WIDTH):
                q, r = divmod(first + k, SUBLANES)
                w8 = wdw_ref[k, :, cols]
                for g in range(groups):
                    src = pl.ds(r0 + (q + g) * SUBLANES, SUBLANES)
                    slab = zext_ref[src, cols] if r == 0 else zs_ref[r - 1, src, cols]
                    accs[g] = accs[g] + w8 * slab
            for g in range(groups):
                y_s[pl.ds(r0 + g * SUBLANES, SUBLANES), cols] = accs[g]
        y = y_s[rows, :]
        mu = jnp.mean(y, axis=-1, keepdims=True)
        yc = y - mu
        yn = yc * lax.rsqrt(jnp.mean(yc * yc, axis=-1, keepdims=True) + EPS)
        yn = yn * lng_ref[...] + lnb_ref[...]
        a_s[rows, :] = (yn * jax.nn.sigmoid(yn)).astype(BF16)

    _row_loop(ts, conv_rows, unroll=2)
    zext_ref[0:HIST, :] = zext_ref[ts:ts + HIST, :]

    m = jnp.dot(a_s[...], wpw2_ref[...], preferred_element_type=F32) + bpw2_ref[...]
    h_out_ref[...] = x_ref[...] + m
    _route_tail(h_out_ref, gffn_ref, wrc_ref, wrh_ref, br_ref, tril_ref, run_ref,
                hn_out_ref, route_ref, route_t_ref, cnt_ref, hi_s, lo_s)


def _gmlp_layer_body(h_ref, g1_ref, g2_ref, rprev_ref, gmix_ref, win_ref, bin_ref, vg_ref,
                     ws_ref, bst_ref, wout_ref, bout_ref, gffn_ref, wrc_ref, wrh_ref, br_ref,
                     tril_ref,
                     h_out_ref, hn_out_ref, route_ref, route_t_ref, cnt_ref,
                     hn_s, u_s, v_s, gated_s, hi_s, lo_s, run_ref):
    i = pl.program_id(0)
    ts = h_ref.shape[0]

    @pl.when(i == 0)
    def _():
        run_ref[...] = jnp.zeros_like(run_ref)

    def norm_rows(rows):
        h = _moe_combine(h_ref[rows, :], rprev_ref[rows, :], g1_ref[rows, :], g2_ref[rows, :])
        h_out_ref[rows, :] = h
        hn_s[rows, :] = _rms(h, gmix_ref[...]).astype(BF16)

    _row_loop(ts, norm_rows)

    hn = hn_s[...]
    ssq = jnp.zeros((ts, 1), F32)
    for c0 in range(0, 2 * GMLP_INNER, GMLP_CW):
        cols = slice(c0, c0 + GMLP_CW)
        zc = jax.nn.gelu(jnp.dot(hn, win_ref[:, cols], preferred_element_type=F32)
                         + bin_ref[:, cols])
        if c0 < GMLP_INNER:
            u_s[:, cols] = zc
        else:
            v_s[:, c0 - GMLP_INNER:c0 - GMLP_INNER + GMLP_CW] = zc
            ssq = ssq + jnp.sum(zc * zc, axis=-1, keepdims=True)
    rs = lax.rsqrt(ssq * (1.0 / GMLP_INNER) + EPS)

    for blk in range(ts // GMLP_BLOCK):
        rows = slice(blk * GMLP_BLOCK, (blk + 1) * GMLP_BLOCK)
        for hd in range(GMLP_HEADS):
            cols = slice(hd * GMLP_HEAD_DIM, (hd + 1) * GMLP_HEAD_DIM)
            vv = (v_s[rows, cols] * rs[rows] * vg_ref[:, cols]).astype(BF16)
            sv = jnp.dot(ws_ref[hd], vv, preferred_element_type=F32) + bst_ref[:, hd:hd + 1]
            gated_s[rows, cols] = (u_s[rows, cols] * sv).astype(BF16)

    out = jnp.dot(gated_s[...], wout_ref[...], preferred_element_type=F32) + bout_ref[...]
    h_out_ref[...] = h_out_ref[...] + out
    _route_tail(h_out_ref, gffn_ref, wrc_ref, wrh_ref, br_ref, tril_ref, run_ref,
                hn_out_ref, route_ref, route_t_ref, cnt_ref, hi_s, lo_s)


def _expert_body(be_ref, nused_ref, rows_ref, wg_ref, wu_ref, wd_ref, y_ref, wgu_s, wd_s):
    b = pl.program_id(0)
    prev = be_ref[jnp.maximum(b - 1, 0)]
    changed = (b == 0) | (be_ref[b] != prev)

    @pl.when(changed)
    def _():
        wgu_s[:, :D_EXPERT] = wg_ref[0, 0].astype(BF16)
        wgu_s[:, D_EXPERT:] = wu_ref[0, 0].astype(BF16)
        wd_s[...] = wd_ref[0, 0].astype(BF16)

    @pl.when(b < nused_ref[0])
    def _():
        lo, hi = _unpack_pairs_f32(rows_ref[...])
        half = lo.shape[1]
        gu = (jnp.dot(lo.astype(BF16), wgu_s[:half, :], preferred_element_type=F32)
              + jnp.dot(hi.astype(BF16), wgu_s[half:, :], preferred_element_type=F32))
        g = gu[:, :D_EXPERT]
        hb = (g * jax.nn.sigmoid(g)) * gu[:, D_EXPERT:]
        y = jnp.dot(hb.astype(BF16), wd_s[...], preferred_element_type=F32)
        y_ref[...] = _pack_bf16_pairs(y.astype(BF16))


def _final_body(h_ref, g1_ref, g2_ref, rprev_ref, gfin_ref, o_ref):
    h = _moe_combine(h_ref[...], rprev_ref[...], g1_ref[...], g2_ref[...])
    o_ref[...] = _rms(h, gfin_ref[...])


def _const_spec(shape):
    return pl.BlockSpec(shape, lambda i: (0,) * len(shape), pipeline_mode=pl.Buffered(1))


def _row_spec(ts, width):
    return pl.BlockSpec((ts, width), lambda i: (i, 0))


def _layer_out(t, d, ts):
    shapes = [jax.ShapeDtypeStruct((t, d), F32),
              jax.ShapeDtypeStruct((t, d // 2), jnp.int32),
              jax.ShapeDtypeStruct((t, LANES), F32),
              jax.ShapeDtypeStruct((SUBLANES, t), F32),
              jax.ShapeDtypeStruct((SUBLANES, LANES), F32)]
    specs = [_row_spec(ts, d), _row_spec(ts, d // 2), _row_spec(ts, LANES),
             pl.BlockSpec((SUBLANES, ts), lambda i: (0, i)),
             pl.BlockSpec((SUBLANES, LANES), lambda i: (0, 0))]
    return shapes, specs


def _router_specs(d, ts):
    return [_const_spec((1, d)), _const_spec((d, 2 * LANES)), _const_spec((d, LANES)),
            _const_spec((1, LANES)), _const_spec((ts, ts))]


def _router_scratch(ts, d):
    return [pltpu.VMEM((ts, d), BF16), pltpu.VMEM((ts, d), BF16), pltpu.VMEM((1, LANES), F32)]


def _layer_params():
    return pltpu.CompilerParams(dimension_semantics=("arbitrary",), vmem_limit_bytes=VMEM_LIMIT)


def _conv_layer(x2, seq, gmix, wpw1, bpw1, wdw, bdw, lng, lnb, wpw2, bpw2, router):
    t, d = x2.shape
    ts = TS_CONV
    shapes, out_specs = _layer_out(t, d, ts)
    body = functools.partial(_conv_layer_body, seq // ts)
    return pl.pallas_call(
        body,
        grid=(t // ts,),
        in_specs=[_row_spec(ts, d), _const_spec((1, d)), _const_spec((d, 2 * d)),
                  _const_spec((1, 2 * d)), _const_spec((CONV_WIDTH, SUBLANES, d)),
                  _const_spec((SUBLANES, d)),
                  _const_spec((1, d)), _const_spec((1, d)), _const_spec((d, d)),
                  _const_spec((1, d))] + _router_specs(d, ts),
        out_specs=out_specs,
        out_shape=shapes,
        scratch_shapes=[pltpu.VMEM((ts, d), BF16),
                        pltpu.VMEM((ts + HIST, d), F32),
                        pltpu.VMEM((SUBLANES - 1, ts + HIST - SUBLANES, d), F32),
                        pltpu.VMEM((ts, d), F32),
                        pltpu.VMEM((ts, d), BF16)] + _router_scratch(ts, d),
        compiler_params=_layer_params(),
        name="conv_layer",
    )(x2, gmix, wpw1, bpw1, wdw, bdw, lng, lnb, wpw2, bpw2, *router)


def _gmlp_layer(h, g1, g2, rprev, gmix, win, bin_, vg, ws, bst, wout, bout, router):
    t, d = h.shape
    ts = TS_GMLP
    shapes, out_specs = _layer_out(t, d, ts)
    return pl.pallas_call(
        _gmlp_layer_body,
        grid=(t // ts,),
        in_specs=[_row_spec(ts, d), _row_spec(ts, d // 2), _row_spec(ts, d // 2),
                  _row_spec(ts, LANES),
                  _const_spec((1, d)), _const_spec((d, 2 * GMLP_INNER)),
                  _const_spec((1, 2 * GMLP_INNER)), _const_spec((1, GMLP_INNER)),
                  _const_spec((GMLP_HEADS, GMLP_BLOCK, GMLP_BLOCK)),
                  _const_spec((GMLP_BLOCK, GMLP_HEADS)), _const_spec((GMLP_INNER, d)),
                  _const_spec((1, d))] + _router_specs(d, ts),
        out_specs=out_specs,
        out_shape=shapes,
        scratch_shapes=[pltpu.VMEM((ts, d), BF16),
                        pltpu.VMEM((ts, GMLP_INNER), F32),
                        pltpu.VMEM((ts, GMLP_INNER), F32),
                        pltpu.VMEM((ts, GMLP_INNER), BF16)] + _router_scratch(ts, d),
        compiler_params=_layer_params(),
        name="gmlp_layer",
    )(h, g1, g2, rprev, gmix, win, bin_, vg, ws, bst, wout, bout, *router)


def _experts(rows, blk_expert, n_used, wg, wu, wd, layer):
    r = rows.shape[0]
    d = wg.shape[2]
    n_blk = r // BM

    def row_map(b, be, nu):
        return (jnp.maximum(jnp.minimum(b, nu[0] - 1), 0), 0)

    def w_map(b, be, nu):
        return (layer, be[b], 0, 0)

    grid_spec = pltpu.PrefetchScalarGridSpec(
        num_scalar_prefetch=2,
        grid=(n_blk,),
        in_specs=[pl.BlockSpec((BM, d // 2), row_map),
                  pl.BlockSpec((1, 1, d, D_EXPERT), w_map),
                  pl.BlockSpec((1, 1, d, D_EXPERT), w_map),
                  pl.BlockSpec((1, 1, D_EXPERT, d), w_map)],
        out_specs=pl.BlockSpec((BM, d // 2), row_map),
        scratch_shapes=[pltpu.VMEM((d, 2 * D_EXPERT), BF16),
                        pltpu.VMEM((D_EXPERT, d), BF16)],
    )
    return pl.pallas_call(
        _expert_body,
        grid_spec=grid_spec,
        out_shape=jax.ShapeDtypeStruct((r, d // 2), jnp.int32),
        compiler_params=_layer_params(),
        name="experts",
    )(blk_expert, n_used, rows, wg, wu, wd)


def _final(h, g1, g2, rprev, gfin):
    t, d = h.shape
    ts = TS_FINAL
    return pl.pallas_call(
        _final_body,
        grid=(t // ts,),
        in_specs=[_row_spec(ts, d), _row_spec(ts, d // 2), _row_spec(ts, d // 2),
                  _row_spec(ts, LANES),
                  _const_spec((1, d))],
        out_specs=_row_spec(ts, d),
        out_shape=jax.ShapeDtypeStruct((t, d), F32),
        compiler_params=_layer_params(),
        name="final_norm",
    )(h, g1, g2, rprev, gfin)


def _plan_body(n_blk, cnt_ref, rt_ref, d1_ref, d2_ref, be_ref, nu_ref):
    e1 = rt_ref[R_E1:R_E1 + 1, :]
    e2 = rt_ref[R_E2:R_E2 + 1, :]
    d1 = rt_ref[R_RANK1:R_RANK1 + 1, :]
    d2 = rt_ref[R_RANK2:R_RANK2 + 1, :]
    pb = jnp.int32(0)
    last = jnp.int32(0)
    for e in range(N_EXPERTS):
        nb = lax.shift_right_logical(cnt_ref[e] + (BM - 1), BM.bit_length() - 1)
        ps = (pb * BM).astype(F32)
        d1 = d1 + jnp.where(e1 == e, ps, 0.0)
        d2 = d2 + jnp.where(e2 == e, ps, 0.0)

        def fill(j, carry, e=e):
            be_ref[j] = jnp.int32(e)
            return carry

        lax.fori_loop(pb, pb + nb, fill, 0)
        last = jnp.where(nb > 0, jnp.int32(e), last)
        pb = pb + nb
    nu_ref[0] = pb

    def fill_tail(j, carry):
        be_ref[j] = last
        return carry

    lax.fori_loop(pb, n_blk, fill_tail, 0)
    d1_ref[...] = d1.astype(jnp.int32)
    d2_ref[...] = d2.astype(jnp.int32)


def _plan(route_t, counts, n_blk):
    t = route_t.shape[1]
    smem = pl.BlockSpec(memory_space=pltpu.SMEM)
    vmem = pl.BlockSpec(memory_space=pltpu.VMEM)
    return pl.pallas_call(
        functools.partial(_plan_body, n_blk),
        in_specs=[smem, vmem],
        out_specs=[vmem, vmem, smem, smem],
        out_shape=[jax.ShapeDtypeStruct((1, t), jnp.int32),
                   jax.ShapeDtypeStruct((1, t), jnp.int32),
                   jax.ShapeDtypeStruct((n_blk,), jnp.int32),
                   jax.ShapeDtypeStruct((1,), jnp.int32)],
        name="moe_plan",
    )(counts, route_t)


def _sc_workers():
    info = plsc.get_sparse_core_info()
    return info.num_cores, info.num_cores * info.num_subcores


def _sc_mesh():
    return plsc.VectorSubcoreMesh(core_axis_name="c", subcore_axis_name="s")


def _sc_worker_id(num_cores):
    return lax.axis_index("s") * num_cores + lax.axis_index("c")


def _dispatch(hn, dest1, dest2, n_rows):
    t, w = hn.shape
    num_cores, n_workers = _sc_workers()
    per_w = t // n_workers
    chunk = SC_DISPATCH_CHUNK

    def body(hn_hbm, d1_hbm, d2_hbm, rows_hbm, buf, i1, i2):
        base_w = _sc_worker_id(num_cores) * per_w

        @pl.loop(0, per_w // chunk)
        def _(j):
            base = pl.multiple_of(base_w + j * chunk, chunk)
            pltpu.sync_copy(hn_hbm.at[pl.ds(base, chunk)], buf)
            pltpu.sync_copy(d1_hbm.at[:, pl.ds(base, chunk)], i1)
            pltpu.sync_copy(d2_hbm.at[:, pl.ds(base, chunk)], i2)
            pltpu.sync_copy(buf, rows_hbm.at[i1.at[0]])
            pltpu.sync_copy(buf, rows_hbm.at[i2.at[0]])

    return pl.kernel(
        body,
        out_type=jax.ShapeDtypeStruct((n_rows, w), hn.dtype),
        mesh=_sc_mesh(),
        scratch_types=[pltpu.VMEM((chunk, w), hn.dtype),
                       pltpu.VMEM((1, chunk), jnp.int32),
                       pltpu.VMEM((1, chunk), jnp.int32)],
        name="moe_dispatch",
    )(hn, dest1, dest2)


def _combine_gather(y, dest1, dest2):
    d = y.shape[1]
    t = dest1.shape[1]
    num_cores, n_workers = _sc_workers()
    per_w = t // n_workers
    chunk = SC_COMBINE_CHUNK

    def body(y_hbm, d1_hbm, d2_hbm, g1_hbm, g2_hbm, buf, idx):
        base_w = _sc_worker_id(num_cores) * per_w

        @pl.loop(0, per_w // chunk)
        def _(j):
            base = pl.multiple_of(base_w + j * chunk, chunk)
            for d_hbm, g_hbm in ((d1_hbm, g1_hbm), (d2_hbm, g2_hbm)):
                pltpu.sync_copy(d_hbm.at[:, pl.ds(base, chunk)], idx)
                pltpu.sync_copy(y_hbm.at[idx.at[0]], buf)
                pltpu.sync_copy(buf, g_hbm.at[pl.ds(base, chunk)])

    out = jax.ShapeDtypeStruct((t, d), y.dtype)
    return pl.kernel(
        body,
        out_type=(out, out),
        mesh=_sc_mesh(),
        scratch_types=[pltpu.VMEM((chunk, d), y.dtype),
                       pltpu.VMEM((1, chunk), jnp.int32)],
        name="moe_combine_gather",
    )(y, dest1, dest2)


def _moe(hn, route_t, cnt, wg, wu, wd, layer):
    t = hn.shape[0]
    n_blk = (2 * t) // BM + N_EXPERTS
    counts = cnt[0, :N_EXPERTS].astype(jnp.int32)
    dest1, dest2, blk_expert, n_used = _plan(route_t, counts, n_blk)
    rows = _dispatch(hn, dest1, dest2, n_blk * BM)
    y = _experts(rows, blk_expert, n_used, wg, wu, wd, layer)
    return _combine_gather(y, dest1, dest2)


def _router_inputs(gffn, w_group, b_group, w_expert, b_expert, ts):
    d = w_group.shape[0]
    wr = jnp.zeros((d, LANES), F32)
    wr = wr.at[:, :N_EXPERTS].set(w_expert).at[:, GROUP_LANE0:GROUP_LANE0 + N_GROUPS].set(w_group)
    br = jnp.zeros((1, LANES), F32)
    br = br.at[0, :N_EXPERTS].set(b_expert).at[0, GROUP_LANE0:GROUP_LANE0 + N_GROUPS].set(b_group)
    w_hi = wr.astype(BF16)
    w_lo = (wr - w_hi.astype(F32)).astype(BF16)
    idx = jnp.arange(ts)
    tril = (idx[None, :] < idx[:, None]).astype(BF16)
    return (gffn.reshape(1, -1), jnp.concatenate([w_hi, w_lo], axis=1), w_hi, br, tril)


def kernel(x, norm_mix_g, norm_ffn_g, cv_w_pw1, cv_b_pw1, cv_w_dw, cv_b_dw, cv_ln_g, cv_ln_b, cv_w_pw2, cv_b_pw2, gm_w_in, gm_b_in, gm_v_norm_g, gm_w_s, gm_b_s, gm_w_out, gm_b_out, moe_w_group, moe_b_group, moe_w_expert, moe_b_expert, moe_w_gate, moe_w_up, moe_w_down, final_g):
    bsz, seq, d = x.shape
    t = bsz * seq
    x2 = x.reshape(t, d)
    row = lambda a: a.reshape(1, -1)

    router0 = _router_inputs(norm_ffn_g[0], moe_w_group[0], moe_b_group[0], moe_w_expert[0],
                             moe_b_expert[0], TS_CONV)
    h1, hn1, route0, route_t0, cnt0 = _conv_layer(
        x2, seq, row(norm_mix_g[0]), cv_w_pw1[0].astype(BF16), row(cv_b_pw1[0]),
        jnp.broadcast_to(cv_w_dw[0][:, None, :], (CONV_WIDTH, SUBLANES, d)),
        jnp.broadcast_to(cv_b_dw[0][None, :], (SUBLANES, d)),
        row(cv_ln_g[0]), row(cv_ln_b[0]), cv_w_pw2[0].astype(BF16),
        row(cv_b_pw2[0]), router0)
    ga0, gb0 = _moe(hn1, route_t0, cnt0, moe_w_gate, moe_w_up, moe_w_down, 0)

    idx = jnp.arange(GMLP_BLOCK)
    mask = (idx[None, :] // GMLP_CHUNK) <= (idx[:, None] // GMLP_CHUNK)
    ws = jnp.where(mask[None], gm_w_s[0], 0.0).astype(BF16)
    router1 = _router_inputs(norm_ffn_g[1], moe_w_group[1], moe_b_group[1], moe_w_expert[1],
                             moe_b_expert[1], TS_GMLP)
    h2, hn2, route1, route_t1, cnt1 = _gmlp_layer(
        h1, ga0, gb0, route0, row(norm_mix_g[1]), gm_w_in[0].astype(BF16), row(gm_b_in[0]),
        row(gm_v_norm_g[0]), ws, jnp.transpose(gm_b_s[0]), gm_w_out[0].astype(BF16),
        row(gm_b_out[0]), router1)
    ga1, gb1 = _moe(hn2, route_t1, cnt1, moe_w_gate, moe_w_up, moe_w_down, 1)

    out = _final(h2, ga1, gb1, route1, row(final_g))
    return out.reshape(bsz, seq, d)
```

```python
import functools

import jax
import jax.numpy as jnp
from jax import lax
from jax.experimental import pallas as pl
from jax.experimental.pallas import tpu as pltpu
from jax.experimental.pallas import tpu_sc as plsc

D_MODEL = 1024
CONV_WIDTH = 31
GMLP_BLOCK = 128
GMLP_CHUNK = 64
GMLP_INNER = 2 * D_MODEL
GMLP_HEADS = 8
GMLP_HEAD_DIM = GMLP_INNER // GMLP_HEADS
N_GROUPS = 4
EXPERTS_PER_GROUP = 8
N_EXPERTS = N_GROUPS * EXPERTS_PER_GROUP
D_EXPERT = D_MODEL // 2
EPS = 1e-6

LANES = 128
SUBLANES = 8
HIST = 32
TS_CONV = 512
TS_GMLP = 512
TS_FINAL = 512
BM = 512
RC = 32
CONV_CW = 256
GLU_CW = 256
GMLP_CW = 512
GROUP_LANE0 = N_EXPERTS
VMEM_LIMIT = 56 * 1024 * 1024
SC_DISPATCH_CHUNK = 128
SC_COMBINE_CHUNK = 128

R_E1, R_E2, R_RANK1, R_RANK2, R_GATE1, R_GATE2 = range(6)

F32 = jnp.float32
BF16 = jnp.bfloat16


def _rms(xf, g):
    return xf * lax.rsqrt(jnp.mean(xf * xf, axis=-1, keepdims=True) + EPS) * g


def _pack_bf16_pairs(xb):
    w = xb.shape[1] // 2
    bits = lax.bitcast_convert_type(xb.astype(F32), jnp.int32)
    return lax.shift_right_logical(bits[:, :w], 16) | bits[:, w:]


def _unpack_pairs_f32(p):
    lo = lax.bitcast_convert_type(lax.shift_left(p, 16), F32)
    hi = lax.bitcast_convert_type(p & jnp.int32(-65536), F32)
    return lo, hi


def _moe_combine(h, rp, p1, p2):
    w = h.shape[1] // 2
    g1 = rp[:, R_GATE1:R_GATE1 + 1]
    g2 = rp[:, R_GATE2:R_GATE2 + 1]
    lo1, hi1 = _unpack_pairs_f32(p1)
    lo2, hi2 = _unpack_pairs_f32(p2)
    return jnp.concatenate([h[:, :w] + g1 * lo1 + g2 * lo2,
                            h[:, w:] + g1 * hi1 + g2 * hi2], axis=1)


def _row_loop(n_rows, fn, unroll=True):
    def step(ci, carry):
        fn(pl.ds(pl.multiple_of(ci * RC, RC), RC))
        return carry

    lax.fori_loop(0, n_rows // RC, step, 0, unroll=unroll)


def _route_tail(h_ref, gffn_ref, wrc_ref, wrh_ref, br_ref, tril_ref, run_ref,
                hn_out_ref, route_ref, route_t_ref, cnt_ref, hi_s, lo_s):
    ts = h_ref.shape[0]
    w = h_ref.shape[1] // 2

    def norm_rows(rows):
        hn2 = _rms(h_ref[rows, :], gffn_ref[...])
        hi = hn2.astype(BF16)
        hf = hi.astype(F32)
        hi_s[rows, :] = hi
        lo_s[rows, :] = (hn2 - hf).astype(BF16)
        hn_out_ref[rows, :] = _pack_bf16_pairs(hi)

    _row_loop(ts, norm_rows)

    a = jnp.dot(hi_s[...], wrc_ref[...], preferred_element_type=F32)
    logits = (a[:, :LANES] + a[:, LANES:]
              + jnp.dot(lo_s[...], wrh_ref[...], preferred_element_type=F32) + br_ref[...])
    lane = lax.broadcasted_iota(jnp.int32, (ts, LANES), 1).astype(F32)
    ninf = jnp.float32(-jnp.inf)
    big = jnp.float32(1e9)
    is_g = (lane >= GROUP_LANE0) & (lane < GROUP_LANE0 + N_GROUPS)
    lg = jnp.where(is_g, logits, ninf)
    gmax = jnp.max(lg, axis=-1, keepdims=True)
    g_lane = jnp.min(jnp.where(lg == gmax, lane, big), axis=-1, keepdims=True)
    p_g = 1.0 / jnp.sum(jnp.where(is_g, jnp.exp(lg - gmax), 0.0), axis=-1, keepdims=True)
    lo = (g_lane - GROUP_LANE0) * EXPERTS_PER_GROUP
    in_grp = (lane >= lo) & (lane < lo + EXPERTS_PER_GROUP)
    le = jnp.where(in_grp, logits, ninf)
    v1 = jnp.max(le, axis=-1, keepdims=True)
    i1 = jnp.min(jnp.where(le == v1, lane, big), axis=-1, keepdims=True)
    oh1 = lane == i1
    le2 = jnp.where(oh1, ninf, le)
    v2 = jnp.max(le2, axis=-1, keepdims=True)
    i2 = jnp.min(jnp.where(le2 == v2, lane, big), axis=-1, keepdims=True)
    oh2 = lane == i2
    e = jnp.exp(v2 - v1)
    den = 1.0 + e
    gate1 = p_g * (1.0 / den)
    gate2 = p_g * (e / den)

    oh = jnp.where(oh1 | oh2, 1.0, 0.0)
    c = jnp.dot(tril_ref[...], oh.astype(BF16), preferred_element_type=F32) + run_ref[...]
    rank1 = jnp.sum(jnp.where(oh1, c, 0.0), axis=-1, keepdims=True)
    rank2 = jnp.sum(jnp.where(oh2, c, 0.0), axis=-1, keepdims=True)
    run = run_ref[...] + jnp.sum(oh, axis=0, keepdims=True)
    run_ref[...] = run
    cnt_ref[...] = jnp.broadcast_to(run, cnt_ref.shape)

    rec = jnp.where(lane == R_E1, i1, 0.0)
    rec = jnp.where(lane == R_E2, i2, rec)
    rec = jnp.where(lane == R_RANK1, rank1, rec)
    rec = jnp.where(lane == R_RANK2, rank2, rec)
    rec = jnp.where(lane == R_GATE1, gate1, rec)
    rec = jnp.where(lane == R_GATE2, gate2, rec)
    route_ref[...] = rec
    route_t_ref[...] = rec.T[:SUBLANES, :]


def _conv_layer_body(tiles_per_seq,
                     x_ref, gmix_ref, wpw1_ref, bpw1_ref, wdw_ref, bdw_ref, lng_ref, lnb_ref,
                     wpw2_ref, bpw2_ref, gffn_ref, wrc_ref, wrh_ref, br_ref, tril_ref,
                     h_out_ref, hn_out_ref, route_ref, route_t_ref, cnt_ref,
                     hn_s, zext_ref, zs_ref, y_s, a_s, hi_s, lo_s, run_ref):
    i = pl.program_id(0)
    ts, d = x_ref.shape

    @pl.when(i == 0)
    def _():
        run_ref[...] = jnp.zeros_like(run_ref)

    @pl.when(i % tiles_per_seq == 0)
    def _():
        zext_ref[0:HIST, :] = jnp.zeros((HIST, d), F32)

    def norm_rows(rows):
        hn_s[rows, :] = _rms(x_ref[rows, :], gmix_ref[...]).astype(BF16)

    _row_loop(ts, norm_rows)

    hn = hn_s[...]
    for c0 in range(0, d, GLU_CW):
        ca = slice(c0, c0 + GLU_CW)
        cg = slice(d + c0, d + c0 + GLU_CW)
        pa = jnp.dot(hn, wpw1_ref[:, ca], preferred_element_type=F32) + bpw1_ref[:, ca]
        pg = jnp.dot(hn, wpw1_ref[:, cg], preferred_element_type=F32) + bpw1_ref[:, cg]
        zext_ref[HIST:HIST + ts, ca] = pa * jax.nn.sigmoid(pg)

    span = ts + HIST - SUBLANES
    for r in range(1, SUBLANES):
        zs_ref[r - 1, 0:span, :] = zext_ref[r:r + span, :]

    first = HIST - (CONV_WIDTH - 1)

    def conv_rows(rows):
        r0 = rows.start
        groups = RC // SUBLANES
        for c0 in range(0, d, CONV_CW):
            cols = slice(c0, c0 + CONV_CW)
            accs = [bdw_ref[:, cols]] * groups
            for k in range(CONV_WIDTH):
                q, r = divmod(first + k, SUBLANES)
                w8 = wdw_ref[k, :, cols]
                for g in range(groups):
                    src = pl.ds(r0 + (q + g) * SUBLANES, SUBLANES)
                    slab = zext_ref[src, cols] if r == 0 else zs_ref[r - 1, src, cols]
                    accs[g] = accs[g] + w8 * slab
            for g in range(groups):
                y_s[pl.ds(r0 + g * SUBLANES, SUBLANES), cols] = accs[g]
        y = y_s[rows, :]
        mu = jnp.mean(y, axis=-1, keepdims=True)
        yc = y - mu
        yn = yc * lax.rsqrt(jnp.mean(yc * yc, axis=-1, keepdims=True) + EPS)
        yn = yn * lng_ref[...] + lnb_ref[...]
        a_s[rows, :] = (yn * jax.nn.sigmoid(yn)).astype(BF16)

    _row_loop(ts, conv_rows, unroll=2)
    zext_ref[0:HIST, :] = zext_ref[ts:ts + HIST, :]

    m = jnp.dot(a_s[...], wpw2_ref[...], preferred_element_type=F32) + bpw2_ref[...]
    h_out_ref[...] = x_ref[...] + m
    _route_tail(h_out_ref, gffn_ref, wrc_ref, wrh_ref, br_ref, tril_ref, run_ref,
                hn_out_ref, route_ref, route_t_ref, cnt_ref, hi_s, lo_s)


def _gmlp_layer_body(h_ref, g1_ref, g2_ref, rprev_ref, gmix_ref, win_ref, bin_ref, vg_ref,
                     ws_ref, bst_ref, wout_ref, bout_ref, gffn_ref, wrc_ref, wrh_ref, br_ref,
                     tril_ref,
                     h_out_ref, hn_out_ref, route_ref, route_t_ref, cnt_ref,
                     hn_s, u_s, v_s, gated_s, hi_s, lo_s, run_ref):
    i = pl.program_id(0)
    ts = h_ref.shape[0]

    @pl.when(i == 0)
    def _():
        run_ref[...] = jnp.zeros_like(run_ref)

    def norm_rows(rows):
        h = _moe_combine(h_ref[rows, :], rprev_ref[rows, :], g1_ref[rows, :], g2_ref[rows, :])
        h_out_ref[rows, :] = h
        hn_s[rows, :] = _rms(h, gmix_ref[...]).astype(BF16)

    _row_loop(ts, norm_rows)

    hn = hn_s[...]
    ssq = jnp.zeros((ts, 1), F32)
    for c0 in range(0, 2 * GMLP_INNER, GMLP_CW):
        cols = slice(c0, c0 + GMLP_CW)
        zc = jax.nn.gelu(jnp.dot(hn, win_ref[:, cols], preferred_element_type=F32)
                         + bin_ref[:, cols])
        if c0 < GMLP_INNER:
            u_s[:, cols] = zc
        else:
            v_s[:, c0 - GMLP_INNER:c0 - GMLP_INNER + GMLP_CW] = zc
            ssq = ssq + jnp.sum(zc * zc, axis=-1, keepdims=True)
    rs = lax.rsqrt(ssq * (1.0 / GMLP_INNER) + EPS)

    for blk in range(ts // GMLP_BLOCK):
        rows = slice(blk * GMLP_BLOCK, (blk + 1) * GMLP_BLOCK)
        for hd in range(GMLP_HEADS):
            cols = slice(hd * GMLP_HEAD_DIM, (hd + 1) * GMLP_HEAD_DIM)
            vv = (v_s[rows, cols] * rs[rows] * vg_ref[:, cols]).astype(BF16)
            sv = jnp.dot(ws_ref[hd], vv, preferred_element_type=F32) + bst_ref[:, hd:hd + 1]
            gated_s[rows, cols] = (u_s[rows, cols] * sv).astype(BF16)

    out = jnp.dot(gated_s[...], wout_ref[...], preferred_element_type=F32) + bout_ref[...]
    h_out_ref[...] = h_out_ref[...] + out
    _route_tail(h_out_ref, gffn_ref, wrc_ref, wrh_ref, br_ref, tril_ref, run_ref,
                hn_out_ref, route_ref, route_t_ref, cnt_ref, hi_s, lo_s)


def _expert_body(be_ref, nused_ref, rows_ref, wg_ref, wu_ref, wd_ref, y_ref, wgu_s, wd_s):
    b = pl.program_id(0)
    prev = be_ref[jnp.maximum(b - 1, 0)]
    changed = (b == 0) | (be_ref[b] != prev)

    @pl.when(changed)
    def _():
        wgu_s[:, :D_EXPERT] = wg_ref[0, 0].astype(BF16)
        wgu_s[:, D_EXPERT:] = wu_ref[0, 0].astype(BF16)
        wd_s[...] = wd_ref[0, 0].astype(BF16)

    @pl.when(b < nused_ref[0])
    def _():
        lo, hi = _unpack_pairs_f32(rows_ref[...])
        half = lo.shape[1]
        gu = (jnp.dot(lo.astype(BF16), wgu_s[:half, :], preferred_element_type=F32)
              + jnp.dot(hi.astype(BF16), wgu_s[half:, :], preferred_element_type=F32))
        g = gu[:, :D_EXPERT]
        hb = (g * jax.nn.sigmoid(g)) * gu[:, D_EXPERT:]
        y = jnp.dot(hb.astype(BF16), wd_s[...], preferred_element_type=F32)
        y_ref[...] = _pack_bf16_pairs(y.astype(BF16))


def _final_body(h_ref, g1_ref, g2_ref, rprev_ref, gfin_ref, o_ref):
    h = _moe_combine(h_ref[...], rprev_ref[...], g1_ref[...], g2_ref[...])
    o_ref[...] = _rms(h, gfin_ref[...])


def _const_spec(shape):
    return pl.BlockSpec(shape, lambda i: (0,) * len(shape), pipeline_mode=pl.Buffered(1))


def _row_spec(ts, width):
    return pl.BlockSpec((ts, width), lambda i: (i, 0))


def _layer_out(t, d, ts):
    shapes = [jax.ShapeDtypeStruct((t, d), F32),
              jax.ShapeDtypeStruct((t, d // 2), jnp.int32),
              jax.ShapeDtypeStruct((t, LANES), F32),
              jax.ShapeDtypeStruct((SUBLANES, t), F32),
              jax.ShapeDtypeStruct((SUBLANES, LANES), F32)]
    specs = [_row_spec(ts, d), _row_spec(ts, d // 2), _row_spec(ts, LANES),
             pl.BlockSpec((SUBLANES, ts), lambda i: (0, i)),
             pl.BlockSpec((SUBLANES, LANES), lambda i: (0, 0))]
    return shapes, specs


def _router_specs(d, ts):
    return [_const_spec((1, d)), _const_spec((d, 2 * LANES)), _const_spec((d, LANES)),
            _const_spec((1, LANES)), _const_spec((ts, ts))]


def _router_scratch(ts, d):
    return [pltpu.VMEM((ts, d), BF16), pltpu.VMEM((ts, d), BF16), pltpu.VMEM((1, LANES), F32)]


def _layer_params():
    return pltpu.CompilerParams(dimension_semantics=("arbitrary",), vmem_limit_bytes=VMEM_LIMIT)


def _conv_layer(x2, seq, gmix, wpw1, bpw1, wdw, bdw, lng, lnb, wpw2, bpw2, router):
    t, d = x2.shape
    ts = TS_CONV
    shapes, out_specs = _layer_out(t, d, ts)
    body = functools.partial(_conv_layer_body, seq // ts)
    return pl.pallas_call(
        body,
        grid=(t // ts,),
        in_specs=[_row_spec(ts, d), _const_spec((1, d)), _const_spec((d, 2 * d)),
                  _const_spec((1, 2 * d)), _const_spec((CONV_WIDTH, SUBLANES, d)),
                  _const_spec((SUBLANES, d)),
                  _const_spec((1, d)), _const_spec((1, d)), _const_spec((d, d)),
                  _const_spec((1, d))] + _router_specs(d, ts),
        out_specs=out_specs,
        out_shape=shapes,
        scratch_shapes=[pltpu.VMEM((ts, d), BF16),
                        pltpu.VMEM((ts + HIST, d), F32),
                        pltpu.VMEM((SUBLANES - 1, ts + HIST - SUBLANES, d), F32),
                        pltpu.VMEM((ts, d), F32),
                        pltpu.VMEM((ts, d), BF16)] + _router_scratch(ts, d),
        compiler_params=_layer_params(),
        name="conv_layer",
    )(x2, gmix, wpw1, bpw1, wdw, bdw, lng, lnb, wpw2, bpw2, *router)


def _gmlp_layer(h, g1, g2, rprev, gmix, win, bin_, vg, ws, bst, wout, bout, router):
    t, d = h.shape
    ts = TS_GMLP
    shapes, out_specs = _layer_out(t, d, ts)
    return pl.pallas_call(
        _gmlp_layer_body,
        grid=(t // ts,),
        in_specs=[_row_spec(ts, d), _row_spec(ts, d // 2), _row_spec(ts, d // 2),
                  _row_spec(ts, LANES),
                  _const_spec((1, d)), _const_spec((d, 2 * GMLP_INNER)),
                  _const_spec((1, 2 * GMLP_INNER)), _const_spec((1, GMLP_INNER)),
                  _const_spec((GMLP_HEADS, GMLP_BLOCK, GMLP_BLOCK)),
                  _const_spec((GMLP_BLOCK, GMLP_HEADS)), _const_spec((GMLP_INNER, d)),
                  _const_spec((1, d))] + _router_specs(d, ts),
        out_specs=out_specs,
        out_shape=shapes,
        scratch_shapes=[pltpu.VMEM((ts, d), BF16),
                        pltpu.VMEM((ts, GMLP_INNER), F32),
                        pltpu.VMEM((ts, GMLP_INNER), F32),
                        pltpu.VMEM((ts, GMLP_INNER), BF16)] + _router_scratch(ts, d),
        compiler_params=_layer_params(),
        name="gmlp_layer",
    )(h, g1, g2, rprev, gmix, win, bin_, vg, ws, bst, wout, bout, *router)


def _experts(rows, blk_expert, n_used, wg, wu, wd, layer):
    r = rows.shape[0]
    d = wg.shape[2]
    n_blk = r // BM

    def row_map(b, be, nu):
        return (jnp.maximum(jnp.minimum(b, nu[0] - 1), 0), 0)

    def w_map(b, be, nu):
        return (layer, be[b], 0, 0)

    grid_spec = pltpu.PrefetchScalarGridSpec(
        num_scalar_prefetch=2,
        grid=(n_blk,),
        in_specs=[pl.BlockSpec((BM, d // 2), row_map),
                  pl.BlockSpec((1, 1, d, D_EXPERT), w_map),
                  pl.BlockSpec((1, 1, d, D_EXPERT), w_map),
                  pl.BlockSpec((1, 1, D_EXPERT, d), w_map)],
        out_specs=pl.BlockSpec((BM, d // 2), row_map),
        scratch_shapes=[pltpu.VMEM((d, 2 * D_EXPERT), BF16),
                        pltpu.VMEM((D_EXPERT, d), BF16)],
    )
    return pl.pallas_call(
        _expert_body,
        grid_spec=grid_spec,
        out_shape=jax.ShapeDtypeStruct((r, d // 2), jnp.int32),
        compiler_params=_layer_params(),
        name="experts",
    )(blk_expert, n_used, rows, wg, wu, wd)


def _final(h, g1, g2, rprev, gfin):
    t, d = h.shape
    ts = TS_FINAL
    return pl.pallas_call(
        _final_body,
        grid=(t // ts,),
        in_specs=[_row_spec(ts, d), _row_spec(ts, d // 2), _row_spec(ts, d // 2),
                  _row_spec(ts, LANES),
                  _const_spec((1, d))],
        out_specs=_row_spec(ts, d),
        out_shape=jax.ShapeDtypeStruct((t, d), F32),
        compiler_params=_layer_params(),
        name="final_norm",
    )(h, g1, g2, rprev, gfin)


def _plan_body(n_blk, cnt_ref, rt_ref, d1_ref, d2_ref, be_ref, nu_ref):
    e1 = rt_ref[R_E1:R_E1 + 1, :]
    e2 = rt_ref[R_E2:R_E2 + 1, :]
    d1 = rt_ref[R_RANK1:R_RANK1 + 1, :]
    d2 = rt_ref[R_RANK2:R_RANK2 + 1, :]
    pb = jnp.int32(0)
    last = jnp.int32(0)
    for e in range(N_EXPERTS):
        nb = lax.shift_right_logical(cnt_ref[e] + (BM - 1), BM.bit_length() - 1)
        ps = (pb * BM).astype(F32)
        d1 = d1 + jnp.where(e1 == e, ps, 0.0)
        d2 = d2 + jnp.where(e2 == e, ps, 0.0)

        def fill(j, carry, e=e):
            be_ref[j] = jnp.int32(e)
            return carry

        lax.fori_loop(pb, pb + nb, fill, 0)
        last = jnp.where(nb > 0, jnp.int32(e), last)
        pb = pb + nb
    nu_ref[0] = pb

    def fill_tail(j, carry):
        be_ref[j] = last
        return carry

    lax.fori_loop(pb, n_blk, fill_tail, 0)
    d1_ref[...] = d1.astype(jnp.int32)
    d2_ref[...] = d2.astype(jnp.int32)


def _plan(route_t, counts, n_blk):
    t = route_t.shape[1]
    smem = pl.BlockSpec(memory_space=pltpu.SMEM)
    vmem = pl.BlockSpec(memory_space=pltpu.VMEM)
    return pl.pallas_call(
        functools.partial(_plan_body, n_blk),
        in_specs=[smem, vmem],
        out_specs=[vmem, vmem, smem, smem],
        out_shape=[jax.ShapeDtypeStruct((1, t), jnp.int32),
                   jax.ShapeDtypeStruct((1, t), jnp.int32),
                   jax.ShapeDtypeStruct((n_blk,), jnp.int32),
                   jax.ShapeDtypeStruct((1,), jnp.int32)],
        name="moe_plan",
    )(counts, route_t)


def _sc_workers():
    info = plsc.get_sparse_core_info()
    return info.num_cores, info.num_cores * info.num_subcores


def _sc_mesh():
    return plsc.VectorSubcoreMesh(core_axis_name="c", subcore_axis_name="s")


def _sc_worker_id(num_cores):
    return lax.axis_index("s") * num_cores + lax.axis_index("c")


def _dispatch(hn, dest1, dest2, n_rows):
    t, w = hn.shape
    num_cores, n_workers = _sc_workers()
    per_w = t // n_workers
    chunk = SC_DISPATCH_CHUNK

    def body(hn_hbm, d1_hbm, d2_hbm, rows_hbm, buf, i1, i2):
        base_w = _sc_worker_id(num_cores) * per_w

        @pl.loop(0, per_w // chunk)
        def _(j):
            base = pl.multiple_of(base_w + j * chunk, chunk)
            pltpu.sync_copy(hn_hbm.at[pl.ds(base, chunk)], buf)
            pltpu.sync_copy(d1_hbm.at[:, pl.ds(base, chunk)], i1)
            pltpu.sync_copy(d2_hbm.at[:, pl.ds(base, chunk)], i2)
            pltpu.sync_copy(buf, rows_hbm.at[i1.at[0]])
            pltpu.sync_copy(buf, rows_hbm.at[i2.at[0]])

    return pl.kernel(
        body,
        out_type=jax.ShapeDtypeStruct((n_rows, w), hn.dtype),
        mesh=_sc_mesh(),
        scratch_types=[pltpu.VMEM((chunk, w), hn.dtype),
                       pltpu.VMEM((1, chunk), jnp.int32),
                       pltpu.VMEM((1, chunk), jnp.int32)],
        name="moe_dispatch",
    )(hn, dest1, dest2)


def _combine_gather(y, dest1, dest2):
    d = y.shape[1]
    t = dest1.shape[1]
    num_cores, n_workers = _sc_workers()
    per_w = t // n_workers
    chunk = SC_COMBINE_CHUNK

    def body(y_hbm, d1_hbm, d2_hbm, g1_hbm, g2_hbm, buf, idx):
        base_w = _sc_worker_id(num_cores) * per_w

        @pl.loop(0, per_w // chunk)
        def _(j):
            base = pl.multiple_of(base_w + j * chunk, chunk)
            for d_hbm, g_hbm in ((d1_hbm, g1_hbm), (d2_hbm, g2_hbm)):
                pltpu.sync_copy(d_hbm.at[:, pl.ds(base, chunk)], idx)
                pltpu.sync_copy(y_hbm.at[idx.at[0]], buf)
                pltpu.sync_copy(buf, g_hbm.at[pl.ds(base, chunk)])

    out = jax.ShapeDtypeStruct((t, d), y.dtype)
    return pl.kernel(
        body,
        out_type=(out, out),
        mesh=_sc_mesh(),
        scratch_types=[pltpu.VMEM((chunk, d), y.dtype),
                       pltpu.VMEM((1, chunk), jnp.int32)],
        name="moe_combine_gather",
    )(y, dest1, dest2)


def _moe(hn, route_t, cnt, wg, wu, wd, layer):
    t = hn.shape[0]
    n_blk = (2 * t) // BM + N_EXPERTS
    counts = cnt[0, :N_EXPERTS].astype(jnp.int32)
    dest1, dest2, blk_expert, n_used = _plan(route_t, counts, n_blk)
    rows = _dispatch(hn, dest1, dest2, n_blk * BM)
    y = _experts(rows, blk_expert, n_used, wg, wu, wd, layer)
    return _combine_gather(y, dest1, dest2)


def _router_inputs(gffn, w_group, b_group, w_expert, b_expert, ts):
    d = w_group.shape[0]
    wr = jnp.zeros((d, LANES), F32)
    wr = wr.at[:, :N_EXPERTS].set(w_expert).at[:, GROUP_LANE0:GROUP_LANE0 + N_GROUPS].set(w_group)
    br = jnp.zeros((1, LANES), F32)
    br = br.at[0, :N_EXPERTS].set(b_expert).at[0, GROUP_LANE0:GROUP_LANE0 + N_GROUPS].set(b_group)
    w_hi = wr.astype(BF16)
    w_lo = (wr - w_hi.astype(F32)).astype(BF16)
    idx = jnp.arange(ts)
    tril = (idx[None, :] < idx[:, None]).astype(BF16)
    return (gffn.reshape(1, -1), jnp.concatenate([w_hi, w_lo], axis=1), w_hi, br, tril)


def kernel(x, norm_mix_g, norm_ffn_g, cv_w_pw1, cv_b_pw1, cv_w_dw, cv_b_dw, cv_ln_g, cv_ln_b, cv_w_pw2, cv_b_pw2, gm_w_in, gm_b_in, gm_v_norm_g, gm_w_s, gm_b_s, gm_w_out, gm_b_out, moe_w_group, moe_b_group, moe_w_expert, moe_b_expert, moe_w_gate, moe_w_up, moe_w_down, final_g):
    bsz, seq, d = x.shape
    t = bsz * seq
    x2 = x.reshape(t, d)
    row = lambda a: a.reshape(1, -1)

    router0 = _router_inputs(norm_ffn_g[0], moe_w_group[0], moe_b_group[0], moe_w_expert[0],
                             moe_b_expert[0], TS_CONV)
    h1, hn1, route0, route_t0, cnt0 = _conv_layer(
        x2, seq, row(norm_mix_g[0]), cv_w_pw1[0].astype(BF16), row(cv_b_pw1[0]),
        jnp.broadcast_to(cv_w_dw[0][:, None, :], (CONV_WIDTH, SUBLANES, d)),
        jnp.broadcast_to(cv_b_dw[0][None, :], (SUBLANES, d)),
        row(cv_ln_g[0]), row(cv_ln_b[0]), cv_w_pw2[0].astype(BF16),
        row(cv_b_pw2[0]), router0)
    ga0, gb0 = _moe(hn1, route_t0, cnt0, moe_w_gate, moe_w_up, moe_w_down, 0)

    idx = jnp.arange(GMLP_BLOCK)
    mask = (idx[None, :] // GMLP_CHUNK) <= (idx[:, None] // GMLP_CHUNK)
    ws = jnp.where(mask[None], gm_w_s[0], 0.0).astype(BF16)
    router1 = _router_inputs(norm_ffn_g[1], moe_w_group[1], moe_b_group[1], moe_w_expert[1],
                             moe_b_expert[1], TS_GMLP)
    h2, hn2, route1, route_t1, cnt1 = _gmlp_layer(
        h1, ga0, gb0, route0, row(norm_mix_g[1]), gm_w_in[0].astype(BF16), row(gm_b_in[0]),
        row(gm_v_norm_g[0]), ws, jnp.transpose(gm_b_s[0]), gm_w_out[0].astype(BF16),
        row(gm_b_out[0]), router1)
    ga1, gb1 = _moe(hn2, route_t1, cnt1, moe_w_gate, moe_w_up, moe_w_down, 1)

    out = _final(h2, ga1, gb1, route1, row(final_g))
    return out.reshape(bsz, seq, d)
```

```python
import functools

import jax
import jax.numpy as jnp
from jax import lax
from jax.experimental import pallas as pl
from jax.experimental.pallas import tpu as pltpu
from jax.experimental.pallas import tpu_sc as plsc

D_MODEL = 1024
CONV_WIDTH = 31
GMLP_BLOCK = 128
GMLP_CHUNK = 64
GMLP_INNER = 2 * D_MODEL
GMLP_HEADS = 8
GMLP_HEAD_DIM = GMLP_INNER // GMLP_HEADS
N_GROUPS = 4
EXPERTS_PER_GROUP = 8
N_EXPERTS = N_GROUPS * EXPERTS_PER_GROUP
D_EXPERT = D_MODEL // 2
EPS = 1e-6

LANES = 128
SUBLANES = 8
HIST = 32
TS_CONV = 512
TS_GMLP = 512
TS_FINAL = 512
BM = 512
RC = 32
CONV_CW = 256
GLU_CW = 256
GMLP_CW = 512
GMLP_SUB = 512
GROUP_LANE0 = N_EXPERTS
VMEM_LIMIT = 56 * 1024 * 1024
SC_DISPATCH_CHUNK = 128
SC_COMBINE_CHUNK = 128

R_E1, R_E2, R_RANK1, R_RANK2, R_GATE1, R_GATE2 = range(6)

F32 = jnp.float32
BF16 = jnp.bfloat16


def _rms(xf, g):
    return xf * lax.rsqrt(jnp.mean(xf * xf, axis=-1, keepdims=True) + EPS) * g


def _pack_bf16_pairs(xb):
    w = xb.shape[1] // 2
    bits = lax.bitcast_convert_type(xb.astype(F32), jnp.int32)
    return lax.shift_right_logical(bits[:, :w], 16) | bits[:, w:]


def _unpack_pairs_f32(p):
    lo = lax.bitcast_convert_type(lax.shift_left(p, 16), F32)
    hi = lax.bitcast_convert_type(p & jnp.int32(-65536), F32)
    return lo, hi


def _moe_combine(h, rp, p1, p2):
    w = h.shape[1] // 2
    g1 = rp[:, R_GATE1:R_GATE1 + 1]
    g2 = rp[:, R_GATE2:R_GATE2 + 1]
    lo1, hi1 = _unpack_pairs_f32(p1)
    lo2, hi2 = _unpack_pairs_f32(p2)
    return jnp.concatenate([h[:, :w] + g1 * lo1 + g2 * lo2,
                            h[:, w:] + g1 * hi1 + g2 * hi2], axis=1)


def _row_loop(n_rows, fn, unroll=True):
    def step(ci, carry):
        fn(pl.ds(pl.multiple_of(ci * RC, RC), RC))
        return carry

    lax.fori_loop(0, n_rows // RC, step, 0, unroll=unroll)


def _route_tail(h_ref, gffn_ref, wrc_ref, wrh_ref, br_ref, tril_ref, run_ref,
                hn_out_ref, route_ref, route_t_ref, cnt_ref, hi_s, lo_s):
    ts = h_ref.shape[0]
    w = h_ref.shape[1] // 2

    def norm_rows(rows):
        hn2 = _rms(h_ref[rows, :], gffn_ref[...])
        hi = hn2.astype(BF16)
        hf = hi.astype(F32)
        hi_s[rows, :] = hi
        lo_s[rows, :] = (hn2 - hf).astype(BF16)
        hn_out_ref[rows, :] = _pack_bf16_pairs(hi)

    _row_loop(ts, norm_rows)

    a = jnp.dot(hi_s[...], wrc_ref[...], preferred_element_type=F32)
    logits = (a[:, :LANES] + a[:, LANES:]
              + jnp.dot(lo_s[...], wrh_ref[...], preferred_element_type=F32) + br_ref[...])
    lane = lax.broadcasted_iota(jnp.int32, (ts, LANES), 1).astype(F32)
    ninf = jnp.float32(-jnp.inf)
    big = jnp.float32(1e9)
    is_g = (lane >= GROUP_LANE0) & (lane < GROUP_LANE0 + N_GROUPS)
    lg = jnp.where(is_g, logits, ninf)
    gmax = jnp.max(lg, axis=-1, keepdims=True)
    g_lane = jnp.min(jnp.where(lg == gmax, lane, big), axis=-1, keepdims=True)
    p_g = 1.0 / jnp.sum(jnp.where(is_g, jnp.exp(lg - gmax), 0.0), axis=-1, keepdims=True)
    lo = (g_lane - GROUP_LANE0) * EXPERTS_PER_GROUP
    in_grp = (lane >= lo) & (lane < lo + EXPERTS_PER_GROUP)
    le = jnp.where(in_grp, logits, ninf)
    v1 = jnp.max(le, axis=-1, keepdims=True)
    i1 = jnp.min(jnp.where(le == v1, lane, big), axis=-1, keepdims=True)
    oh1 = lane == i1
    le2 = jnp.where(oh1, ninf, le)
    v2 = jnp.max(le2, axis=-1, keepdims=True)
    i2 = jnp.min(jnp.where(le2 == v2, lane, big), axis=-1, keepdims=True)
    oh2 = lane == i2
    e = jnp.exp(v2 - v1)
    den = 1.0 + e
    gate1 = p_g * (1.0 / den)
    gate2 = p_g * (e / den)

    oh = jnp.where(oh1 | oh2, 1.0, 0.0)
    c = jnp.dot(tril_ref[...], oh.astype(BF16), preferred_element_type=F32) + run_ref[...]
    rank1 = jnp.sum(jnp.where(oh1, c, 0.0), axis=-1, keepdims=True)
    rank2 = jnp.sum(jnp.where(oh2, c, 0.0), axis=-1, keepdims=True)
    run = run_ref[...] + jnp.sum(oh, axis=0, keepdims=True)
    run_ref[...] = run
    cnt_ref[...] = jnp.broadcast_to(run, cnt_ref.shape)

    rec = jnp.where(lane == R_E1, i1, 0.0)
    rec = jnp.where(lane == R_E2, i2, rec)
    rec = jnp.where(lane == R_RANK1, rank1, rec)
    rec = jnp.where(lane == R_RANK2, rank2, rec)
    rec = jnp.where(lane == R_GATE1, gate1, rec)
    rec = jnp.where(lane == R_GATE2, gate2, rec)
    route_ref[...] = rec
    route_t_ref[...] = rec.T[:SUBLANES, :]


def _conv_layer_body(tiles_per_seq,
                     x_ref, gmix_ref, wpw1_ref, bpw1_ref, wdw_ref, bdw_ref, lng_ref, lnb_ref,
                     wpw2_ref, bpw2_ref, gffn_ref, wrc_ref, wrh_ref, br_ref, tril_ref,
                     h_out_ref, hn_out_ref, route_ref, route_t_ref, cnt_ref,
                     hn_s, zext_ref, zs_ref, y_s, a_s, hi_s, lo_s, run_ref):
    i = pl.program_id(0)
    ts, d = x_ref.shape

    @pl.when(i == 0)
    def _():
        run_ref[...] = jnp.zeros_like(run_ref)

    @pl.when(i % tiles_per_seq == 0)
    def _():
        zext_ref[0:HIST, :] = jnp.zeros((HIST, d), F32)

    def norm_rows(rows):
        hn_s[rows, :] = _rms(x_ref[rows, :], gmix_ref[...]).astype(BF16)

    _row_loop(ts, norm_rows)

    hn = hn_s[...]
    for c0 in range(0, d, GLU_CW):
        ca = slice(c0, c0 + GLU_CW)
        cg = slice(d + c0, d + c0 + GLU_CW)
        pa = jnp.dot(hn, wpw1_ref[:, ca], preferred_element_type=F32) + bpw1_ref[:, ca]
        pg = jnp.dot(hn, wpw1_ref[:, cg], preferred_element_type=F32) + bpw1_ref[:, cg]
        zext_ref[HIST:HIST + ts, ca] = pa * jax.nn.sigmoid(pg)

    span = ts + HIST - SUBLANES
    sub_id = lax.broadcasted_iota(jnp.int32, (SUBLANES, LANES), 0)
    shifts = range(1, SUBLANES)
    for c0 in range(0, d, LANES):
        cols = slice(c0, c0 + LANES)

        def rotate(g):
            up = {0: g}
            for r in (4, 2, 6, 1, 3, 5, 7):
                step = r & -r
                up[r] = pltpu.roll(up[r - step], SUBLANES - step, axis=0)
            return [up[r] for r in shifts]

        cur = rotate(zext_ref[0:SUBLANES, cols])
        for m0 in range(0, span, SUBLANES):
            nxt = rotate(zext_ref[m0 + SUBLANES:m0 + 2 * SUBLANES, cols])
            for r in shifts:
                zs_ref[r - 1, m0:m0 + SUBLANES, cols] = jnp.where(
                    sub_id < SUBLANES - r, cur[r - 1], nxt[r - 1])
            cur = nxt

    first = HIST - (CONV_WIDTH - 1)

    def conv_rows(rows):
        r0 = rows.start
        groups = RC // SUBLANES
        for c0 in range(0, d, CONV_CW):
            cols = slice(c0, c0 + CONV_CW)
            accs = [bdw_ref[:, cols]] * groups
            for k in range(CONV_WIDTH):
                q, r = divmod(first + k, SUBLANES)
                w8 = wdw_ref[k, :, cols]
                for g in range(groups):
                    src = pl.ds(r0 + (q + g) * SUBLANES, SUBLANES)
                    slab = zext_ref[src, cols] if r == 0 else zs_ref[r - 1, src, cols]
                    accs[g] = accs[g] + w8 * slab
            for g in range(groups):
                y_s[pl.ds(r0 + g * SUBLANES, SUBLANES), cols] = accs[g]
        y = y_s[rows, :]
        mu = jnp.mean(y, axis=-1, keepdims=True)
        yc = y - mu
        yn = yc * lax.rsqrt(jnp.mean(yc * yc, axis=-1, keepdims=True) + EPS)
        yn = yn * lng_ref[...] + lnb_ref[...]
        a_s[rows, :] = (yn * jax.nn.sigmoid(yn)).astype(BF16)

    _row_loop(ts, conv_rows, unroll=True)
    zext_ref[0:HIST, :] = zext_ref[ts:ts + HIST, :]

    for r0 in range(0, ts, ts // 2):
        rows = slice(r0, r0 + ts // 2)
        m = jnp.dot(a_s[rows, :], wpw2_ref[...], preferred_element_type=F32) + bpw2_ref[...]
        h_out_ref[rows, :] = x_ref[rows, :] + m
    _route_tail(h_out_ref, gffn_ref, wrc_ref, wrh_ref, br_ref, tril_ref, run_ref,
                hn_out_ref, route_ref, route_t_ref, cnt_ref, hi_s, lo_s)


def _gmlp_layer_body(h_ref, g1_ref, g2_ref, rprev_ref, gmix_ref, win_ref, bin_ref, vg_ref,
                     ws_ref, bst_ref, wout_ref, bout_ref, gffn_ref, wrc_ref, wrh_ref, br_ref,
                     tril_ref,
                     h_out_ref, hn_out_ref, route_ref, route_t_ref, cnt_ref,
                     hn_s, u_s, v_s, gated_s, hi_s, lo_s, run_ref):
    i = pl.program_id(0)
    ts = h_ref.shape[0]

    @pl.when(i == 0)
    def _():
        run_ref[...] = jnp.zeros_like(run_ref)

    def norm_rows(rows):
        h = _moe_combine(h_ref[rows, :], rprev_ref[rows, :], g1_ref[rows, :], g2_ref[rows, :])
        h_out_ref[rows, :] = h
        hn_s[rows, :] = _rms(h, gmix_ref[...]).astype(BF16)

    _row_loop(ts, norm_rows)

    for s0 in range(0, ts, GMLP_SUB):
        sub = slice(s0, s0 + GMLP_SUB)
        hn = hn_s[sub, :]
        ssq = jnp.zeros((GMLP_SUB, 1), F32)
        for c0 in range(0, 2 * GMLP_INNER, GMLP_CW):
            cols = slice(c0, c0 + GMLP_CW)
            zc = jax.nn.gelu(jnp.dot(hn, win_ref[:, cols], preferred_element_type=F32)
                             + bin_ref[:, cols])
            if c0 < GMLP_INNER:
                u_s[sub, cols] = zc
            else:
                v_s[sub, c0 - GMLP_INNER:c0 - GMLP_INNER + GMLP_CW] = zc
                ssq = ssq + jnp.sum(zc * zc, axis=-1, keepdims=True)
        rs = lax.rsqrt(ssq * (1.0 / GMLP_INNER) + EPS)

        for b0 in range(0, GMLP_SUB, GMLP_BLOCK):
            rows = slice(s0 + b0, s0 + b0 + GMLP_BLOCK)
            for hd in range(GMLP_HEADS):
                cols = slice(hd * GMLP_HEAD_DIM, (hd + 1) * GMLP_HEAD_DIM)
                vv = (v_s[rows, cols] * rs[b0:b0 + GMLP_BLOCK] * vg_ref[:, cols]).astype(BF16)
                sv = (jnp.dot(ws_ref[hd], vv, preferred_element_type=F32)
                      + bst_ref[:, hd:hd + 1])
                gated_s[rows, cols] = (u_s[rows, cols] * sv).astype(BF16)

        out = jnp.dot(gated_s[sub, :], wout_ref[...], preferred_element_type=F32) + bout_ref[...]
        h_out_ref[sub, :] = h_out_ref[sub, :] + out
    _route_tail(h_out_ref, gffn_ref, wrc_ref, wrh_ref, br_ref, tril_ref, run_ref,
                hn_out_ref, route_ref, route_t_ref, cnt_ref, hi_s, lo_s)


def _expert_body(be_ref, nused_ref, rows_ref, wg_ref, wu_ref, wd_ref, y_ref, wgu_s, wd_s):
    b = pl.program_id(0)
    prev = be_ref[jnp.maximum(b - 1, 0)]
    changed = (b == 0) | (be_ref[b] != prev)

    @pl.when(changed)
    def _():
        wgu_s[:, :D_EXPERT] = wg_ref[0, 0].astype(BF16)
        wgu_s[:, D_EXPERT:] = wu_ref[0, 0].astype(BF16)
        wd_s[...] = wd_ref[0, 0].astype(BF16)

    @pl.when(b < nused_ref[0])
    def _():
        lo, hi = _unpack_pairs_f32(rows_ref[...])
        half = lo.shape[1]
        gu = (jnp.dot(lo.astype(BF16), wgu_s[:half, :], preferred_element_type=F32)
              + jnp.dot(hi.astype(BF16), wgu_s[half:, :], preferred_element_type=F32))
        g = gu[:, :D_EXPERT]
        hb = (g * jax.nn.sigmoid(g)) * gu[:, D_EXPERT:]
        y = jnp.dot(hb.astype(BF16), wd_s[...], preferred_element_type=F32)
        y_ref[...] = _pack_bf16_pairs(y.astype(BF16))


def _final_body(h_ref, g1_ref, g2_ref, rprev_ref, gfin_ref, o_ref):
    h = _moe_combine(h_ref[...], rprev_ref[...], g1_ref[...], g2_ref[...])
    o_ref[...] = _rms(h, gfin_ref[...])


def _const_spec(shape):
    return pl.BlockSpec(shape, lambda i: (0,) * len(shape), pipeline_mode=pl.Buffered(1))


def _row_spec(ts, width):
    return pl.BlockSpec((ts, width), lambda i: (i, 0))


def _layer_out(t, d, ts):
    shapes = [jax.ShapeDtypeStruct((t, d), F32),
              jax.ShapeDtypeStruct((t, d // 2), jnp.int32),
              jax.ShapeDtypeStruct((t, LANES), F32),
              jax.ShapeDtypeStruct((SUBLANES, t), F32),
              jax.ShapeDtypeStruct((SUBLANES, LANES), F32)]
    specs = [_row_spec(ts, d), _row_spec(ts, d // 2), _row_spec(ts, LANES),
             pl.BlockSpec((SUBLANES, ts), lambda i: (0, i)),
             pl.BlockSpec((SUBLANES, LANES), lambda i: (0, 0))]
    return shapes, specs


def _router_specs(d, ts):
    return [_const_spec((1, d)), _const_spec((d, 2 * LANES)), _const_spec((d, LANES)),
            _const_spec((1, LANES)), _const_spec((ts, ts))]


def _router_scratch(ts, d):
    return [pltpu.VMEM((ts, d), BF16), pltpu.VMEM((ts, d), BF16), pltpu.VMEM((1, LANES), F32)]


def _layer_params():
    return pltpu.CompilerParams(dimension_semantics=("arbitrary",), vmem_limit_bytes=VMEM_LIMIT)


def _conv_layer(x2, seq, gmix, wpw1, bpw1, wdw, bdw, lng, lnb, wpw2, bpw2, router):
    t, d = x2.shape
    ts = TS_CONV
    shapes, out_specs = _layer_out(t, d, ts)
    body = functools.partial(_conv_layer_body, seq // ts)
    return pl.pallas_call(
        body,
        grid=(t // ts,),
        in_specs=[_row_spec(ts, d), _const_spec((1, d)), _const_spec((d, 2 * d)),
                  _const_spec((1, 2 * d)), _const_spec((CONV_WIDTH, SUBLANES, d)),
                  _const_spec((SUBLANES, d)),
                  _const_spec((1, d)), _const_spec((1, d)), _const_spec((d, d)),
                  _const_spec((1, d))] + _router_specs(d, ts),
        out_specs=out_specs,
        out_shape=shapes,
        scratch_shapes=[pltpu.VMEM((ts, d), BF16),
                        pltpu.VMEM((ts + HIST, d), F32),
                        pltpu.VMEM((SUBLANES - 1, ts + HIST - SUBLANES, d), F32),
                        pltpu.VMEM((ts, d), F32),
                        pltpu.VMEM((ts, d), BF16)] + _router_scratch(ts, d),
        compiler_params=_layer_params(),
        name="conv_layer",
    )(x2, gmix, wpw1, bpw1, wdw, bdw, lng, lnb, wpw2, bpw2, *router)


def _gmlp_layer(h, g1, g2, rprev, gmix, win, bin_, vg, ws, bst, wout, bout, router):
    t, d = h.shape
    ts = TS_GMLP
    shapes, out_specs = _layer_out(t, d, ts)
    return pl.pallas_call(
        _gmlp_layer_body,
        grid=(t // ts,),
        in_specs=[_row_spec(ts, d), _row_spec(ts, d // 2), _row_spec(ts, d // 2),
                  _row_spec(ts, LANES),
                  _const_spec((1, d)), _const_spec((d, 2 * GMLP_INNER)),
                  _const_spec((1, 2 * GMLP_INNER)), _const_spec((1, GMLP_INNER)),
                  _const_spec((GMLP_HEADS, GMLP_BLOCK, GMLP_BLOCK)),
                  _const_spec((GMLP_BLOCK, GMLP_HEADS)), _const_spec((GMLP_INNER, d)),
                  _const_spec((1, d))] + _router_specs(d, ts),
        out_specs=out_specs,
        out_shape=shapes,
        scratch_shapes=[pltpu.VMEM((ts, d), BF16),
                        pltpu.VMEM((ts, GMLP_INNER), F32),
                        pltpu.VMEM((ts, GMLP_INNER), F32),
                        pltpu.VMEM((ts, GMLP_INNER), BF16)] + _router_scratch(ts, d),
        compiler_params=_layer_params(),
        name="gmlp_layer",
    )(h, g1, g2, rprev, gmix, win, bin_, vg, ws, bst, wout, bout, *router)


def _experts(rows, blk_expert, n_used, wg, wu, wd, layer):
    r = rows.shape[0]
    d = wg.shape[2]
    n_blk = r // BM

    def row_map(b, be, nu):
        return (jnp.maximum(jnp.minimum(b, nu[0] - 1), 0), 0)

    def w_map(b, be, nu):
        return (layer, be[b], 0, 0)

    grid_spec = pltpu.PrefetchScalarGridSpec(
        num_scalar_prefetch=2,
        grid=(n_blk,),
        in_specs=[pl.BlockSpec((BM, d // 2), row_map),
                  pl.BlockSpec((1, 1, d, D_EXPERT), w_map),
                  pl.BlockSpec((1, 1, d, D_EXPERT), w_map),
                  pl.BlockSpec((1, 1, D_EXPERT, d), w_map)],
        out_specs=pl.BlockSpec((BM, d // 2), row_map),
        scratch_shapes=[pltpu.VMEM((d, 2 * D_EXPERT), BF16),
                        pltpu.VMEM((D_EXPERT, d), BF16)],
    )
    return pl.pallas_call(
        _expert_body,
        grid_spec=grid_spec,
        out_shape=jax.ShapeDtypeStruct((r, d // 2), jnp.int32),
        compiler_params=_layer_params(),
        name="experts",
    )(blk_expert, n_used, rows, wg, wu, wd)


def _final(h, g1, g2, rprev, gfin):
    t, d = h.shape
    ts = TS_FINAL
    return pl.pallas_call(
        _final_body,
        grid=(t // ts,),
        in_specs=[_row_spec(ts, d), _row_spec(ts, d // 2), _row_spec(ts, d // 2),
                  _row_spec(ts, LANES),
                  _const_spec((1, d))],
        out_specs=_row_spec(ts, d),
        out_shape=jax.ShapeDtypeStruct((t, d), F32),
        compiler_params=_layer_params(),
        name="final_norm",
    )(h, g1, g2, rprev, gfin)


def _plan_body(n_blk, cnt_ref, rt_ref, d1_ref, d2_ref, be_ref, nu_ref):
    e1 = rt_ref[R_E1:R_E1 + 1, :]
    e2 = rt_ref[R_E2:R_E2 + 1, :]
    d1 = rt_ref[R_RANK1:R_RANK1 + 1, :]
    d2 = rt_ref[R_RANK2:R_RANK2 + 1, :]
    pb = jnp.int32(0)
    last = jnp.int32(0)
    for e in range(N_EXPERTS):
        nb = lax.shift_right_logical(cnt_ref[e] + (BM - 1), BM.bit_length() - 1)
        ps = (pb * BM).astype(F32)
        d1 = d1 + jnp.where(e1 == e, ps, 0.0)
        d2 = d2 + jnp.where(e2 == e, ps, 0.0)

        def fill(j, carry, e=e):
            be_ref[j] = jnp.int32(e)
            return carry

        lax.fori_loop(pb, pb + nb, fill, 0)
        last = jnp.where(nb > 0, jnp.int32(e), last)
        pb = pb + nb
    nu_ref[0] = pb

    def fill_tail(j, carry):
        be_ref[j] = last
        return carry

    lax.fori_loop(pb, n_blk, fill_tail, 0)
    d1_ref[...] = d1.astype(jnp.int32)
    d2_ref[...] = d2.astype(jnp.int32)


def _plan(route_t, counts, n_blk):
    t = route_t.shape[1]
    smem = pl.BlockSpec(memory_space=pltpu.SMEM)
    vmem = pl.BlockSpec(memory_space=pltpu.VMEM)
    return pl.pallas_call(
        functools.partial(_plan_body, n_blk),
        in_specs=[smem, vmem],
        out_specs=[vmem, vmem, smem, smem],
        out_shape=[jax.ShapeDtypeStruct((1, t), jnp.int32),
                   jax.ShapeDtypeStruct((1, t), jnp.int32),
                   jax.ShapeDtypeStruct((n_blk,), jnp.int32),
                   jax.ShapeDtypeStruct((1,), jnp.int32)],
        name="moe_plan",
    )(counts, route_t)


def _sc_workers():
    info = plsc.get_sparse_core_info()
    return info.num_cores, info.num_cores * info.num_subcores


def _sc_mesh():
    return plsc.VectorSubcoreMesh(core_axis_name="c", subcore_axis_name="s")


def _sc_worker_id(num_cores):
    return lax.axis_index("s") * num_cores + lax.axis_index("c")


def _dispatch(hn, dest1, dest2, n_rows):
    t, w = hn.shape
    num_cores, n_workers = _sc_workers()
    per_w = t // n_workers
    chunk = SC_DISPATCH_CHUNK

    def body(hn_hbm, d1_hbm, d2_hbm, rows_hbm, buf, i1, i2):
        base_w = _sc_worker_id(num_cores) * per_w

        @pl.loop(0, per_w // chunk)
        def _(j):
            base = pl.multiple_of(base_w + j * chunk, chunk)
            pltpu.sync_copy(hn_hbm.at[pl.ds(base, chunk)], buf)
            pltpu.sync_copy(d1_hbm.at[:, pl.ds(base, chunk)], i1)
            pltpu.sync_copy(d2_hbm.at[:, pl.ds(base, chunk)], i2)
            pltpu.sync_copy(buf, rows_hbm.at[i1.at[0]])
            pltpu.sync_copy(buf, rows_hbm.at[i2.at[0]])

    return pl.kernel(
        body,
        out_type=jax.ShapeDtypeStruct((n_rows, w), hn.dtype),
        mesh=_sc_mesh(),
        scratch_types=[pltpu.VMEM((chunk, w), hn.dtype),
                       pltpu.VMEM((1, chunk), jnp.int32),
                       pltpu.VMEM((1, chunk), jnp.int32)],
        name="moe_dispatch",
    )(hn, dest1, dest2)


def _combine_gather(y, dest1, dest2):
    d = y.shape[1]
    t = dest1.shape[1]
    num_cores, n_workers = _sc_workers()
    per_w = t // n_workers
    chunk = SC_COMBINE_CHUNK

    def body(y_hbm, d1_hbm, d2_hbm, g1_hbm, g2_hbm, buf, idx):
        base_w = _sc_worker_id(num_cores) * per_w

        @pl.loop(0, per_w // chunk)
        def _(j):
            base = pl.multiple_of(base_w + j * chunk, chunk)
            for d_hbm, g_hbm in ((d1_hbm, g1_hbm), (d2_hbm, g2_hbm)):
                pltpu.sync_copy(d_hbm.at[:, pl.ds(base, chunk)], idx)
                pltpu.sync_copy(y_hbm.at[idx.at[0]], buf)
                pltpu.sync_copy(buf, g_hbm.at[pl.ds(base, chunk)])

    out = jax.ShapeDtypeStruct((t, d), y.dtype)
    return pl.kernel(
        body,
        out_type=(out, out),
        mesh=_sc_mesh(),
        scratch_types=[pltpu.VMEM((chunk, d), y.dtype),
                       pltpu.VMEM((1, chunk), jnp.int32)],
        name="moe_combine_gather",
    )(y, dest1, dest2)


def _moe(hn, route_t, cnt, wg, wu, wd, layer):
    t = hn.shape[0]
    n_blk = (2 * t) // BM + N_EXPERTS
    counts = cnt[0, :N_EXPERTS].astype(jnp.int32)
    dest1, dest2, blk_expert, n_used = _plan(route_t, counts, n_blk)
    rows = _dispatch(hn, dest1, dest2, n_blk * BM)
    y = _experts(rows, blk_expert, n_used, wg, wu, wd, layer)
    return _combine_gather(y, dest1, dest2)


def _router_inputs(gffn, w_group, b_group, w_expert, b_expert, ts):
    d = w_group.shape[0]
    wr = jnp.zeros((d, LANES), F32)
    wr = wr.at[:, :N_EXPERTS].set(w_expert).at[:, GROUP_LANE0:GROUP_LANE0 + N_GROUPS].set(w_group)
    br = jnp.zeros((1, LANES), F32)
    br = br.at[0, :N_EXPERTS].set(b_expert).at[0, GROUP_LANE0:GROUP_LANE0 + N_GROUPS].set(b_group)
    w_hi = wr.astype(BF16)
    w_lo = (wr - w_hi.astype(F32)).astype(BF16)
    idx = jnp.arange(ts)
    tril = (idx[None, :] < idx[:, None]).astype(BF16)
    return (gffn.reshape(1, -1), jnp.concatenate([w_hi, w_lo], axis=1), w_hi, br, tril)


def kernel(x, norm_mix_g, norm_ffn_g, cv_w_pw1, cv_b_pw1, cv_w_dw, cv_b_dw, cv_ln_g, cv_ln_b, cv_w_pw2, cv_b_pw2, gm_w_in, gm_b_in, gm_v_norm_g, gm_w_s, gm_b_s, gm_w_out, gm_b_out, moe_w_group, moe_b_group, moe_w_expert, moe_b_expert, moe_w_gate, moe_w_up, moe_w_down, final_g):
    bsz, seq, d = x.shape
    t = bsz * seq
    x2 = x.reshape(t, d)
    row = lambda a: a.reshape(1, -1)

    router0 = _router_inputs(norm_ffn_g[0], moe_w_group[0], moe_b_group[0], moe_w_expert[0],
                             moe_b_expert[0], TS_CONV)
    h1, hn1, route0, route_t0, cnt0 = _conv_layer(
        x2, seq, row(norm_mix_g[0]), cv_w_pw1[0].astype(BF16), row(cv_b_pw1[0]),
        jnp.broadcast_to(cv_w_dw[0][:, None, :], (CONV_WIDTH, SUBLANES, d)),
        jnp.broadcast_to(cv_b_dw[0][None, :], (SUBLANES, d)),
        row(cv_ln_g[0]), row(cv_ln_b[0]), cv_w_pw2[0].astype(BF16),
        row(cv_b_pw2[0]), router0)
    ga0, gb0 = _moe(hn1, route_t0, cnt0, moe_w_gate, moe_w_up, moe_w_down, 0)

    idx = jnp.arange(GMLP_BLOCK)
    mask = (idx[None, :] // GMLP_CHUNK) <= (idx[:, None] // GMLP_CHUNK)
    ws = jnp.where(mask[None], gm_w_s[0], 0.0).astype(BF16)
    router1 = _router_inputs(norm_ffn_g[1], moe_w_group[1], moe_b_group[1], moe_w_expert[1],
                             moe_b_expert[1], TS_GMLP)
    h2, hn2, route1, route_t1, cnt1 = _gmlp_layer(
        h1, ga0, gb0, route0, row(norm_mix_g[1]), gm_w_in[0].astype(BF16), row(gm_b_in[0]),
        row(gm_v_norm_g[0]), ws, jnp.transpose(gm_b_s[0]), gm_w_out[0].astype(BF16),
        row(gm_b_out[0]), router1)
    ga1, gb1 = _moe(hn2, route_t1, cnt1, moe_w_gate, moe_w_up, moe_w_down, 1)

    out = _final(h2, ga1, gb1, route1, row(final_g))
    return out.reshape(bsz, seq, d)
```

```python
import functools

import jax
import jax.numpy as jnp
from jax import lax
from jax.experimental import pallas as pl
from jax.experimental.pallas import tpu as pltpu
from jax.experimental.pallas import tpu_sc as plsc

D_MODEL = 1024
CONV_WIDTH = 31
GMLP_BLOCK = 128
GMLP_CHUNK = 64
GMLP_INNER = 2 * D_MODEL
GMLP_HEADS = 8
GMLP_HEAD_DIM = GMLP_INNER // GMLP_HEADS
N_GROUPS = 4
EXPERTS_PER_GROUP = 8
N_EXPERTS = N_GROUPS * EXPERTS_PER_GROUP
D_EXPERT = D_MODEL // 2
EPS = 1e-6

LANES = 128
SUBLANES = 8
HIST = 32
TS_CONV = 512
TS_GMLP = 512
TS_FINAL = 512
BM = 512
RC = 32
CONV_CW = 256
GLU_CW = 256
GMLP_CW = 512
GROUP_ROW0 = N_EXPERTS
ROUTER_ROWS = 48
VMEM_LIMIT = 56 * 1024 * 1024
SC_DISPATCH_CHUNK = 128
SC_COMBINE_CHUNK = 128

R_E1, R_E2, R_RANK1, R_RANK2, R_GATE1, R_GATE2 = range(6)

F32 = jnp.float32
BF16 = jnp.bfloat16


def _gelu_tanh(x):
    c = 0.7978845608028654
    t = jnp.tanh(x * (c + (c * 0.044715) * (x * x)))
    hx = 0.5 * x
    return hx + hx * t


def _rms(xf, g):
    return xf * lax.rsqrt(jnp.mean(xf * xf, axis=-1, keepdims=True) + EPS) * g


def _pack_bf16_pairs(xb):
    w = xb.shape[1] // 2
    bits = lax.bitcast_convert_type(xb.astype(F32), jnp.int32)
    return lax.shift_right_logical(bits[:, :w], 16) | bits[:, w:]


def _unpack_pairs_f32(p):
    lo = lax.bitcast_convert_type(lax.shift_left(p, 16), F32)
    hi = lax.bitcast_convert_type(p & jnp.int32(-65536), F32)
    return lo, hi


def _moe_combine(h, rp, p1, p2):
    w = h.shape[1] // 2
    g1 = rp[:, R_GATE1:R_GATE1 + 1]
    g2 = rp[:, R_GATE2:R_GATE2 + 1]
    lo1, hi1 = _unpack_pairs_f32(p1)
    lo2, hi2 = _unpack_pairs_f32(p2)
    return jnp.concatenate([h[:, :w] + g1 * lo1 + g2 * lo2,
                            h[:, w:] + g1 * hi1 + g2 * hi2], axis=1)


def _row_loop(n_rows, fn, unroll=True):
    def step(ci, carry):
        fn(pl.ds(pl.multiple_of(ci * RC, RC), RC))
        return carry

    lax.fori_loop(0, n_rows // RC, step, 0, unroll=unroll)


def _route_tail(h_ref, gffn_ref, wrh_ref, wrl_ref, br_ref, triu_ref, run_ref,
                hn_out_ref, route_ref, route_t_ref, cnt_ref, hi_s, lo_s):
    ts = h_ref.shape[0]

    def norm_rows(rows):
        hn2 = _rms(h_ref[rows, :], gffn_ref[...])
        hi = hn2.astype(BF16)
        hf = hi.astype(F32)
        hi_s[rows, :] = hi
        lo_s[rows, :] = (hn2 - hf).astype(BF16)
        hn_out_ref[rows, :] = _pack_bf16_pairs(hi)

    _row_loop(ts, norm_rows)

    nt = (((1,), (1,)), ((), ()))
    hi = hi_s[...]
    lt = (lax.dot_general(wrh_ref[...], hi, nt, preferred_element_type=F32)
          + lax.dot_general(wrl_ref[...], hi, nt, preferred_element_type=F32)
          + lax.dot_general(wrh_ref[...], lo_s[...], nt, preferred_element_type=F32)
          + br_ref[...])
    sub = lax.broadcasted_iota(jnp.int32, (SUBLANES, ts), 0).astype(F32)
    ninf = jnp.float32(-jnp.inf)
    big = jnp.float32(1e9)
    first_idx = lambda hit: jnp.min(jnp.where(hit, sub, big), axis=0, keepdims=True)

    g_ok = sub < N_GROUPS
    lg = jnp.where(g_ok, lt[GROUP_ROW0:GROUP_ROW0 + SUBLANES, :], ninf)
    gmax = jnp.max(lg, axis=0, keepdims=True)
    gsel = first_idx(lg == gmax)
    p_g = 1.0 / jnp.sum(jnp.where(g_ok, jnp.exp(lg - gmax), 0.0), axis=0, keepdims=True)
    le = lt[0:EXPERTS_PER_GROUP, :]
    for g in range(1, N_GROUPS):
        le = jnp.where(gsel == g, lt[g * EXPERTS_PER_GROUP:(g + 1) * EXPERTS_PER_GROUP, :], le)
    v1 = jnp.max(le, axis=0, keepdims=True)
    i1 = first_idx(le == v1)
    le2 = jnp.where(sub == i1, ninf, le)
    v2 = jnp.max(le2, axis=0, keepdims=True)
    i2 = first_idx(le2 == v2)
    e = jnp.exp(v2 - v1)
    den = 1.0 + e
    gate1 = p_g * (1.0 / den)
    gate2 = p_g * (e / den)
    e1 = gsel * EXPERTS_PER_GROUP + i1
    e2 = gsel * EXPERTS_PER_GROUP + i2

    eid = lax.broadcasted_iota(jnp.int32, (N_EXPERTS, ts), 0).astype(F32)
    oh1 = eid == e1
    oh2 = eid == e2
    oh = jnp.where(oh1 | oh2, 1.0, 0.0)
    run = run_ref[...]
    c = jnp.dot(oh.astype(BF16), triu_ref[...], preferred_element_type=F32) + run
    rank1 = jnp.sum(jnp.where(oh1, c, 0.0), axis=0, keepdims=True)
    rank2 = jnp.sum(jnp.where(oh2, c, 0.0), axis=0, keepdims=True)
    run = run + jnp.broadcast_to(jnp.sum(oh, axis=1, keepdims=True), run.shape)
    run_ref[...] = run
    cnt_ref[...] = run[:, :LANES]

    rec_t = jnp.concatenate([e1, e2, rank1, rank2, gate1, gate2,
                             jnp.zeros((SUBLANES - 6, ts), F32)], axis=0)
    route_t_ref[...] = rec_t
    route_ref[...] = jnp.concatenate([rec_t, jnp.zeros((LANES - SUBLANES, ts), F32)], axis=0).T


def _conv_layer_body(tiles_per_seq,
                     x_ref, gmix_ref, wpw1_ref, bpw1_ref, wdw_ref, bdw_ref, lng_ref, lnb_ref,
                     wpw2_ref, bpw2_ref, gffn_ref, wrh_ref, wrl_ref, br_ref, triu_ref,
                     h_out_ref, hn_out_ref, route_ref, route_t_ref, cnt_ref,
                     hn_s, zext_ref, zs_ref, y_s, a_s, hi_s, lo_s, run_ref):
    i = pl.program_id(0)
    ts, d = x_ref.shape

    @pl.when(i == 0)
    def _():
        run_ref[...] = jnp.zeros_like(run_ref)

    @pl.when(i % tiles_per_seq == 0)
    def _():
        zext_ref[0:HIST, :] = jnp.zeros((HIST, d), F32)

    def norm_rows(rows):
        hn_s[rows, :] = _rms(x_ref[rows, :], gmix_ref[...]).astype(BF16)

    _row_loop(ts, norm_rows)

    hn = hn_s[...]
    for c0 in range(0, d, GLU_CW):
        ca = slice(c0, c0 + GLU_CW)
        cg = slice(d + c0, d + c0 + GLU_CW)
        pa = jnp.dot(hn, wpw1_ref[:, ca], preferred_element_type=F32) + bpw1_ref[:, ca]
        pg = jnp.dot(hn, wpw1_ref[:, cg], preferred_element_type=F32) + bpw1_ref[:, cg]
        zext_ref[HIST:HIST + ts, ca] = pa * jax.nn.sigmoid(pg)

    span = ts + HIST - SUBLANES
    sub_id = lax.broadcasted_iota(jnp.int32, (SUBLANES, LANES), 0)
    shifts = range(1, SUBLANES)
    for c0 in range(0, d, LANES):
        cols = slice(c0, c0 + LANES)

        def rotate(g):
            up = {0: g}
            for r in (4, 2, 6, 1, 3, 5, 7):
                step = r & -r
                up[r] = pltpu.roll(up[r - step], SUBLANES - step, axis=0)
            return [up[r] for r in shifts]

        cur = rotate(zext_ref[0:SUBLANES, cols])
        for m0 in range(0, span, SUBLANES):
            nxt = rotate(zext_ref[m0 + SUBLANES:m0 + 2 * SUBLANES, cols])
            for r in shifts:
                zs_ref[r - 1, m0:m0 + SUBLANES, cols] = jnp.where(
                    sub_id < SUBLANES - r, cur[r - 1], nxt[r - 1])
            cur = nxt

    first = HIST - (CONV_WIDTH - 1)

    def conv_rows(rows):
        r0 = rows.start
        groups = RC // SUBLANES
        for c0 in range(0, d, CONV_CW):
            cols = slice(c0, c0 + CONV_CW)
            accs = [bdw_ref[:, cols]] * groups
            for k in range(CONV_WIDTH):
                q, r = divmod(first + k, SUBLANES)
                w8 = wdw_ref[k, :, cols]
                for g in range(groups):
                    src = pl.ds(r0 + (q + g) * SUBLANES, SUBLANES)
                    slab = zext_ref[src, cols] if r == 0 else zs_ref[r - 1, src, cols]
                    accs[g] = accs[g] + w8 * slab
            for g in range(groups):
                y_s[pl.ds(r0 + g * SUBLANES, SUBLANES), cols] = accs[g]
        y = y_s[rows, :]
        mu = jnp.mean(y, axis=-1, keepdims=True)
        yc = y - mu
        yn = yc * lax.rsqrt(jnp.mean(yc * yc, axis=-1, keepdims=True) + EPS)
        yn = yn * lng_ref[...] + lnb_ref[...]
        a_s[rows, :] = (yn * jax.nn.sigmoid(yn)).astype(BF16)

    _row_loop(ts, conv_rows, unroll=True)
    zext_ref[0:HIST, :] = zext_ref[ts:ts + HIST, :]

    for r0 in range(0, ts, ts // 2):
        rows = slice(r0, r0 + ts // 2)
        m = jnp.dot(a_s[rows, :], wpw2_ref[...], preferred_element_type=F32) + bpw2_ref[...]
        h_out_ref[rows, :] = x_ref[rows, :] + m
    _route_tail(h_out_ref, gffn_ref, wrh_ref, wrl_ref, br_ref, triu_ref, run_ref,
                hn_out_ref, route_ref, route_t_ref, cnt_ref, hi_s, lo_s)


def _gmlp_layer_body(h_ref, g1_ref, g2_ref, rprev_ref, gmix_ref, win_ref, bin_ref, vg_ref,
                     ws_ref, bst_ref, wout_ref, bout_ref, gffn_ref, wrh_ref, wrl_ref, br_ref,
                     triu_ref,
                     h_out_ref, hn_out_ref, route_ref, route_t_ref, cnt_ref,
                     hn_s, u_s, v_s, gated_s, hi_s, lo_s, run_ref):
    i = pl.program_id(0)
    ts = h_ref.shape[0]

    @pl.when(i == 0)
    def _():
        run_ref[...] = jnp.zeros_like(run_ref)

    def norm_rows(rows):
        h = _moe_combine(h_ref[rows, :], rprev_ref[rows, :], g1_ref[rows, :], g2_ref[rows, :])
        h_out_ref[rows, :] = h + bout_ref[...]
        hn_s[rows, :] = _rms(h, gmix_ref[...]).astype(BF16)

    _row_loop(ts, norm_rows)

    hn = hn_s[...]

    def in_proj(c0):
        cols = slice(c0, c0 + GMLP_CW)
        return _gelu_tanh(jnp.dot(hn, win_ref[:, cols], preferred_element_type=F32)
                          + bin_ref[:, cols])

    ssq = jnp.zeros((ts, 1), F32)
    for c0 in range(0, GMLP_INNER, GMLP_CW):
        zc = in_proj(GMLP_INNER + c0)
        v_s[:, c0:c0 + GMLP_CW] = zc
        ssq = ssq + jnp.sum(zc * zc, axis=-1, keepdims=True)
    rs = lax.rsqrt(ssq * (1.0 / GMLP_INNER) + EPS)

    for c0 in range(0, GMLP_INNER, GMLP_CW):
        chunk = slice(c0, c0 + GMLP_CW)
        u_s[:, chunk] = in_proj(c0)
        for h0 in range(c0, c0 + GMLP_CW, GMLP_HEAD_DIM):
            hd = h0 // GMLP_HEAD_DIM
            cols = slice(h0, h0 + GMLP_HEAD_DIM)
            for b0 in range(0, ts, GMLP_BLOCK):
                rows = slice(b0, b0 + GMLP_BLOCK)
                vv = (v_s[rows, cols] * rs[rows] * vg_ref[:, cols]).astype(BF16)
                sv = (jnp.dot(ws_ref[hd], vv, preferred_element_type=F32)
                      + bst_ref[:, hd:hd + 1])
                gated_s[rows, cols] = (u_s[rows, cols] * sv).astype(BF16)
        h_out_ref[...] = h_out_ref[...] + jnp.dot(gated_s[:, chunk], wout_ref[chunk, :],
                                                  preferred_element_type=F32)
    _route_tail(h_out_ref, gffn_ref, wrh_ref, wrl_ref, br_ref, triu_ref, run_ref,
                hn_out_ref, route_ref, route_t_ref, cnt_ref, hi_s, lo_s)


def _expert_body(first_ref, count_ref, nused_ref, rows_hbm, wg_ref, wu_ref, wd_ref, y_hbm,
                 xbuf, ybuf, wgu_s, wd_s, in_sem, out_sem):
    e = pl.program_id(0)
    first = first_ref[e]
    count = count_ref[e]
    n_used = nused_ref[0]

    def rows_of(g):
        return pl.ds(pl.multiple_of(g * BM, BM), BM)

    def in_copy(g):
        slot = g & 1
        return pltpu.make_async_copy(rows_hbm.at[rows_of(g)], xbuf.at[slot], in_sem.at[slot])

    def out_copy(g):
        slot = g & 1
        return pltpu.make_async_copy(ybuf.at[slot], y_hbm.at[rows_of(g)], out_sem.at[slot])

    @pl.when(e == 0)
    def _():
        in_copy(0).start()

    @pl.when(count > 0)
    def _():
        wgu_s[:, :D_EXPERT] = wg_ref[0, 0].astype(BF16)
        wgu_s[:, D_EXPERT:] = wu_ref[0, 0].astype(BF16)
        wd_s[...] = wd_ref[0, 0].astype(BF16)

        def block(g, carry):
            slot = g & 1
            in_copy(g).wait()

            @pl.when(g + 1 < n_used)
            def _():
                in_copy(g + 1).start()

            @pl.when(g >= 2)
            def _():
                out_copy(g - 2).wait()

            lo, hi = _unpack_pairs_f32(xbuf[slot])
            half = lo.shape[1]
            gu = (jnp.dot(lo.astype(BF16), wgu_s[:half, :], preferred_element_type=F32)
                  + jnp.dot(hi.astype(BF16), wgu_s[half:, :], preferred_element_type=F32))
            gate = gu[:, :D_EXPERT]
            hb = (gate * jax.nn.sigmoid(gate)) * gu[:, D_EXPERT:]
            y = jnp.dot(hb.astype(BF16), wd_s[...], preferred_element_type=F32)
            ybuf[slot] = _pack_bf16_pairs(y.astype(BF16))
            out_copy(g).start()
            return carry

        lax.fori_loop(first, first + count, block, 0)

    @pl.when(e == pl.num_programs(0) - 1)
    def _():
        @pl.when(n_used >= 2)
        def _():
            out_copy(n_used - 2).wait()

        out_copy(n_used - 1).wait()


def _final_body(h_ref, g1_ref, g2_ref, rprev_ref, gfin_ref, o_ref):
    h = _moe_combine(h_ref[...], rprev_ref[...], g1_ref[...], g2_ref[...])
    o_ref[...] = _rms(h, gfin_ref[...])


def _const_spec(shape):
    return pl.BlockSpec(shape, lambda i: (0,) * len(shape), pipeline_mode=pl.Buffered(1))


def _row_spec(ts, width):
    return pl.BlockSpec((ts, width), lambda i: (i, 0))


def _layer_out(t, d, ts):
    shapes = [jax.ShapeDtypeStruct((t, d), F32),
              jax.ShapeDtypeStruct((t, d // 2), jnp.int32),
              jax.ShapeDtypeStruct((t, LANES), F32),
              jax.ShapeDtypeStruct((SUBLANES, t), F32),
              jax.ShapeDtypeStruct((N_EXPERTS, LANES), F32)]
    specs = [_row_spec(ts, d), _row_spec(ts, d // 2), _row_spec(ts, LANES),
             pl.BlockSpec((SUBLANES, ts), lambda i: (0, i)),
             pl.BlockSpec((N_EXPERTS, LANES), lambda i: (0, 0))]
    return shapes, specs


def _router_specs(d, ts):
    return [_const_spec((1, d)), _const_spec((ROUTER_ROWS, d)), _const_spec((ROUTER_ROWS, d)),
            _const_spec((ROUTER_ROWS, ts)), _const_spec((ts, ts))]


def _router_scratch(ts, d):
    return [pltpu.VMEM((ts, d), BF16), pltpu.VMEM((ts, d), BF16),
            pltpu.VMEM((N_EXPERTS, ts), F32)]


def _layer_params():
    return pltpu.CompilerParams(dimension_semantics=("arbitrary",), vmem_limit_bytes=VMEM_LIMIT)


def _conv_layer(x2, seq, gmix, wpw1, bpw1, wdw, bdw, lng, lnb, wpw2, bpw2, router):
    t, d = x2.shape
    ts = TS_CONV
    shapes, out_specs = _layer_out(t, d, ts)
    body = functools.partial(_conv_layer_body, seq // ts)
    return pl.pallas_call(
        body,
        grid=(t // ts,),
        in_specs=[_row_spec(ts, d), _const_spec((1, d)), _const_spec((d, 2 * d)),
                  _const_spec((1, 2 * d)), _const_spec((CONV_WIDTH, SUBLANES, d)),
                  _const_spec((SUBLANES, d)),
                  _const_spec((1, d)), _const_spec((1, d)), _const_spec((d, d)),
                  _const_spec((1, d))] + _router_specs(d, ts),
        out_specs=out_specs,
        out_shape=shapes,
        scratch_shapes=[pltpu.VMEM((ts, d), BF16),
                        pltpu.VMEM((ts + HIST, d), F32),
                        pltpu.VMEM((SUBLANES - 1, ts + HIST - SUBLANES, d), F32),
                        pltpu.VMEM((ts, d), F32),
                        pltpu.VMEM((ts, d), BF16)] + _router_scratch(ts, d),
        compiler_params=_layer_params(),
        name="conv_layer",
    )(x2, gmix, wpw1, bpw1, wdw, bdw, lng, lnb, wpw2, bpw2, *router)


def _gmlp_layer(h, g1, g2, rprev, gmix, win, bin_, vg, ws, bst, wout, bout, router):
    t, d = h.shape
    ts = TS_GMLP
    shapes, out_specs = _layer_out(t, d, ts)
    return pl.pallas_call(
        _gmlp_layer_body,
        grid=(t // ts,),
        in_specs=[_row_spec(ts, d), _row_spec(ts, d // 2), _row_spec(ts, d // 2),
                  _row_spec(ts, LANES),
                  _const_spec((1, d)), _const_spec((d, 2 * GMLP_INNER)),
                  _const_spec((1, 2 * GMLP_INNER)), _const_spec((1, GMLP_INNER)),
                  _const_spec((GMLP_HEADS, GMLP_BLOCK, GMLP_BLOCK)),
                  _const_spec((GMLP_BLOCK, GMLP_HEADS)), _const_spec((GMLP_INNER, d)),
                  _const_spec((1, d))] + _router_specs(d, ts),
        out_specs=out_specs,
        out_shape=shapes,
        scratch_shapes=[pltpu.VMEM((ts, d), BF16),
                        pltpu.VMEM((ts, GMLP_INNER), F32),
                        pltpu.VMEM((ts, GMLP_INNER), F32),
                        pltpu.VMEM((ts, GMLP_INNER), BF16)] + _router_scratch(ts, d),
        compiler_params=_layer_params(),
        name="gmlp_layer",
    )(h, g1, g2, rprev, gmix, win, bin_, vg, ws, bst, wout, bout, *router)


def _experts(rows, first_blk, blk_count, n_used, wg, wu, wd, layer):
    r, w = rows.shape
    d = wg.shape[2]

    def w_map(e, first, count, nu):
        return (layer, e, 0, 0)

    any_space = pl.BlockSpec(memory_space=pl.ANY)
    grid_spec = pltpu.PrefetchScalarGridSpec(
        num_scalar_prefetch=3,
        grid=(N_EXPERTS,),
        in_specs=[any_space,
                  pl.BlockSpec((1, 1, d, D_EXPERT), w_map),
                  pl.BlockSpec((1, 1, d, D_EXPERT), w_map),
                  pl.BlockSpec((1, 1, D_EXPERT, d), w_map)],
        out_specs=any_space,
        scratch_shapes=[pltpu.VMEM((2, BM, w), rows.dtype),
                        pltpu.VMEM((2, BM, w), rows.dtype),
                        pltpu.VMEM((d, 2 * D_EXPERT), BF16),
                        pltpu.VMEM((D_EXPERT, d), BF16),
                        pltpu.SemaphoreType.DMA((2,)),
                        pltpu.SemaphoreType.DMA((2,))],
    )
    return pl.pallas_call(
        _expert_body,
        grid_spec=grid_spec,
        out_shape=jax.ShapeDtypeStruct((r, w), rows.dtype),
        compiler_params=_layer_params(),
        name="experts",
    )(first_blk, blk_count, n_used, rows, wg, wu, wd)


def _final(h, g1, g2, rprev, gfin):
    t, d = h.shape
    ts = TS_FINAL
    return pl.pallas_call(
        _final_body,
        grid=(t // ts,),
        in_specs=[_row_spec(ts, d), _row_spec(ts, d // 2), _row_spec(ts, d // 2),
                  _row_spec(ts, LANES),
                  _const_spec((1, d))],
        out_specs=_row_spec(ts, d),
        out_shape=jax.ShapeDtypeStruct((t, d), F32),
        compiler_params=_layer_params(),
        name="final_norm",
    )(h, g1, g2, rprev, gfin)


def _plan_body(cnt_ref, rt_ref, d1_ref, d2_ref, first_ref, count_ref, nu_ref):
    e1 = rt_ref[R_E1:R_E1 + 1, :]
    e2 = rt_ref[R_E2:R_E2 + 1, :]
    d1 = rt_ref[R_RANK1:R_RANK1 + 1, :]
    d2 = rt_ref[R_RANK2:R_RANK2 + 1, :]
    pb = jnp.int32(0)
    for e in range(N_EXPERTS):
        nb = lax.shift_right_logical(cnt_ref[e] + (BM - 1), BM.bit_length() - 1)
        ps = (pb * BM).astype(F32)
        d1 = d1 + jnp.where(e1 == e, ps, 0.0)
        d2 = d2 + jnp.where(e2 == e, ps, 0.0)
        first_ref[e] = pb
        count_ref[e] = nb
        pb = pb + nb
    nu_ref[0] = pb
    d1_ref[...] = d1.astype(jnp.int32)
    d2_ref[...] = d2.astype(jnp.int32)


def _plan(route_t, counts):
    t = route_t.shape[1]
    smem = pl.BlockSpec(memory_space=pltpu.SMEM)
    vmem = pl.BlockSpec(memory_space=pltpu.VMEM)
    return pl.pallas_call(
        _plan_body,
        in_specs=[smem, vmem],
        out_specs=[vmem, vmem, smem, smem, smem],
        out_shape=[jax.ShapeDtypeStruct((1, t), jnp.int32),
                   jax.ShapeDtypeStruct((1, t), jnp.int32),
                   jax.ShapeDtypeStruct((N_EXPERTS,), jnp.int32),
                   jax.ShapeDtypeStruct((N_EXPERTS,), jnp.int32),
                   jax.ShapeDtypeStruct((1,), jnp.int32)],
        name="moe_plan",
    )(counts, route_t)


def _sc_workers():
    info = plsc.get_sparse_core_info()
    return info.num_cores, info.num_cores * info.num_subcores


def _sc_mesh():
    return plsc.VectorSubcoreMesh(core_axis_name="c", subcore_axis_name="s")


def _sc_worker_id(num_cores):
    return lax.axis_index("s") * num_cores + lax.axis_index("c")


def _dispatch(hn, dest1, dest2, n_rows):
    t, w = hn.shape
    num_cores, n_workers = _sc_workers()
    per_w = t // n_workers
    chunk = SC_DISPATCH_CHUNK

    def body(hn_hbm, d1_hbm, d2_hbm, rows_hbm, buf, i1, i2):
        base_w = _sc_worker_id(num_cores) * per_w

        @pl.loop(0, per_w // chunk)
        def _(j):
            base = pl.multiple_of(base_w + j * chunk, chunk)
            pltpu.sync_copy(hn_hbm.at[pl.ds(base, chunk)], buf)
            pltpu.sync_copy(d1_hbm.at[:, pl.ds(base, chunk)], i1)
            pltpu.sync_copy(d2_hbm.at[:, pl.ds(base, chunk)], i2)
            pltpu.sync_copy(buf, rows_hbm.at[i1.at[0]])
            pltpu.sync_copy(buf, rows_hbm.at[i2.at[0]])

    return pl.kernel(
        body,
        out_type=jax.ShapeDtypeStruct((n_rows, w), hn.dtype),
        mesh=_sc_mesh(),
        scratch_types=[pltpu.VMEM((chunk, w), hn.dtype),
                       pltpu.VMEM((1, chunk), jnp.int32),
                       pltpu.VMEM((1, chunk), jnp.int32)],
        name="moe_dispatch",
    )(hn, dest1, dest2)


def _combine_gather(y, dest1, dest2):
    d = y.shape[1]
    t = dest1.shape[1]
    num_cores, n_workers = _sc_workers()
    per_w = t // n_workers
    chunk = SC_COMBINE_CHUNK

    def body(y_hbm, d1_hbm, d2_hbm, g1_hbm, g2_hbm, buf, idx):
        base_w = _sc_worker_id(num_cores) * per_w

        @pl.loop(0, per_w // chunk)
        def _(j):
            base = pl.multiple_of(base_w + j * chunk, chunk)
            for d_hbm, g_hbm in ((d1_hbm, g1_hbm), (d2_hbm, g2_hbm)):
                pltpu.sync_copy(d_hbm.at[:, pl.ds(base, chunk)], idx)
                pltpu.sync_copy(y_hbm.at[idx.at[0]], buf)
                pltpu.sync_copy(buf, g_hbm.at[pl.ds(base, chunk)])

    out = jax.ShapeDtypeStruct((t, d), y.dtype)
    return pl.kernel(
        body,
        out_type=(out, out),
        mesh=_sc_mesh(),
        scratch_types=[pltpu.VMEM((chunk, d), y.dtype),
                       pltpu.VMEM((1, chunk), jnp.int32)],
        name="moe_combine_gather",
    )(y, dest1, dest2)


def _moe(hn, route_t, cnt, wg, wu, wd, layer):
    t = hn.shape[0]
    n_blk = (2 * t) // BM + N_EXPERTS
    counts = cnt[:, 0].astype(jnp.int32)
    dest1, dest2, first_blk, blk_count, n_used = _plan(route_t, counts)
    rows = _dispatch(hn, dest1, dest2, n_blk * BM)
    y = _experts(rows, first_blk, blk_count, n_used, wg, wu, wd, layer)
    return _combine_gather(y, dest1, dest2)


def _router_inputs(gffn, w_group, b_group, w_expert, b_expert, ts):
    d = w_group.shape[0]
    wr = jnp.zeros((ROUTER_ROWS, d), F32)
    wr = wr.at[:N_EXPERTS].set(w_expert.T).at[GROUP_ROW0:GROUP_ROW0 + N_GROUPS].set(w_group.T)
    br = jnp.zeros((ROUTER_ROWS,), F32)
    br = br.at[:N_EXPERTS].set(b_expert).at[GROUP_ROW0:GROUP_ROW0 + N_GROUPS].set(b_group)
    w_hi = wr.astype(BF16)
    w_lo = (wr - w_hi.astype(F32)).astype(BF16)
    idx = jnp.arange(ts)
    triu = (idx[:, None] < idx[None, :]).astype(BF16)
    return (gffn.reshape(1, -1), w_hi, w_lo, jnp.broadcast_to(br[:, None], (ROUTER_ROWS, ts)), triu)


def kernel(x, norm_mix_g, norm_ffn_g, cv_w_pw1, cv_b_pw1, cv_w_dw, cv_b_dw, cv_ln_g, cv_ln_b, cv_w_pw2, cv_b_pw2, gm_w_in, gm_b_in, gm_v_norm_g, gm_w_s, gm_b_s, gm_w_out, gm_b_out, moe_w_group, moe_b_group, moe_w_expert, moe_b_expert, moe_w_gate, moe_w_up, moe_w_down, final_g):
    bsz, seq, d = x.shape
    t = bsz * seq
    x2 = x.reshape(t, d)
    row = lambda a: a.reshape(1, -1)

    router0 = _router_inputs(norm_ffn_g[0], moe_w_group[0], moe_b_group[0], moe_w_expert[0],
                             moe_b_expert[0], TS_CONV)
    h1, hn1, route0, route_t0, cnt0 = _conv_layer(
        x2, seq, row(norm_mix_g[0]), cv_w_pw1[0].astype(BF16), row(cv_b_pw1[0]),
        jnp.broadcast_to(cv_w_dw[0][:, None, :], (CONV_WIDTH, SUBLANES, d)),
        jnp.broadcast_to(cv_b_dw[0][None, :], (SUBLANES, d)),
        row(cv_ln_g[0]), row(cv_ln_b[0]), cv_w_pw2[0].astype(BF16),
        row(cv_b_pw2[0]), router0)
    ga0, gb0 = _moe(hn1, route_t0, cnt0, moe_w_gate, moe_w_up, moe_w_down, 0)

    idx = jnp.arange(GMLP_BLOCK)
    mask = (idx[None, :] // GMLP_CHUNK) <= (idx[:, None] // GMLP_CHUNK)
    ws = jnp.where(mask[None], gm_w_s[0], 0.0).astype(BF16)
    router1 = _router_inputs(norm_ffn_g[1], moe_w_group[1], moe_b_group[1], moe_w_expert[1],
                             moe_b_expert[1], TS_GMLP)
    h2, hn2, route1, route_t1, cnt1 = _gmlp_layer(
        h1, ga0, gb0, route0, row(norm_mix_g[1]), gm_w_in[0].astype(BF16), row(gm_b_in[0]),
        row(gm_v_norm_g[0]), ws, jnp.transpose(gm_b_s[0]), gm_w_out[0].astype(BF16),
        row(gm_b_out[0]), router1)
    ga1, gb1 = _moe(hn2, route_t1, cnt1, moe_w_gate, moe_w_up, moe_w_down, 1)

    out = _final(h2, ga1, gb1, route1, row(final_g))
    return out.reshape(bsz, seq, d)
```

```python
import functools

import jax
import jax.numpy as jnp
from jax import lax
from jax.experimental import pallas as pl
from jax.experimental.pallas import tpu as pltpu
from jax.experimental.pallas import tpu_sc as plsc

D_MODEL = 1024
CONV_WIDTH = 31
GMLP_BLOCK = 128
GMLP_CHUNK = 64
GMLP_INNER = 2 * D_MODEL
GMLP_HEADS = 8
GMLP_HEAD_DIM = GMLP_INNER // GMLP_HEADS
N_GROUPS = 4
EXPERTS_PER_GROUP = 8
N_EXPERTS = N_GROUPS * EXPERTS_PER_GROUP
D_EXPERT = D_MODEL // 2
EPS = 1e-6

LANES = 128
SUBLANES = 8
HIST = 32
TS_CONV = 512
TS_GMLP = 512
TS_FINAL = 512
BM = 512
EXPERT_BIG = 2
RC = 32
CONV_CW = 256
GLU_CW = 256
GMLP_CW = 512
GROUP_ROW0 = N_EXPERTS
ROUTER_ROWS = 48
VMEM_LIMIT = 56 * 1024 * 1024
SC_DISPATCH_CHUNK = 128
SC_COMBINE_CHUNK = 128

R_E1, R_E2, R_RANK1, R_RANK2, R_GATE1, R_GATE2 = range(6)

F32 = jnp.float32
BF16 = jnp.bfloat16


def _gelu_tanh(x):
    c = 0.7978845608028654
    t = jnp.tanh(x * (c + (c * 0.044715) * (x * x)))
    hx = 0.5 * x
    return hx + hx * t


def _rms(xf, g):
    return xf * lax.rsqrt(jnp.mean(xf * xf, axis=-1, keepdims=True) + EPS) * g


def _pack_bf16_pairs(xb):
    w = xb.shape[1] // 2
    bits = lax.bitcast_convert_type(xb.astype(F32), jnp.int32)
    return lax.shift_right_logical(bits[:, :w], 16) | bits[:, w:]


def _unpack_pairs_f32(p):
    lo = lax.bitcast_convert_type(lax.shift_left(p, 16), F32)
    hi = lax.bitcast_convert_type(p & jnp.int32(-65536), F32)
    return lo, hi


def _moe_combine(h, rp, p1, p2):
    w = h.shape[1] // 2
    g1 = rp[:, R_GATE1:R_GATE1 + 1]
    g2 = rp[:, R_GATE2:R_GATE2 + 1]
    lo1, hi1 = _unpack_pairs_f32(p1)
    lo2, hi2 = _unpack_pairs_f32(p2)
    return jnp.concatenate([h[:, :w] + g1 * lo1 + g2 * lo2,
                            h[:, w:] + g1 * hi1 + g2 * hi2], axis=1)


def _row_loop(n_rows, fn, unroll=True):
    def step(ci, carry):
        fn(pl.ds(pl.multiple_of(ci * RC, RC), RC))
        return carry

    lax.fori_loop(0, n_rows // RC, step, 0, unroll=unroll)


def _route_tail(h_ref, gffn_ref, wrh_ref, wrl_ref, br_ref, triu_ref, run_ref,
                hn_out_ref, route_ref, route_t_ref, cnt_ref, hi_s, lo_s):
    ts = h_ref.shape[0]

    def norm_rows(rows):
        hn2 = _rms(h_ref[rows, :], gffn_ref[...])
        hi = hn2.astype(BF16)
        hf = hi.astype(F32)
        hi_s[rows, :] = hi
        lo_s[rows, :] = (hn2 - hf).astype(BF16)
        hn_out_ref[rows, :] = _pack_bf16_pairs(hi)

    _row_loop(ts, norm_rows)

    nt = (((1,), (1,)), ((), ()))
    hi = hi_s[...]
    lt = (lax.dot_general(wrh_ref[...], hi, nt, preferred_element_type=F32)
          + lax.dot_general(wrl_ref[...], hi, nt, preferred_element_type=F32)
          + lax.dot_general(wrh_ref[...], lo_s[...], nt, preferred_element_type=F32)
          + br_ref[...])
    sub = lax.broadcasted_iota(jnp.int32, (SUBLANES, ts), 0).astype(F32)
    ninf = jnp.float32(-jnp.inf)
    big = jnp.float32(1e9)
    first_idx = lambda hit: jnp.min(jnp.where(hit, sub, big), axis=0, keepdims=True)

    g_ok = sub < N_GROUPS
    lg = jnp.where(g_ok, lt[GROUP_ROW0:GROUP_ROW0 + SUBLANES, :], ninf)
    gmax = jnp.max(lg, axis=0, keepdims=True)
    gsel = first_idx(lg == gmax)
    p_g = 1.0 / jnp.sum(jnp.where(g_ok, jnp.exp(lg - gmax), 0.0), axis=0, keepdims=True)
    le = lt[0:EXPERTS_PER_GROUP, :]
    for g in range(1, N_GROUPS):
        le = jnp.where(gsel == g, lt[g * EXPERTS_PER_GROUP:(g + 1) * EXPERTS_PER_GROUP, :], le)
    v1 = jnp.max(le, axis=0, keepdims=True)
    i1 = first_idx(le == v1)
    le2 = jnp.where(sub == i1, ninf, le)
    v2 = jnp.max(le2, axis=0, keepdims=True)
    i2 = first_idx(le2 == v2)
    e = jnp.exp(v2 - v1)
    den = 1.0 + e
    gate1 = p_g * (1.0 / den)
    gate2 = p_g * (e / den)
    e1 = gsel * EXPERTS_PER_GROUP + i1
    e2 = gsel * EXPERTS_PER_GROUP + i2

    eid = lax.broadcasted_iota(jnp.int32, (N_EXPERTS, ts), 0).astype(F32)
    oh1 = eid == e1
    oh2 = eid == e2
    oh = jnp.where(oh1 | oh2, 1.0, 0.0)
    run = run_ref[...]
    c = jnp.dot(oh.astype(BF16), triu_ref[...], preferred_element_type=F32) + run
    rank1 = jnp.sum(jnp.where(oh1, c, 0.0), axis=0, keepdims=True)
    rank2 = jnp.sum(jnp.where(oh2, c, 0.0), axis=0, keepdims=True)
    run = run + jnp.broadcast_to(jnp.sum(oh, axis=1, keepdims=True), run.shape)
    run_ref[...] = run
    cnt_ref[...] = run[:, :LANES]

    rec_t = jnp.concatenate([e1, e2, rank1, rank2, gate1, gate2,
                             jnp.zeros((SUBLANES - 6, ts), F32)], axis=0)
    route_t_ref[...] = rec_t
    route_ref[...] = jnp.concatenate([rec_t, jnp.zeros((LANES - SUBLANES, ts), F32)], axis=0).T


def _conv_layer_body(tiles_per_seq,
                     x_ref, gmix_ref, wpw1_ref, bpw1_ref, wdw_ref, bdw_ref, lng_ref, lnb_ref,
                     wpw2_ref, bpw2_ref, gffn_ref, wrh_ref, wrl_ref, br_ref, triu_ref,
                     h_out_ref, hn_out_ref, route_ref, route_t_ref, cnt_ref,
                     hn_s, zext_ref, zs_ref, y_s, a_s, hi_s, lo_s, run_ref):
    i = pl.program_id(0)
    ts, d = x_ref.shape

    @pl.when(i == 0)
    def _():
        run_ref[...] = jnp.zeros_like(run_ref)

    @pl.when(i % tiles_per_seq == 0)
    def _():
        zext_ref[0:HIST, :] = jnp.zeros((HIST, d), F32)

    def norm_rows(rows):
        hn_s[rows, :] = _rms(x_ref[rows, :], gmix_ref[...]).astype(BF16)

    _row_loop(ts, norm_rows)

    hn = hn_s[...]
    for c0 in range(0, d, GLU_CW):
        ca = slice(c0, c0 + GLU_CW)
        cg = slice(d + c0, d + c0 + GLU_CW)
        pa = jnp.dot(hn, wpw1_ref[:, ca], preferred_element_type=F32) + bpw1_ref[:, ca]
        pg = jnp.dot(hn, wpw1_ref[:, cg], preferred_element_type=F32) + bpw1_ref[:, cg]
        zext_ref[HIST:HIST + ts, ca] = pa * jax.nn.sigmoid(pg)

    span = ts + HIST - SUBLANES
    sub_id = lax.broadcasted_iota(jnp.int32, (SUBLANES, LANES), 0)
    shifts = range(1, SUBLANES)
    for c0 in range(0, d, LANES):
        cols = slice(c0, c0 + LANES)

        def rotate(g):
            up = {0: g}
            for r in (4, 2, 6, 1, 3, 5, 7):
                step = r & -r
                up[r] = pltpu.roll(up[r - step], SUBLANES - step, axis=0)
            return [up[r] for r in shifts]

        cur = rotate(zext_ref[0:SUBLANES, cols])
        for m0 in range(0, span, SUBLANES):
            nxt = rotate(zext_ref[m0 + SUBLANES:m0 + 2 * SUBLANES, cols])
            for r in shifts:
                zs_ref[r - 1, m0:m0 + SUBLANES, cols] = jnp.where(
                    sub_id < SUBLANES - r, cur[r - 1], nxt[r - 1])
            cur = nxt

    first = HIST - (CONV_WIDTH - 1)

    def conv_rows(rows):
        r0 = rows.start
        groups = RC // SUBLANES
        for c0 in range(0, d, CONV_CW):
            cols = slice(c0, c0 + CONV_CW)
            accs = [bdw_ref[:, cols]] * groups
            for k in range(CONV_WIDTH):
                q, r = divmod(first + k, SUBLANES)
                w8 = wdw_ref[k, :, cols]
                for g in range(groups):
                    src = pl.ds(r0 + (q + g) * SUBLANES, SUBLANES)
                    slab = zext_ref[src, cols] if r == 0 else zs_ref[r - 1, src, cols]
                    accs[g] = accs[g] + w8 * slab
            for g in range(groups):
                y_s[pl.ds(r0 + g * SUBLANES, SUBLANES), cols] = accs[g]
        y = y_s[rows, :]
        mu = jnp.mean(y, axis=-1, keepdims=True)
        yc = y - mu
        yn = yc * lax.rsqrt(jnp.mean(yc * yc, axis=-1, keepdims=True) + EPS)
        yn = yn * lng_ref[...] + lnb_ref[...]
        a_s[rows, :] = (yn * jax.nn.sigmoid(yn)).astype(BF16)

    _row_loop(ts, conv_rows, unroll=True)
    zext_ref[0:HIST, :] = zext_ref[ts:ts + HIST, :]

    for r0 in range(0, ts, ts // 2):
        rows = slice(r0, r0 + ts // 2)
        m = jnp.dot(a_s[rows, :], wpw2_ref[...], preferred_element_type=F32) + bpw2_ref[...]
        h_out_ref[rows, :] = x_ref[rows, :] + m
    _route_tail(h_out_ref, gffn_ref, wrh_ref, wrl_ref, br_ref, triu_ref, run_ref,
                hn_out_ref, route_ref, route_t_ref, cnt_ref, hi_s, lo_s)


def _gmlp_layer_body(h_ref, g1_ref, g2_ref, rprev_ref, gmix_ref, win_ref, bin_ref, vg_ref,
                     ws_ref, bst_ref, wout_ref, bout_ref, gffn_ref, wrh_ref, wrl_ref, br_ref,
                     triu_ref,
                     h_out_ref, hn_out_ref, route_ref, route_t_ref, cnt_ref,
                     hn_s, u_s, v_s, gated_s, hi_s, lo_s, run_ref):
    i = pl.program_id(0)
    ts = h_ref.shape[0]

    @pl.when(i == 0)
    def _():
        run_ref[...] = jnp.zeros_like(run_ref)

    def norm_rows(rows):
        h = _moe_combine(h_ref[rows, :], rprev_ref[rows, :], g1_ref[rows, :], g2_ref[rows, :])
        h_out_ref[rows, :] = h
        hn_s[rows, :] = _rms(h, gmix_ref[...]).astype(BF16)

    _row_loop(ts, norm_rows)

    hn = hn_s[...]
    ssq = jnp.zeros((ts, 1), F32)
    for c0 in range(0, 2 * GMLP_INNER, GMLP_CW):
        cols = slice(c0, c0 + GMLP_CW)
        zc = _gelu_tanh(jnp.dot(hn, win_ref[:, cols], preferred_element_type=F32)
                        + bin_ref[:, cols])
        if c0 < GMLP_INNER:
            u_s[:, cols] = zc
        else:
            v_s[:, c0 - GMLP_INNER:c0 - GMLP_INNER + GMLP_CW] = zc
            ssq = ssq + jnp.sum(zc * zc, axis=-1, keepdims=True)
    rs = lax.rsqrt(ssq * (1.0 / GMLP_INNER) + EPS)

    for b0 in range(0, ts, GMLP_BLOCK):
        rows = slice(b0, b0 + GMLP_BLOCK)
        for hd in range(GMLP_HEADS):
            cols = slice(hd * GMLP_HEAD_DIM, (hd + 1) * GMLP_HEAD_DIM)
            vv = (v_s[rows, cols] * rs[rows] * vg_ref[:, cols]).astype(BF16)
            sv = jnp.dot(ws_ref[hd], vv, preferred_element_type=F32) + bst_ref[:, hd:hd + 1]
            gated_s[rows, cols] = (u_s[rows, cols] * sv).astype(BF16)

    out = jnp.dot(gated_s[...], wout_ref[...], preferred_element_type=F32) + bout_ref[...]
    h_out_ref[...] = h_out_ref[...] + out
    _route_tail(h_out_ref, gffn_ref, wrh_ref, wrl_ref, br_ref, triu_ref, run_ref,
                hn_out_ref, route_ref, route_t_ref, cnt_ref, hi_s, lo_s)


def _swiglu_rows(x_packed, wgu_s, wd_s):
    lo, hi = _unpack_pairs_f32(x_packed)
    half = lo.shape[1]
    gu = (jnp.dot(lo.astype(BF16), wgu_s[:half, :], preferred_element_type=F32)
          + jnp.dot(hi.astype(BF16), wgu_s[half:, :], preferred_element_type=F32))
    gate = gu[:, :D_EXPERT]
    hb = (gate * jax.nn.sigmoid(gate)) * gu[:, D_EXPERT:]
    y = jnp.dot(hb.astype(BF16), wd_s[...], preferred_element_type=F32)
    return _pack_bf16_pairs(y.astype(BF16))


def _expert_body(first_ref, count_ref, rows_hbm, wg_ref, wu_ref, wd_ref, y_hbm,
                 xbig, ybig, xsm, ysm, wgu_s, wd_s, in_sem, out_sem, sm_sem):
    e = pl.program_id(0)
    big_rows = EXPERT_BIG * BM

    def split(count):
        return lax.shift_right_logical(count, 1), count & 1

    first = first_ref[e]
    nbig, nsmall = split(count_ref[e])

    def big_rows_of(i):
        return pl.ds(pl.multiple_of((first + EXPERT_BIG * i) * BM, BM), big_rows)

    small_rows = pl.ds(pl.multiple_of((first + EXPERT_BIG * nbig) * BM, BM), BM)

    def in_big(i):
        slot = i & 1
        return pltpu.make_async_copy(rows_hbm.at[big_rows_of(i)], xbig.at[slot], in_sem.at[slot])

    def out_big(i):
        slot = i & 1
        return pltpu.make_async_copy(ybig.at[slot], y_hbm.at[big_rows_of(i)], out_sem.at[slot])

    in_small = pltpu.make_async_copy(rows_hbm.at[small_rows], xsm, sm_sem.at[0])
    out_small = pltpu.make_async_copy(ysm, y_hbm.at[small_rows], sm_sem.at[1])

    def drain(n_big, n_small):
        def wait_big(slot):
            pltpu.make_async_copy(ybig.at[slot], y_hbm.at[pl.ds(0, big_rows)],
                                  out_sem.at[slot]).wait()

        @pl.when(n_big >= 2)
        def _():
            wait_big(n_big & 1)

        @pl.when(n_big >= 1)
        def _():
            wait_big((n_big - 1) & 1)

        @pl.when(n_small > 0)
        def _():
            pltpu.make_async_copy(ysm, y_hbm.at[pl.ds(0, BM)], sm_sem.at[1]).wait()

    @pl.when(nbig > 0)
    def _():
        in_big(0).start()

    @pl.when(e > 0)
    def _():
        drain(*split(count_ref[jnp.maximum(e - 1, 0)]))

    @pl.when(nsmall > 0)
    def _():
        in_small.start()

    @pl.when(nbig + nsmall > 0)
    def _():
        wgu_s[:, :D_EXPERT] = wg_ref[0, 0].astype(BF16)
        wgu_s[:, D_EXPERT:] = wu_ref[0, 0].astype(BF16)
        wd_s[...] = wd_ref[0, 0].astype(BF16)

        def big(i, carry):
            slot = i & 1
            in_big(i).wait()

            @pl.when(i + 1 < nbig)
            def _():
                in_big(i + 1).start()

            @pl.when(i >= 2)
            def _():
                out_big(i - 2).wait()

            ybig[slot] = _swiglu_rows(xbig[slot], wgu_s, wd_s)
            out_big(i).start()
            return carry

        lax.fori_loop(0, nbig, big, 0)

        @pl.when(nsmall > 0)
        def _():
            in_small.wait()
            ysm[...] = _swiglu_rows(xsm[...], wgu_s, wd_s)
            out_small.start()

    @pl.when(e == pl.num_programs(0) - 1)
    def _():
        drain(nbig, nsmall)


def _final_body(h_ref, g1_ref, g2_ref, rprev_ref, gfin_ref, o_ref):
    h = _moe_combine(h_ref[...], rprev_ref[...], g1_ref[...], g2_ref[...])
    o_ref[...] = _rms(h, gfin_ref[...])


def _const_spec(shape):
    return pl.BlockSpec(shape, lambda i: (0,) * len(shape), pipeline_mode=pl.Buffered(1))


def _row_spec(ts, width):
    return pl.BlockSpec((ts, width), lambda i: (i, 0))


def _layer_out(t, d, ts):
    shapes = [jax.ShapeDtypeStruct((t, d), F32),
              jax.ShapeDtypeStruct((t, d // 2), jnp.int32),
              jax.ShapeDtypeStruct((t, LANES), F32),
              jax.ShapeDtypeStruct((SUBLANES, t), F32),
              jax.ShapeDtypeStruct((N_EXPERTS, LANES), F32)]
    specs = [_row_spec(ts, d), _row_spec(ts, d // 2), _row_spec(ts, LANES),
             pl.BlockSpec((SUBLANES, ts), lambda i: (0, i)),
             pl.BlockSpec((N_EXPERTS, LANES), lambda i: (0, 0))]
    return shapes, specs


def _router_specs(d, ts):
    return [_const_spec((1, d)), _const_spec((ROUTER_ROWS, d)), _const_spec((ROUTER_ROWS, d)),
            _const_spec((ROUTER_ROWS, ts)), _const_spec((ts, ts))]


def _router_scratch(ts, d):
    return [pltpu.VMEM((ts, d), BF16), pltpu.VMEM((ts, d), BF16),
            pltpu.VMEM((N_EXPERTS, ts), F32)]


def _layer_params():
    return pltpu.CompilerParams(dimension_semantics=("arbitrary",), vmem_limit_bytes=VMEM_LIMIT)


def _conv_layer(x2, seq, gmix, wpw1, bpw1, wdw, bdw, lng, lnb, wpw2, bpw2, router):
    t, d = x2.shape
    ts = TS_CONV
    shapes, out_specs = _layer_out(t, d, ts)
    body = functools.partial(_conv_layer_body, seq // ts)
    return pl.pallas_call(
        body,
        grid=(t // ts,),
        in_specs=[_row_spec(ts, d), _const_spec((1, d)), _const_spec((d, 2 * d)),
                  _const_spec((1, 2 * d)), _const_spec((CONV_WIDTH, SUBLANES, d)),
                  _const_spec((SUBLANES, d)),
                  _const_spec((1, d)), _const_spec((1, d)), _const_spec((d, d)),
                  _const_spec((1, d))] + _router_specs(d, ts),
        out_specs=out_specs,
        out_shape=shapes,
        scratch_shapes=[pltpu.VMEM((ts, d), BF16),
                        pltpu.VMEM((ts + HIST, d), F32),
                        pltpu.VMEM((SUBLANES - 1, ts + HIST - SUBLANES, d), F32),
                        pltpu.VMEM((ts, d), F32),
                        pltpu.VMEM((ts, d), BF16)] + _router_scratch(ts, d),
        compiler_params=_layer_params(),
        name="conv_layer",
    )(x2, gmix, wpw1, bpw1, wdw, bdw, lng, lnb, wpw2, bpw2, *router)


def _gmlp_layer(h, g1, g2, rprev, gmix, win, bin_, vg, ws, bst, wout, bout, router):
    t, d = h.shape
    ts = TS_GMLP
    shapes, out_specs = _layer_out(t, d, ts)
    return pl.pallas_call(
        _gmlp_layer_body,
        grid=(t // ts,),
        in_specs=[_row_spec(ts, d), _row_spec(ts, d // 2), _row_spec(ts, d // 2),
                  _row_spec(ts, LANES),
                  _const_spec((1, d)), _const_spec((d, 2 * GMLP_INNER)),
                  _const_spec((1, 2 * GMLP_INNER)), _const_spec((1, GMLP_INNER)),
                  _const_spec((GMLP_HEADS, GMLP_BLOCK, GMLP_BLOCK)),
                  _const_spec((GMLP_BLOCK, GMLP_HEADS)), _const_spec((GMLP_INNER, d)),
                  _const_spec((1, d))] + _router_specs(d, ts),
        out_specs=out_specs,
        out_shape=shapes,
        scratch_shapes=[pltpu.VMEM((ts, d), BF16),
                        pltpu.VMEM((ts, GMLP_INNER), F32),
                        pltpu.VMEM((ts, GMLP_INNER), F32),
                        pltpu.VMEM((ts, GMLP_INNER), BF16)] + _router_scratch(ts, d),
        compiler_params=_layer_params(),
        name="gmlp_layer",
    )(h, g1, g2, rprev, gmix, win, bin_, vg, ws, bst, wout, bout, *router)


def _experts(rows, first_blk, blk_count, wg, wu, wd, layer):
    r, w = rows.shape
    d = wg.shape[2]

    def w_map(e, first, count):
        return (layer, e, 0, 0)

    any_space = pl.BlockSpec(memory_space=pl.ANY)
    grid_spec = pltpu.PrefetchScalarGridSpec(
        num_scalar_prefetch=2,
        grid=(N_EXPERTS,),
        in_specs=[any_space,
                  pl.BlockSpec((1, 1, d, D_EXPERT), w_map),
                  pl.BlockSpec((1, 1, d, D_EXPERT), w_map),
                  pl.BlockSpec((1, 1, D_EXPERT, d), w_map)],
        out_specs=any_space,
        scratch_shapes=[pltpu.VMEM((2, EXPERT_BIG * BM, w), rows.dtype),
                        pltpu.VMEM((2, EXPERT_BIG * BM, w), rows.dtype),
                        pltpu.VMEM((BM, w), rows.dtype),
                        pltpu.VMEM((BM, w), rows.dtype),
                        pltpu.VMEM((d, 2 * D_EXPERT), BF16),
                        pltpu.VMEM((D_EXPERT, d), BF16),
                        pltpu.SemaphoreType.DMA((2,)),
                        pltpu.SemaphoreType.DMA((2,)),
                        pltpu.SemaphoreType.DMA((2,))],
    )
    return pl.pallas_call(
        _expert_body,
        grid_spec=grid_spec,
        out_shape=jax.ShapeDtypeStruct((r, w), rows.dtype),
        compiler_params=_layer_params(),
        name="experts",
    )(first_blk, blk_count, rows, wg, wu, wd)


def _final(h, g1, g2, rprev, gfin):
    t, d = h.shape
    ts = TS_FINAL
    return pl.pallas_call(
        _final_body,
        grid=(t // ts,),
        in_specs=[_row_spec(ts, d), _row_spec(ts, d // 2), _row_spec(ts, d // 2),
                  _row_spec(ts, LANES),
                  _const_spec((1, d))],
        out_specs=_row_spec(ts, d),
        out_shape=jax.ShapeDtypeStruct((t, d), F32),
        compiler_params=_layer_params(),
        name="final_norm",
    )(h, g1, g2, rprev, gfin)


def _plan_body(cnt_ref, rt_ref, d1_ref, d2_ref, first_ref, count_ref):
    e1 = rt_ref[R_E1:R_E1 + 1, :]
    e2 = rt_ref[R_E2:R_E2 + 1, :]
    d1 = rt_ref[R_RANK1:R_RANK1 + 1, :]
    d2 = rt_ref[R_RANK2:R_RANK2 + 1, :]
    pb = jnp.int32(0)
    for e in range(N_EXPERTS):
        nb = lax.shift_right_logical(cnt_ref[e] + (BM - 1), BM.bit_length() - 1)
        ps = (pb * BM).astype(F32)
        d1 = d1 + jnp.where(e1 == e, ps, 0.0)
        d2 = d2 + jnp.where(e2 == e, ps, 0.0)
        first_ref[e] = pb
        count_ref[e] = nb
        pb = pb + nb
    d1_ref[...] = d1.astype(jnp.int32)
    d2_ref[...] = d2.astype(jnp.int32)


def _plan(route_t, counts):
    t = route_t.shape[1]
    smem = pl.BlockSpec(memory_space=pltpu.SMEM)
    vmem = pl.BlockSpec(memory_space=pltpu.VMEM)
    return pl.pallas_call(
        _plan_body,
        in_specs=[smem, vmem],
        out_specs=[vmem, vmem, smem, smem],
        out_shape=[jax.ShapeDtypeStruct((1, t), jnp.int32),
                   jax.ShapeDtypeStruct((1, t), jnp.int32),
                   jax.ShapeDtypeStruct((N_EXPERTS,), jnp.int32),
                   jax.ShapeDtypeStruct((N_EXPERTS,), jnp.int32)],
        name="moe_plan",
    )(counts, route_t)


def _sc_workers():
    info = plsc.get_sparse_core_info()
    return info.num_cores, info.num_cores * info.num_subcores


def _sc_mesh():
    return plsc.VectorSubcoreMesh(core_axis_name="c", subcore_axis_name="s")


def _sc_worker_id(num_cores):
    return lax.axis_index("s") * num_cores + lax.axis_index("c")


def _dispatch(hn, dest1, dest2, n_rows):
    t, w = hn.shape
    num_cores, n_workers = _sc_workers()
    per_w = t // n_workers
    chunk = SC_DISPATCH_CHUNK

    def body(hn_hbm, d1_hbm, d2_hbm, rows_hbm, buf, i1, i2):
        base_w = _sc_worker_id(num_cores) * per_w

        @pl.loop(0, per_w // chunk)
        def _(j):
            base = pl.multiple_of(base_w + j * chunk, chunk)
            pltpu.sync_copy(hn_hbm.at[pl.ds(base, chunk)], buf)
            pltpu.sync_copy(d1_hbm.at[:, pl.ds(base, chunk)], i1)
            pltpu.sync_copy(d2_hbm.at[:, pl.ds(base, chunk)], i2)
            pltpu.sync_copy(buf, rows_hbm.at[i1.at[0]])
            pltpu.sync_copy(buf, rows_hbm.at[i2.at[0]])

    return pl.kernel(
        body,
        out_type=jax.ShapeDtypeStruct((n_rows, w), hn.dtype),
        mesh=_sc_mesh(),
        scratch_types=[pltpu.VMEM((chunk, w), hn.dtype),
                       pltpu.VMEM((1, chunk), jnp.int32),
                       pltpu.VMEM((1, chunk), jnp.int32)],
        name="moe_dispatch",
    )(hn, dest1, dest2)


def _combine_gather(y, dest1, dest2):
    d = y.shape[1]
    t = dest1.shape[1]
    num_cores, n_workers = _sc_workers()
    per_w = t // n_workers
    chunk = SC_COMBINE_CHUNK

    def body(y_hbm, d1_hbm, d2_hbm, g1_hbm, g2_hbm, buf, idx):
        base_w = _sc_worker_id(num_cores) * per_w

        @pl.loop(0, per_w // chunk)
        def _(j):
            base = pl.multiple_of(base_w + j * chunk, chunk)
            for d_hbm, g_hbm in ((d1_hbm, g1_hbm), (d2_hbm, g2_hbm)):
                pltpu.sync_copy(d_hbm.at[:, pl.ds(base, chunk)], idx)
                pltpu.sync_copy(y_hbm.at[idx.at[0]], buf)
                pltpu.sync_copy(buf, g_hbm.at[pl.ds(base, chunk)])

    out = jax.ShapeDtypeStruct((t, d), y.dtype)
    return pl.kernel(
        body,
        out_type=(out, out),
        mesh=_sc_mesh(),
        scratch_types=[pltpu.VMEM((chunk, d), y.dtype),
                       pltpu.VMEM((1, chunk), jnp.int32)],
        name="moe_combine_gather",
    )(y, dest1, dest2)


def _moe(hn, route_t, cnt, wg, wu, wd, layer):
    t = hn.shape[0]
    n_blk = (2 * t) // BM + N_EXPERTS
    counts = cnt[:, 0].astype(jnp.int32)
    dest1, dest2, first_blk, blk_count = _plan(route_t, counts)
    rows = _dispatch(hn, dest1, dest2, n_blk * BM)
    y = _experts(rows, first_blk, blk_count, wg, wu, wd, layer)
    return _combine_gather(y, dest1, dest2)


def _router_inputs(gffn, w_group, b_group, w_expert, b_expert, ts):
    d = w_group.shape[0]
    wr = jnp.zeros((ROUTER_ROWS, d), F32)
    wr = wr.at[:N_EXPERTS].set(w_expert.T).at[GROUP_ROW0:GROUP_ROW0 + N_GROUPS].set(w_group.T)
    br = jnp.zeros((ROUTER_ROWS,), F32)
    br = br.at[:N_EXPERTS].set(b_expert).at[GROUP_ROW0:GROUP_ROW0 + N_GROUPS].set(b_group)
    w_hi = wr.astype(BF16)
    w_lo = (wr - w_hi.astype(F32)).astype(BF16)
    idx = jnp.arange(ts)
    triu = (idx[:, None] < idx[None, :]).astype(BF16)
    return (gffn.reshape(1, -1), w_hi, w_lo, jnp.broadcast_to(br[:, None], (ROUTER_ROWS, ts)), triu)


def kernel(x, norm_mix_g, norm_ffn_g, cv_w_pw1, cv_b_pw1, cv_w_dw, cv_b_dw, cv_ln_g, cv_ln_b, cv_w_pw2, cv_b_pw2, gm_w_in, gm_b_in, gm_v_norm_g, gm_w_s, gm_b_s, gm_w_out, gm_b_out, moe_w_group, moe_b_group, moe_w_expert, moe_b_expert, moe_w_gate, moe_w_up, moe_w_down, final_g):
    bsz, seq, d = x.shape
    t = bsz * seq
    x2 = x.reshape(t, d)
    row = lambda a: a.reshape(1, -1)

    router0 = _router_inputs(norm_ffn_g[0], moe_w_group[0], moe_b_group[0], moe_w_expert[0],
                             moe_b_expert[0], TS_CONV)
    h1, hn1, route0, route_t0, cnt0 = _conv_layer(
        x2, seq, row(norm_mix_g[0]), cv_w_pw1[0].astype(BF16), row(cv_b_pw1[0]),
        jnp.broadcast_to(cv_w_dw[0][:, None, :], (CONV_WIDTH, SUBLANES, d)),
        jnp.broadcast_to(cv_b_dw[0][None, :], (SUBLANES, d)),
        row(cv_ln_g[0]), row(cv_ln_b[0]), cv_w_pw2[0].astype(BF16),
        row(cv_b_pw2[0]), router0)
    ga0, gb0 = _moe(hn1, route_t0, cnt0, moe_w_gate, moe_w_up, moe_w_down, 0)

    idx = jnp.arange(GMLP_BLOCK)
    mask = (idx[None, :] // GMLP_CHUNK) <= (idx[:, None] // GMLP_CHUNK)
    ws = jnp.where(mask[None], gm_w_s[0], 0.0).astype(BF16)
    router1 = _router_inputs(norm_ffn_g[1], moe_w_group[1], moe_b_group[1], moe_w_expert[1],
                             moe_b_expert[1], TS_GMLP)
    h2, hn2, route1, route_t1, cnt1 = _gmlp_layer(
        h1, ga0, gb0, route0, row(norm_mix_g[1]), gm_w_in[0].astype(BF16), row(gm_b_in[0]),
        row(gm_v_norm_g[0]), ws, jnp.transpose(gm_b_s[0]), gm_w_out[0].astype(BF16),
        row(gm_b_out[0]), router1)
    ga1, gb1 = _moe(hn2, route_t1, cnt1, moe_w_gate, moe_w_up, moe_w_down, 1)

    out = _final(h2, ga1, gb1, route1, row(final_g))
    return out.reshape(bsz, seq, d)
```

```python
import functools

import jax
import jax.numpy as jnp
from jax import lax
from jax.experimental import pallas as pl
from jax.experimental.pallas import tpu as pltpu
from jax.experimental.pallas import tpu_sc as plsc

D_MODEL = 1024
CONV_WIDTH = 31
GMLP_BLOCK = 128
GMLP_CHUNK = 64
GMLP_INNER = 2 * D_MODEL
GMLP_HEADS = 8
GMLP_HEAD_DIM = GMLP_INNER // GMLP_HEADS
N_GROUPS = 4
EXPERTS_PER_GROUP = 8
N_EXPERTS = N_GROUPS * EXPERTS_PER_GROUP
D_EXPERT = D_MODEL // 2
EPS = 1e-6

LANES = 128
SUBLANES = 8
HIST = 32
TS_CONV = 512
TS_GMLP = 512
TS_FINAL = 512
BM = 512
IN_SLOTS = 3
RC = 32
CONV_CW = 256
GLU_CW = 256
GMLP_CW = 512
GROUP_ROW0 = N_EXPERTS
ROUTER_ROWS = 48
VMEM_LIMIT = 56 * 1024 * 1024
SC_DISPATCH_CHUNK = 128
SC_COMBINE_CHUNK = 128

R_E1, R_E2, R_RANK1, R_RANK2, R_GATE1, R_GATE2 = range(6)

F32 = jnp.float32
BF16 = jnp.bfloat16


def _gelu_tanh(x):
    c = 0.7978845608028654
    t = jnp.tanh(x * (c + (c * 0.044715) * (x * x)))
    hx = 0.5 * x
    return hx + hx * t


def _rms(xf, g):
    return xf * lax.rsqrt(jnp.mean(xf * xf, axis=-1, keepdims=True) + EPS) * g


def _pack_bf16_pairs(xb):
    w = xb.shape[1] // 2
    bits = lax.bitcast_convert_type(xb.astype(F32), jnp.int32)
    return lax.shift_right_logical(bits[:, :w], 16) | bits[:, w:]


def _unpack_pairs_f32(p):
    lo = lax.bitcast_convert_type(lax.shift_left(p, 16), F32)
    hi = lax.bitcast_convert_type(p & jnp.int32(-65536), F32)
    return lo, hi


def _moe_combine(h, rp, p1, p2):
    w = h.shape[1] // 2
    g1 = rp[:, R_GATE1:R_GATE1 + 1]
    g2 = rp[:, R_GATE2:R_GATE2 + 1]
    lo1, hi1 = _unpack_pairs_f32(p1)
    lo2, hi2 = _unpack_pairs_f32(p2)
    return jnp.concatenate([h[:, :w] + g1 * lo1 + g2 * lo2,
                            h[:, w:] + g1 * hi1 + g2 * hi2], axis=1)


def _row_loop(n_rows, fn, unroll=True):
    def step(ci, carry):
        fn(pl.ds(pl.multiple_of(ci * RC, RC), RC))
        return carry

    lax.fori_loop(0, n_rows // RC, step, 0, unroll=unroll)


def _route_tail(h_ref, gffn_ref, wrh_ref, wrl_ref, br_ref, triu_ref, run_ref,
                hn_out_ref, route_ref, route_t_ref, cnt_ref, hi_s, lo_s):
    ts = h_ref.shape[0]

    def norm_rows(rows):
        hn2 = _rms(h_ref[rows, :], gffn_ref[...])
        hi = hn2.astype(BF16)
        hf = hi.astype(F32)
        hi_s[rows, :] = hi
        lo_s[rows, :] = (hn2 - hf).astype(BF16)
        hn_out_ref[rows, :] = _pack_bf16_pairs(hi)

    _row_loop(ts, norm_rows)

    nt = (((1,), (1,)), ((), ()))
    hi = hi_s[...]
    lt = (lax.dot_general(wrh_ref[...], hi, nt, preferred_element_type=F32)
          + lax.dot_general(wrl_ref[...], hi, nt, preferred_element_type=F32)
          + lax.dot_general(wrh_ref[...], lo_s[...], nt, preferred_element_type=F32)
          + br_ref[...])
    sub = lax.broadcasted_iota(jnp.int32, (SUBLANES, ts), 0).astype(F32)
    ninf = jnp.float32(-jnp.inf)
    big = jnp.float32(1e9)
    first_idx = lambda hit: jnp.min(jnp.where(hit, sub, big), axis=0, keepdims=True)

    g_ok = sub < N_GROUPS
    lg = jnp.where(g_ok, lt[GROUP_ROW0:GROUP_ROW0 + SUBLANES, :], ninf)
    gmax = jnp.max(lg, axis=0, keepdims=True)
    gsel = first_idx(lg == gmax)
    p_g = 1.0 / jnp.sum(jnp.where(g_ok, jnp.exp(lg - gmax), 0.0), axis=0, keepdims=True)
    le = lt[0:EXPERTS_PER_GROUP, :]
    for g in range(1, N_GROUPS):
        le = jnp.where(gsel == g, lt[g * EXPERTS_PER_GROUP:(g + 1) * EXPERTS_PER_GROUP, :], le)
    v1 = jnp.max(le, axis=0, keepdims=True)
    i1 = first_idx(le == v1)
    le2 = jnp.where(sub == i1, ninf, le)
    v2 = jnp.max(le2, axis=0, keepdims=True)
    i2 = first_idx(le2 == v2)
    e = jnp.exp(v2 - v1)
    den = 1.0 + e
    gate1 = p_g * (1.0 / den)
    gate2 = p_g * (e / den)
    e1 = gsel * EXPERTS_PER_GROUP + i1
    e2 = gsel * EXPERTS_PER_GROUP + i2

    eid = lax.broadcasted_iota(jnp.int32, (N_EXPERTS, ts), 0).astype(F32)
    oh1 = eid == e1
    oh2 = eid == e2
    oh = jnp.where(oh1 | oh2, 1.0, 0.0)
    run = run_ref[...]
    c = jnp.dot(oh.astype(BF16), triu_ref[...], preferred_element_type=F32) + run
    rank1 = jnp.sum(jnp.where(oh1, c, 0.0), axis=0, keepdims=True)
    rank2 = jnp.sum(jnp.where(oh2, c, 0.0), axis=0, keepdims=True)
    run = run + jnp.broadcast_to(jnp.sum(oh, axis=1, keepdims=True), run.shape)
    run_ref[...] = run
    cnt_ref[...] = run[:, :LANES]

    rec_t = jnp.concatenate([e1, e2, rank1, rank2, gate1, gate2,
                             jnp.zeros((SUBLANES - 6, ts), F32)], axis=0)
    route_t_ref[...] = rec_t
    route_ref[...] = jnp.concatenate([rec_t, jnp.zeros((LANES - SUBLANES, ts), F32)], axis=0).T


def _conv_layer_body(tiles_per_seq,
                     x_ref, gmix_ref, wpw1_ref, bpw1_ref, wdw_ref, bdw_ref, lng_ref, lnb_ref,
                     wpw2_ref, bpw2_ref, gffn_ref, wrh_ref, wrl_ref, br_ref, triu_ref,
                     h_out_ref, hn_out_ref, route_ref, route_t_ref, cnt_ref,
                     hn_s, zext_ref, zs_ref, y_s, a_s, hi_s, lo_s, run_ref):
    i = pl.program_id(0)
    ts, d = x_ref.shape

    @pl.when(i == 0)
    def _():
        run_ref[...] = jnp.zeros_like(run_ref)

    @pl.when(i % tiles_per_seq == 0)
    def _():
        zext_ref[0:HIST, :] = jnp.zeros((HIST, d), F32)

    def norm_rows(rows):
        hn_s[rows, :] = _rms(x_ref[rows, :], gmix_ref[...]).astype(BF16)

    _row_loop(ts, norm_rows)

    hn = hn_s[...]
    for c0 in range(0, d, GLU_CW):
        ca = slice(c0, c0 + GLU_CW)
        cg = slice(d + c0, d + c0 + GLU_CW)
        pa = jnp.dot(hn, wpw1_ref[:, ca], preferred_element_type=F32) + bpw1_ref[:, ca]
        pg = jnp.dot(hn, wpw1_ref[:, cg], preferred_element_type=F32) + bpw1_ref[:, cg]
        zext_ref[HIST:HIST + ts, ca] = pa * jax.nn.sigmoid(pg)

    span = ts + HIST - SUBLANES
    sub_id = lax.broadcasted_iota(jnp.int32, (SUBLANES, LANES), 0)
    shifts = range(1, SUBLANES)
    for c0 in range(0, d, LANES):
        cols = slice(c0, c0 + LANES)

        def rotate(g):
            up = {0: g}
            for r in (4, 2, 6, 1, 3, 5, 7):
                step = r & -r
                up[r] = pltpu.roll(up[r - step], SUBLANES - step, axis=0)
            return [up[r] for r in shifts]

        cur = rotate(zext_ref[0:SUBLANES, cols])
        for m0 in range(0, span, SUBLANES):
            nxt = rotate(zext_ref[m0 + SUBLANES:m0 + 2 * SUBLANES, cols])
            for r in shifts:
                zs_ref[r - 1, m0:m0 + SUBLANES, cols] = jnp.where(
                    sub_id < SUBLANES - r, cur[r - 1], nxt[r - 1])
            cur = nxt

    first = HIST - (CONV_WIDTH - 1)

    def conv_rows(rows):
        r0 = rows.start
        groups = RC // SUBLANES
        for c0 in range(0, d, CONV_CW):
            cols = slice(c0, c0 + CONV_CW)
            accs = [bdw_ref[:, cols]] * groups
            for k in range(CONV_WIDTH):
                q, r = divmod(first + k, SUBLANES)
                w8 = wdw_ref[k, :, cols]
                for g in range(groups):
                    src = pl.ds(r0 + (q + g) * SUBLANES, SUBLANES)
                    slab = zext_ref[src, cols] if r == 0 else zs_ref[r - 1, src, cols]
                    accs[g] = accs[g] + w8 * slab
            for g in range(groups):
                y_s[pl.ds(r0 + g * SUBLANES, SUBLANES), cols] = accs[g]
        y = y_s[rows, :]
        mu = jnp.mean(y, axis=-1, keepdims=True)
        yc = y - mu
        yn = yc * lax.rsqrt(jnp.mean(yc * yc, axis=-1, keepdims=True) + EPS)
        yn = yn * lng_ref[...] + lnb_ref[...]
        a_s[rows, :] = (yn * jax.nn.sigmoid(yn)).astype(BF16)

    _row_loop(ts, conv_rows, unroll=True)
    zext_ref[0:HIST, :] = zext_ref[ts:ts + HIST, :]

    for r0 in range(0, ts, ts // 2):
        rows = slice(r0, r0 + ts // 2)
        m = jnp.dot(a_s[rows, :], wpw2_ref[...], preferred_element_type=F32) + bpw2_ref[...]
        h_out_ref[rows, :] = x_ref[rows, :] + m
    _route_tail(h_out_ref, gffn_ref, wrh_ref, wrl_ref, br_ref, triu_ref, run_ref,
                hn_out_ref, route_ref, route_t_ref, cnt_ref, hi_s, lo_s)


def _gmlp_layer_body(h_ref, g1_ref, g2_ref, rprev_ref, gmix_ref, win_ref, bin_ref, vg_ref,
                     ws_ref, bst_ref, wout_ref, bout_ref, gffn_ref, wrh_ref, wrl_ref, br_ref,
                     triu_ref,
                     h_out_ref, hn_out_ref, route_ref, route_t_ref, cnt_ref,
                     hn_s, u_s, v_s, gated_s, hi_s, lo_s, run_ref):
    i = pl.program_id(0)
    ts = h_ref.shape[0]

    @pl.when(i == 0)
    def _():
        run_ref[...] = jnp.zeros_like(run_ref)

    def norm_rows(rows):
        h = _moe_combine(h_ref[rows, :], rprev_ref[rows, :], g1_ref[rows, :], g2_ref[rows, :])
        h_out_ref[rows, :] = h
        hn_s[rows, :] = _rms(h, gmix_ref[...]).astype(BF16)

    _row_loop(ts, norm_rows)

    hn = hn_s[...]
    ssq = jnp.zeros((ts, 1), F32)
    for c0 in range(0, 2 * GMLP_INNER, GMLP_CW):
        cols = slice(c0, c0 + GMLP_CW)
        zc = _gelu_tanh(jnp.dot(hn, win_ref[:, cols], preferred_element_type=F32)
                        + bin_ref[:, cols])
        if c0 < GMLP_INNER:
            u_s[:, cols] = zc
        else:
            v_s[:, c0 - GMLP_INNER:c0 - GMLP_INNER + GMLP_CW] = zc
            ssq = ssq + jnp.sum(zc * zc, axis=-1, keepdims=True)
    rs = lax.rsqrt(ssq * (1.0 / GMLP_INNER) + EPS)

    for b0 in range(0, ts, GMLP_BLOCK):
        rows = slice(b0, b0 + GMLP_BLOCK)
        for hd in range(GMLP_HEADS):
            cols = slice(hd * GMLP_HEAD_DIM, (hd + 1) * GMLP_HEAD_DIM)
            vv = (v_s[rows, cols] * rs[rows] * vg_ref[:, cols]).astype(BF16)
            sv = jnp.dot(ws_ref[hd], vv, preferred_element_type=F32) + bst_ref[:, hd:hd + 1]
            gated_s[rows, cols] = (u_s[rows, cols] * sv).astype(BF16)

    out = jnp.dot(gated_s[...], wout_ref[...], preferred_element_type=F32) + bout_ref[...]
    h_out_ref[...] = h_out_ref[...] + out
    _route_tail(h_out_ref, gffn_ref, wrh_ref, wrl_ref, br_ref, triu_ref, run_ref,
                hn_out_ref, route_ref, route_t_ref, cnt_ref, hi_s, lo_s)


def _expert_body(first_ref, count_ref, nused_ref, rows_hbm, wg_ref, wu_ref, wd_ref, y_hbm,
                 xbuf, ybuf, wgu_s, wd_s, in_sem, out_sem):
    e = pl.program_id(0)
    first = first_ref[e]
    count = count_ref[e]
    n_used = nused_ref[0]

    def rows_of(g):
        return pl.ds(pl.multiple_of(g * BM, BM), BM)

    def in_copy(g):
        slot = lax.rem(g, IN_SLOTS)
        return pltpu.make_async_copy(rows_hbm.at[rows_of(g)], xbuf.at[slot], in_sem.at[slot])

    def out_copy(g):
        slot = g & 1
        return pltpu.make_async_copy(ybuf.at[slot], y_hbm.at[rows_of(g)], out_sem.at[slot])

    @pl.when(e == 0)
    def _():
        for g in range(IN_SLOTS - 1):
            @pl.when(g < n_used)
            def _():
                in_copy(g).start()

    @pl.when(count > 0)
    def _():
        wgu_s[:, :D_EXPERT] = wg_ref[0, 0].astype(BF16)
        wgu_s[:, D_EXPERT:] = wu_ref[0, 0].astype(BF16)
        wd_s[...] = wd_ref[0, 0].astype(BF16)

        def block(g, carry):
            in_copy(g).wait()

            @pl.when(g + IN_SLOTS - 1 < n_used)
            def _():
                in_copy(g + IN_SLOTS - 1).start()

            @pl.when(g >= 2)
            def _():
                out_copy(g - 2).wait()

            lo, hi = _unpack_pairs_f32(xbuf[lax.rem(g, IN_SLOTS)])
            half = lo.shape[1]
            gu = (jnp.dot(lo.astype(BF16), wgu_s[:half, :], preferred_element_type=F32)
                  + jnp.dot(hi.astype(BF16), wgu_s[half:, :], preferred_element_type=F32))
            gate = gu[:, :D_EXPERT]
            hb = (gate * jax.nn.sigmoid(gate)) * gu[:, D_EXPERT:]
            y = jnp.dot(hb.astype(BF16), wd_s[...], preferred_element_type=F32)
            ybuf[g & 1] = _pack_bf16_pairs(y.astype(BF16))
            out_copy(g).start()
            return carry

        lax.fori_loop(first, first + count, block, 0)

    @pl.when(e == pl.num_programs(0) - 1)
    def _():
        @pl.when(n_used >= 2)
        def _():
            out_copy(n_used - 2).wait()

        out_copy(n_used - 1).wait()


def _final_body(h_ref, g1_ref, g2_ref, rprev_ref, gfin_ref, o_ref):
    h = _moe_combine(h_ref[...], rprev_ref[...], g1_ref[...], g2_ref[...])
    o_ref[...] = _rms(h, gfin_ref[...])


def _const_spec(shape):
    return pl.BlockSpec(shape, lambda i: (0,) * len(shape), pipeline_mode=pl.Buffered(1))


def _row_spec(ts, width):
    return pl.BlockSpec((ts, width), lambda i: (i, 0))


def _layer_out(t, d, ts):
    shapes = [jax.ShapeDtypeStruct((t, d), F32),
              jax.ShapeDtypeStruct((t, d // 2), jnp.int32),
              jax.ShapeDtypeStruct((t, LANES), F32),
              jax.ShapeDtypeStruct((SUBLANES, t), F32),
              jax.ShapeDtypeStruct((N_EXPERTS, LANES), F32)]
    specs = [_row_spec(ts, d), _row_spec(ts, d // 2), _row_spec(ts, LANES),
             pl.BlockSpec((SUBLANES, ts), lambda i: (0, i)),
             pl.BlockSpec((N_EXPERTS, LANES), lambda i: (0, 0))]
    return shapes, specs


def _router_specs(d, ts):
    return [_const_spec((1, d)), _const_spec((ROUTER_ROWS, d)), _const_spec((ROUTER_ROWS, d)),
            _const_spec((ROUTER_ROWS, ts)), _const_spec((ts, ts))]


def _router_scratch(ts, d):
    return [pltpu.VMEM((ts, d), BF16), pltpu.VMEM((ts, d), BF16),
            pltpu.VMEM((N_EXPERTS, ts), F32)]


def _layer_params():
    return pltpu.CompilerParams(dimension_semantics=("arbitrary",), vmem_limit_bytes=VMEM_LIMIT)


def _conv_layer(x2, seq, gmix, wpw1, bpw1, wdw, bdw, lng, lnb, wpw2, bpw2, router):
    t, d = x2.shape
    ts = TS_CONV
    shapes, out_specs = _layer_out(t, d, ts)
    body = functools.partial(_conv_layer_body, seq // ts)
    return pl.pallas_call(
        body,
        grid=(t // ts,),
        in_specs=[_row_spec(ts, d), _const_spec((1, d)), _const_spec((d, 2 * d)),
                  _const_spec((1, 2 * d)), _const_spec((CONV_WIDTH, SUBLANES, d)),
                  _const_spec((SUBLANES, d)),
                  _const_spec((1, d)), _const_spec((1, d)), _const_spec((d, d)),
                  _const_spec((1, d))] + _router_specs(d, ts),
        out_specs=out_specs,
        out_shape=shapes,
        scratch_shapes=[pltpu.VMEM((ts, d), BF16),
                        pltpu.VMEM((ts + HIST, d), F32),
                        pltpu.VMEM((SUBLANES - 1, ts + HIST - SUBLANES, d), F32),
                        pltpu.VMEM((ts, d), F32),
                        pltpu.VMEM((ts, d), BF16)] + _router_scratch(ts, d),
        compiler_params=_layer_params(),
        name="conv_layer",
    )(x2, gmix, wpw1, bpw1, wdw, bdw, lng, lnb, wpw2, bpw2, *router)


def _gmlp_layer(h, g1, g2, rprev, gmix, win, bin_, vg, ws, bst, wout, bout, router):
    t, d = h.shape
    ts = TS_GMLP
    shapes, out_specs = _layer_out(t, d, ts)
    return pl.pallas_call(
        _gmlp_layer_body,
        grid=(t // ts,),
        in_specs=[_row_spec(ts, d), _row_spec(ts, d // 2), _row_spec(ts, d // 2),
                  _row_spec(ts, LANES),
                  _const_spec((1, d)), _const_spec((d, 2 * GMLP_INNER)),
                  _const_spec((1, 2 * GMLP_INNER)), _const_spec((1, GMLP_INNER)),
                  _const_spec((GMLP_HEADS, GMLP_BLOCK, GMLP_BLOCK)),
                  _const_spec((GMLP_BLOCK, GMLP_HEADS)), _const_spec((GMLP_INNER, d)),
                  _const_spec((1, d))] + _router_specs(d, ts),
        out_specs=out_specs,
        out_shape=shapes,
        scratch_shapes=[pltpu.VMEM((ts, d), BF16),
                        pltpu.VMEM((ts, GMLP_INNER), F32),
                        pltpu.VMEM((ts, GMLP_INNER), F32),
                        pltpu.VMEM((ts, GMLP_INNER), BF16)] + _router_scratch(ts, d),
        compiler_params=_layer_params(),
        name="gmlp_layer",
    )(h, g1, g2, rprev, gmix, win, bin_, vg, ws, bst, wout, bout, *router)


def _experts(rows, first_blk, blk_count, n_used, wg, wu, wd, layer):
    r, w = rows.shape
    d = wg.shape[2]

    def w_map(e, first, count, nu):
        return (layer, e, 0, 0)

    any_space = pl.BlockSpec(memory_space=pl.ANY)
    grid_spec = pltpu.PrefetchScalarGridSpec(
        num_scalar_prefetch=3,
        grid=(N_EXPERTS,),
        in_specs=[any_space,
                  pl.BlockSpec((1, 1, d, D_EXPERT), w_map),
                  pl.BlockSpec((1, 1, d, D_EXPERT), w_map),
                  pl.BlockSpec((1, 1, D_EXPERT, d), w_map)],
        out_specs=any_space,
        scratch_shapes=[pltpu.VMEM((IN_SLOTS, BM, w), rows.dtype),
                        pltpu.VMEM((2, BM, w), rows.dtype),
                        pltpu.VMEM((d, 2 * D_EXPERT), BF16),
                        pltpu.VMEM((D_EXPERT, d), BF16),
                        pltpu.SemaphoreType.DMA((IN_SLOTS,)),
                        pltpu.SemaphoreType.DMA((2,))],
    )
    return pl.pallas_call(
        _expert_body,
        grid_spec=grid_spec,
        out_shape=jax.ShapeDtypeStruct((r, w), rows.dtype),
        compiler_params=_layer_params(),
        name="experts",
    )(first_blk, blk_count, n_used, rows, wg, wu, wd)


def _final(h, g1, g2, rprev, gfin):
    t, d = h.shape
    ts = TS_FINAL
    return pl.pallas_call(
        _final_body,
        grid=(t // ts,),
        in_specs=[_row_spec(ts, d), _row_spec(ts, d // 2), _row_spec(ts, d // 2),
                  _row_spec(ts, LANES),
                  _const_spec((1, d))],
        out_specs=_row_spec(ts, d),
        out_shape=jax.ShapeDtypeStruct((t, d), F32),
        compiler_params=_layer_params(),
        name="final_norm",
    )(h, g1, g2, rprev, gfin)


def _plan_body(cnt_ref, rt_ref, d1_ref, d2_ref, first_ref, count_ref, nu_ref):
    e1 = rt_ref[R_E1:R_E1 + 1, :]
    e2 = rt_ref[R_E2:R_E2 + 1, :]
    d1 = rt_ref[R_RANK1:R_RANK1 + 1, :]
    d2 = rt_ref[R_RANK2:R_RANK2 + 1, :]
    pb = jnp.int32(0)
    for e in range(N_EXPERTS):
        nb = lax.shift_right_logical(cnt_ref[e] + (BM - 1), BM.bit_length() - 1)
        ps = (pb * BM).astype(F32)
        d1 = d1 + jnp.where(e1 == e, ps, 0.0)
        d2 = d2 + jnp.where(e2 == e, ps, 0.0)
        first_ref[e] = pb
        count_ref[e] = nb
        pb = pb + nb
    nu_ref[0] = pb
    d1_ref[...] = d1.astype(jnp.int32)
    d2_ref[...] = d2.astype(jnp.int32)


def _plan(route_t, counts):
    t = route_t.shape[1]
    smem = pl.BlockSpec(memory_space=pltpu.SMEM)
    vmem = pl.BlockSpec(memory_space=pltpu.VMEM)
    return pl.pallas_call(
        _plan_body,
        in_specs=[smem, vmem],
        out_specs=[vmem, vmem, smem, smem, smem],
        out_shape=[jax.ShapeDtypeStruct((1, t), jnp.int32),
                   jax.ShapeDtypeStruct((1, t), jnp.int32),
                   jax.ShapeDtypeStruct((N_EXPERTS,), jnp.int32),
                   jax.ShapeDtypeStruct((N_EXPERTS,), jnp.int32),
                   jax.ShapeDtypeStruct((1,), jnp.int32)],
        name="moe_plan",
    )(counts, route_t)


def _sc_workers():
    info = plsc.get_sparse_core_info()
    return info.num_cores, info.num_cores * info.num_subcores


def _sc_mesh():
    return plsc.VectorSubcoreMesh(core_axis_name="c", subcore_axis_name="s")


def _sc_worker_id(num_cores):
    return lax.axis_index("s") * num_cores + lax.axis_index("c")


def _dispatch(hn, dest1, dest2, n_rows):
    t, w = hn.shape
    num_cores, n_workers = _sc_workers()
    per_w = t // n_workers
    chunk = SC_DISPATCH_CHUNK

    def body(hn_hbm, d1_hbm, d2_hbm, rows_hbm, buf, i1, i2):
        base_w = _sc_worker_id(num_cores) * per_w

        @pl.loop(0, per_w // chunk)
        def _(j):
            base = pl.multiple_of(base_w + j * chunk, chunk)
            pltpu.sync_copy(hn_hbm.at[pl.ds(base, chunk)], buf)
            pltpu.sync_copy(d1_hbm.at[:, pl.ds(base, chunk)], i1)
            pltpu.sync_copy(d2_hbm.at[:, pl.ds(base, chunk)], i2)
            pltpu.sync_copy(buf, rows_hbm.at[i1.at[0]])
            pltpu.sync_copy(buf, rows_hbm.at[i2.at[0]])

    return pl.kernel(
        body,
        out_type=jax.ShapeDtypeStruct((n_rows, w), hn.dtype),
        mesh=_sc_mesh(),
        scratch_types=[pltpu.VMEM((chunk, w), hn.dtype),
                       pltpu.VMEM((1, chunk), jnp.int32),
                       pltpu.VMEM((1, chunk), jnp.int32)],
        name="moe_dispatch",
    )(hn, dest1, dest2)


def _combine_gather(y, dest1, dest2):
    d = y.shape[1]
    t = dest1.shape[1]
    num_cores, n_workers = _sc_workers()
    per_w = t // n_workers
    chunk = SC_COMBINE_CHUNK

    def body(y_hbm, d1_hbm, d2_hbm, g1_hbm, g2_hbm, buf, idx):
        base_w = _sc_worker_id(num_cores) * per_w

        @pl.loop(0, per_w // chunk)
        def _(j):
            base = pl.multiple_of(base_w + j * chunk, chunk)
            for d_hbm, g_hbm in ((d1_hbm, g1_hbm), (d2_hbm, g2_hbm)):
                pltpu.sync_copy(d_hbm.at[:, pl.ds(base, chunk)], idx)
                pltpu.sync_copy(y_hbm.at[idx.at[0]], buf)
                pltpu.sync_copy(buf, g_hbm.at[pl.ds(base, chunk)])

    out = jax.ShapeDtypeStruct((t, d), y.dtype)
    return pl.kernel(
        body,
        out_type=(out, out),
        mesh=_sc_mesh(),
        scratch_types=[pltpu.VMEM((chunk, d), y.dtype),
                       pltpu.VMEM((1, chunk), jnp.int32)],
        name="moe_combine_gather",
    )(y, dest1, dest2)


def _moe(hn, route_t, cnt, wg, wu, wd, layer):
    t = hn.shape[0]
    n_blk = (2 * t) // BM + N_EXPERTS
    counts = cnt[:, 0].astype(jnp.int32)
    dest1, dest2, first_blk, blk_count, n_used = _plan(route_t, counts)
    rows = _dispatch(hn, dest1, dest2, n_blk * BM)
    y = _experts(rows, first_blk, blk_count, n_used, wg, wu, wd, layer)
    return _combine_gather(y, dest1, dest2)


def _router_inputs(gffn, w_group, b_group, w_expert, b_expert, ts):
    d = w_group.shape[0]
    wr = jnp.zeros((ROUTER_ROWS, d), F32)
    wr = wr.at[:N_EXPERTS].set(w_expert.T).at[GROUP_ROW0:GROUP_ROW0 + N_GROUPS].set(w_group.T)
    br = jnp.zeros((ROUTER_ROWS,), F32)
    br = br.at[:N_EXPERTS].set(b_expert).at[GROUP_ROW0:GROUP_ROW0 + N_GROUPS].set(b_group)
    w_hi = wr.astype(BF16)
    w_lo = (wr - w_hi.astype(F32)).astype(BF16)
    idx = jnp.arange(ts)
    triu = (idx[:, None] < idx[None, :]).astype(BF16)
    return (gffn.reshape(1, -1), w_hi, w_lo, jnp.broadcast_to(br[:, None], (ROUTER_ROWS, ts)), triu)


def kernel(x, norm_mix_g, norm_ffn_g, cv_w_pw1, cv_b_pw1, cv_w_dw, cv_b_dw, cv_ln_g, cv_ln_b, cv_w_pw2, cv_b_pw2, gm_w_in, gm_b_in, gm_v_norm_g, gm_w_s, gm_b_s, gm_w_out, gm_b_out, moe_w_group, moe_b_group, moe_w_expert, moe_b_expert, moe_w_gate, moe_w_up, moe_w_down, final_g):
    bsz, seq, d = x.shape
    t = bsz * seq
    x2 = x.reshape(t, d)
    row = lambda a: a.reshape(1, -1)

    router0 = _router_inputs(norm_ffn_g[0], moe_w_group[0], moe_b_group[0], moe_w_expert[0],
                             moe_b_expert[0], TS_CONV)
    h1, hn1, route0, route_t0, cnt0 = _conv_layer(
        x2, seq, row(norm_mix_g[0]), cv_w_pw1[0].astype(BF16), row(cv_b_pw1[0]),
        jnp.broadcast_to(cv_w_dw[0][:, None, :], (CONV_WIDTH, SUBLANES, d)),
        jnp.broadcast_to(cv_b_dw[0][None, :], (SUBLANES, d)),
        row(cv_ln_g[0]), row(cv_ln_b[0]), cv_w_pw2[0].astype(BF16),
        row(cv_b_pw2[0]), router0)
    ga0, gb0 = _moe(hn1, route_t0, cnt0, moe_w_gate, moe_w_up, moe_w_down, 0)

    idx = jnp.arange(GMLP_BLOCK)
    mask = (idx[None, :] // GMLP_CHUNK) <= (idx[:, None] // GMLP_CHUNK)
    ws = jnp.where(mask[None], gm_w_s[0], 0.0).astype(BF16)
    router1 = _router_inputs(norm_ffn_g[1], moe_w_group[1], moe_b_group[1], moe_w_expert[1],
                             moe_b_expert[1], TS_GMLP)
    h2, hn2, route1, route_t1, cnt1 = _gmlp_layer(
        h1, ga0, gb0, route0, row(norm_mix_g[1]), gm_w_in[0].astype(BF16), row(gm_b_in[0]),
        row(gm_v_norm_g[0]), ws, jnp.transpose(gm_b_s[0]), gm_w_out[0].astype(BF16),
        row(gm_b_out[0]), router1)
    ga1, gb1 = _moe(hn2, route_t1, cnt1, moe_w_gate, moe_w_up, moe_w_down, 1)

    out = _final(h2, ga1, gb1, route1, row(final_g))
    return out.reshape(bsz, seq, d)
```

```python
import functools

import jax
import jax.numpy as jnp
from jax import lax
from jax.experimental import pallas as pl
from jax.experimental.pallas import tpu as pltpu
from jax.experimental.pallas import tpu_sc as plsc

D_MODEL = 1024
CONV_WIDTH = 31
GMLP_BLOCK = 128
GMLP_CHUNK = 64
GMLP_INNER = 2 * D_MODEL
GMLP_HEADS = 8
GMLP_HEAD_DIM = GMLP_INNER // GMLP_HEADS
N_GROUPS = 4
EXPERTS_PER_GROUP = 8
N_EXPERTS = N_GROUPS * EXPERTS_PER_GROUP
D_EXPERT = D_MODEL // 2
EPS = 1e-6

LANES = 128
SUBLANES = 8
HIST = 32
TS_CONV = 512
TS_GMLP = 512
TS_FINAL = 1024
BM = 512
IN_SLOTS = 4
OUT_SLOTS = 3
RC = 32
CONV_CW = 256
GLU_CW = 256
GMLP_CW = 512
GROUP_ROW0 = N_EXPERTS
ROUTER_ROWS = 48
VMEM_LIMIT = 56 * 1024 * 1024
SC_DISPATCH_CHUNK = 128
SC_COMBINE_CHUNK = 128

R_E1, R_E2, R_RANK1, R_RANK2, R_GATE1, R_GATE2 = range(6)

F32 = jnp.float32
BF16 = jnp.bfloat16


def _gelu_tanh(x):
    c = 0.7978845608028654
    t = jnp.tanh(x * (c + (c * 0.044715) * (x * x)))
    hx = 0.5 * x
    return hx + hx * t


def _rms(xf, g):
    return xf * lax.rsqrt(jnp.mean(xf * xf, axis=-1, keepdims=True) + EPS) * g


def _pack_bf16_pairs(xb):
    w = xb.shape[1] // 2
    bits = lax.bitcast_convert_type(xb.astype(F32), jnp.int32)
    return lax.shift_right_logical(bits[:, :w], 16) | bits[:, w:]


def _unpack_pairs_f32(p):
    lo = lax.bitcast_convert_type(lax.shift_left(p, 16), F32)
    hi = lax.bitcast_convert_type(p & jnp.int32(-65536), F32)
    return lo, hi


def _moe_combine(h, rp, p1, p2):
    w = h.shape[1] // 2
    g1 = rp[:, R_GATE1:R_GATE1 + 1]
    g2 = rp[:, R_GATE2:R_GATE2 + 1]
    lo1, hi1 = _unpack_pairs_f32(p1)
    lo2, hi2 = _unpack_pairs_f32(p2)
    return jnp.concatenate([h[:, :w] + g1 * lo1 + g2 * lo2,
                            h[:, w:] + g1 * hi1 + g2 * hi2], axis=1)


def _row_loop(n_rows, fn, unroll=True):
    def step(ci, carry):
        fn(pl.ds(pl.multiple_of(ci * RC, RC), RC))
        return carry

    lax.fori_loop(0, n_rows // RC, step, 0, unroll=unroll)


def _route_tail(h_ref, gffn_ref, wrh_ref, wrl_ref, br_ref, triu_ref, run_ref,
                hn_out_ref, route_ref, route_t_ref, cnt_ref, hi_s, lo_s):
    ts = h_ref.shape[0]

    def norm_rows(rows):
        hn2 = _rms(h_ref[rows, :], gffn_ref[...])
        hi = hn2.astype(BF16)
        hf = hi.astype(F32)
        hi_s[rows, :] = hi
        lo_s[rows, :] = (hn2 - hf).astype(BF16)
        hn_out_ref[rows, :] = _pack_bf16_pairs(hi)

    _row_loop(ts, norm_rows)

    nt = (((1,), (1,)), ((), ()))
    hi = hi_s[...]
    lt = (lax.dot_general(wrh_ref[...], hi, nt, preferred_element_type=F32)
          + lax.dot_general(wrl_ref[...], hi, nt, preferred_element_type=F32)
          + lax.dot_general(wrh_ref[...], lo_s[...], nt, preferred_element_type=F32)
          + br_ref[...])
    sub = lax.broadcasted_iota(jnp.int32, (SUBLANES, ts), 0).astype(F32)
    ninf = jnp.float32(-jnp.inf)
    big = jnp.float32(1e9)
    first_idx = lambda hit: jnp.min(jnp.where(hit, sub, big), axis=0, keepdims=True)

    g_ok = sub < N_GROUPS
    lg = jnp.where(g_ok, lt[GROUP_ROW0:GROUP_ROW0 + SUBLANES, :], ninf)
    gmax = jnp.max(lg, axis=0, keepdims=True)
    gsel = first_idx(lg == gmax)
    p_g = 1.0 / jnp.sum(jnp.where(g_ok, jnp.exp(lg - gmax), 0.0), axis=0, keepdims=True)
    le = lt[0:EXPERTS_PER_GROUP, :]
    for g in range(1, N_GROUPS):
        le = jnp.where(gsel == g, lt[g * EXPERTS_PER_GROUP:(g + 1) * EXPERTS_PER_GROUP, :], le)
    v1 = jnp.max(le, axis=0, keepdims=True)
    i1 = first_idx(le == v1)
    le2 = jnp.where(sub == i1, ninf, le)
    v2 = jnp.max(le2, axis=0, keepdims=True)
    i2 = first_idx(le2 == v2)
    e = jnp.exp(v2 - v1)
    den = 1.0 + e
    gate1 = p_g * (1.0 / den)
    gate2 = p_g * (e / den)
    e1 = gsel * EXPERTS_PER_GROUP + i1
    e2 = gsel * EXPERTS_PER_GROUP + i2

    eid = lax.broadcasted_iota(jnp.int32, (N_EXPERTS, ts), 0).astype(F32)
    oh1 = eid == e1
    oh2 = eid == e2
    oh = jnp.where(oh1 | oh2, 1.0, 0.0)
    run = run_ref[...]
    c = jnp.dot(oh.astype(BF16), triu_ref[...], preferred_element_type=F32) + run
    rank1 = jnp.sum(jnp.where(oh1, c, 0.0), axis=0, keepdims=True)
    rank2 = jnp.sum(jnp.where(oh2, c, 0.0), axis=0, keepdims=True)
    run = run + jnp.broadcast_to(jnp.sum(oh, axis=1, keepdims=True), run.shape)
    run_ref[...] = run
    cnt_ref[...] = run[:, :LANES]

    rec_t = jnp.concatenate([e1, e2, rank1, rank2, gate1, gate2,
                             jnp.zeros((SUBLANES - 6, ts), F32)], axis=0)
    route_t_ref[...] = rec_t
    route_ref[...] = jnp.concatenate([rec_t, jnp.zeros((LANES - SUBLANES, ts), F32)], axis=0).T


def _conv_layer_body(tiles_per_seq,
                     x_ref, gmix_ref, wpw1_ref, bpw1_ref, wdw_ref, bdw_ref, lng_ref, lnb_ref,
                     wpw2_ref, bpw2_ref, gffn_ref, wrh_ref, wrl_ref, br_ref, triu_ref,
                     h_out_ref, hn_out_ref, route_ref, route_t_ref, cnt_ref,
                     hn_s, zext_ref, zs_ref, y_s, a_s, hi_s, lo_s, run_ref):
    i = pl.program_id(0)
    ts, d = x_ref.shape

    @pl.when(i == 0)
    def _():
        run_ref[...] = jnp.zeros_like(run_ref)

    @pl.when(i % tiles_per_seq == 0)
    def _():
        zext_ref[0:HIST, :] = jnp.zeros((HIST, d), F32)

    def norm_rows(rows):
        hn_s[rows, :] = _rms(x_ref[rows, :], gmix_ref[...]).astype(BF16)

    _row_loop(ts, norm_rows)

    hn = hn_s[...]
    for c0 in range(0, d, GLU_CW):
        ca = slice(c0, c0 + GLU_CW)
        cg = slice(d + c0, d + c0 + GLU_CW)
        pa = jnp.dot(hn, wpw1_ref[:, ca], preferred_element_type=F32) + bpw1_ref[:, ca]
        pg = jnp.dot(hn, wpw1_ref[:, cg], preferred_element_type=F32) + bpw1_ref[:, cg]
        zext_ref[HIST:HIST + ts, ca] = pa * jax.nn.sigmoid(pg)

    span = ts + HIST - SUBLANES
    sub_id = lax.broadcasted_iota(jnp.int32, (SUBLANES, LANES), 0)
    shifts = range(1, SUBLANES)
    for c0 in range(0, d, LANES):
        cols = slice(c0, c0 + LANES)

        def rotate(g):
            up = {0: g}
            for r in (4, 2, 6, 1, 3, 5, 7):
                step = r & -r
                up[r] = pltpu.roll(up[r - step], SUBLANES - step, axis=0)
            return [up[r] for r in shifts]

        cur = rotate(zext_ref[0:SUBLANES, cols])
        for m0 in range(0, span, SUBLANES):
            nxt = rotate(zext_ref[m0 + SUBLANES:m0 + 2 * SUBLANES, cols])
            for r in shifts:
                zs_ref[r - 1, m0:m0 + SUBLANES, cols] = jnp.where(
                    sub_id < SUBLANES - r, cur[r - 1], nxt[r - 1])
            cur = nxt

    first = HIST - (CONV_WIDTH - 1)

    def conv_rows(rows):
        r0 = rows.start
        groups = RC // SUBLANES
        for c0 in range(0, d, CONV_CW):
            cols = slice(c0, c0 + CONV_CW)
            accs = [bdw_ref[:, cols]] * groups
            for k in range(CONV_WIDTH):
                q, r = divmod(first + k, SUBLANES)
                w8 = wdw_ref[k, :, cols]
                for g in range(groups):
                    src = pl.ds(r0 + (q + g) * SUBLANES, SUBLANES)
                    slab = zext_ref[src, cols] if r == 0 else zs_ref[r - 1, src, cols]
                    accs[g] = accs[g] + w8 * slab
            for g in range(groups):
                y_s[pl.ds(r0 + g * SUBLANES, SUBLANES), cols] = accs[g]
        y = y_s[rows, :]
        mu = jnp.mean(y, axis=-1, keepdims=True)
        yc = y - mu
        yn = yc * lax.rsqrt(jnp.mean(yc * yc, axis=-1, keepdims=True) + EPS)
        yn = yn * lng_ref[...] + lnb_ref[...]
        a_s[rows, :] = (yn * jax.nn.sigmoid(yn)).astype(BF16)

    _row_loop(ts, conv_rows, unroll=True)
    zext_ref[0:HIST, :] = zext_ref[ts:ts + HIST, :]

    for r0 in range(0, ts, ts // 2):
        rows = slice(r0, r0 + ts // 2)
        m = jnp.dot(a_s[rows, :], wpw2_ref[...], preferred_element_type=F32) + bpw2_ref[...]
        h_out_ref[rows, :] = x_ref[rows, :] + m
    _route_tail(h_out_ref, gffn_ref, wrh_ref, wrl_ref, br_ref, triu_ref, run_ref,
                hn_out_ref, route_ref, route_t_ref, cnt_ref, hi_s, lo_s)


def _gmlp_layer_body(h_ref, g1_ref, g2_ref, rprev_ref, gmix_ref, win_ref, bin_ref, vg_ref,
                     ws_ref, bst_ref, wout_ref, bout_ref, gffn_ref, wrh_ref, wrl_ref, br_ref,
                     triu_ref,
                     h_out_ref, hn_out_ref, route_ref, route_t_ref, cnt_ref,
                     hn_s, u_s, v_s, gated_s, hi_s, lo_s, run_ref):
    i = pl.program_id(0)
    ts = h_ref.shape[0]

    @pl.when(i == 0)
    def _():
        run_ref[...] = jnp.zeros_like(run_ref)

    def norm_rows(rows):
        h = _moe_combine(h_ref[rows, :], rprev_ref[rows, :], g1_ref[rows, :], g2_ref[rows, :])
        h_out_ref[rows, :] = h
        hn_s[rows, :] = _rms(h, gmix_ref[...]).astype(BF16)

    _row_loop(ts, norm_rows)

    hn = hn_s[...]
    ssq = jnp.zeros((ts, 1), F32)
    for c0 in range(0, 2 * GMLP_INNER, GMLP_CW):
        cols = slice(c0, c0 + GMLP_CW)
        zc = _gelu_tanh(jnp.dot(hn, win_ref[:, cols], preferred_element_type=F32)
                        + bin_ref[:, cols])
        if c0 < GMLP_INNER:
            u_s[:, cols] = zc
        else:
            v_s[:, c0 - GMLP_INNER:c0 - GMLP_INNER + GMLP_CW] = zc
            ssq = ssq + jnp.sum(zc * zc, axis=-1, keepdims=True)
    rs = lax.rsqrt(ssq * (1.0 / GMLP_INNER) + EPS)

    for b0 in range(0, ts, GMLP_BLOCK):
        rows = slice(b0, b0 + GMLP_BLOCK)
        for hd in range(GMLP_HEADS):
            cols = slice(hd * GMLP_HEAD_DIM, (hd + 1) * GMLP_HEAD_DIM)
            vv = (v_s[rows, cols] * rs[rows] * vg_ref[:, cols]).astype(BF16)
            sv = jnp.dot(ws_ref[hd], vv, preferred_element_type=F32) + bst_ref[:, hd:hd + 1]
            gated_s[rows, cols] = (u_s[rows, cols] * sv).astype(BF16)

    out = jnp.dot(gated_s[...], wout_ref[...], preferred_element_type=F32) + bout_ref[...]
    h_out_ref[...] = h_out_ref[...] + out
    _route_tail(h_out_ref, gffn_ref, wrh_ref, wrl_ref, br_ref, triu_ref, run_ref,
                hn_out_ref, route_ref, route_t_ref, cnt_ref, hi_s, lo_s)


def _expert_body(first_ref, count_ref, nused_ref, rows_hbm, wg_ref, wu_ref, wd_ref, y_hbm,
                 xbuf, ybuf, wgu_s, wd_s, in_sem, out_sem):
    e = pl.program_id(0)
    first = first_ref[e]
    count = count_ref[e]
    n_used = nused_ref[0]

    def rows_of(g):
        return pl.ds(pl.multiple_of(g * BM, BM), BM)

    def in_copy(g):
        slot = lax.rem(g, IN_SLOTS)
        return pltpu.make_async_copy(rows_hbm.at[rows_of(g)], xbuf.at[slot], in_sem.at[slot])

    def out_copy(g):
        slot = lax.rem(g, OUT_SLOTS)
        return pltpu.make_async_copy(ybuf.at[slot], y_hbm.at[rows_of(g)], out_sem.at[slot])

    @pl.when(e == 0)
    def _():
        for g in range(IN_SLOTS - 1):
            @pl.when(g < n_used)
            def _():
                in_copy(g).start()

    @pl.when(count > 0)
    def _():
        wgu_s[:, :D_EXPERT] = wg_ref[0, 0].astype(BF16)
        wgu_s[:, D_EXPERT:] = wu_ref[0, 0].astype(BF16)
        wd_s[...] = wd_ref[0, 0].astype(BF16)

        def block(g, carry):
            in_copy(g).wait()

            @pl.when(g + IN_SLOTS - 1 < n_used)
            def _():
                in_copy(g + IN_SLOTS - 1).start()

            @pl.when(g >= OUT_SLOTS)
            def _():
                out_copy(g - OUT_SLOTS).wait()

            lo, hi = _unpack_pairs_f32(xbuf[lax.rem(g, IN_SLOTS)])
            half = lo.shape[1]
            gu = (jnp.dot(lo.astype(BF16), wgu_s[:half, :], preferred_element_type=F32)
                  + jnp.dot(hi.astype(BF16), wgu_s[half:, :], preferred_element_type=F32))
            gate = gu[:, :D_EXPERT]
            hb = (gate * jax.nn.sigmoid(gate)) * gu[:, D_EXPERT:]
            y = jnp.dot(hb.astype(BF16), wd_s[...], preferred_element_type=F32)
            ybuf[lax.rem(g, OUT_SLOTS)] = _pack_bf16_pairs(y.astype(BF16))
            out_copy(g).start()
            return carry

        lax.fori_loop(first, first + count, block, 0)

    @pl.when(e == pl.num_programs(0) - 1)
    def _():
        for back in range(OUT_SLOTS, 0, -1):
            @pl.when(n_used >= back)
            def _():
                out_copy(n_used - back).wait()


def _final_body(h_ref, g1_ref, g2_ref, rprev_ref, gfin_ref, o_ref):
    h = _moe_combine(h_ref[...], rprev_ref[...], g1_ref[...], g2_ref[...])
    o_ref[...] = _rms(h, gfin_ref[...])


def _const_spec(shape):
    return pl.BlockSpec(shape, lambda i: (0,) * len(shape), pipeline_mode=pl.Buffered(1))


def _row_spec(ts, width):
    return pl.BlockSpec((ts, width), lambda i: (i, 0))


def _layer_out(t, d, ts):
    shapes = [jax.ShapeDtypeStruct((t, d), F32),
              jax.ShapeDtypeStruct((t, d // 2), jnp.int32),
              jax.ShapeDtypeStruct((t, LANES), F32),
              jax.ShapeDtypeStruct((SUBLANES, t), F32),
              jax.ShapeDtypeStruct((N_EXPERTS, LANES), F32)]
    specs = [_row_spec(ts, d), _row_spec(ts, d // 2), _row_spec(ts, LANES),
             pl.BlockSpec((SUBLANES, ts), lambda i: (0, i)),
             pl.BlockSpec((N_EXPERTS, LANES), lambda i: (0, 0))]
    return shapes, specs


def _router_specs(d, ts):
    return [_const_spec((1, d)), _const_spec((ROUTER_ROWS, d)), _const_spec((ROUTER_ROWS, d)),
            _const_spec((ROUTER_ROWS, ts)), _const_spec((ts, ts))]


def _router_scratch(ts, d):
    return [pltpu.VMEM((ts, d), BF16), pltpu.VMEM((ts, d), BF16),
            pltpu.VMEM((N_EXPERTS, ts), F32)]


def _layer_params():
    return pltpu.CompilerParams(dimension_semantics=("arbitrary",), vmem_limit_bytes=VMEM_LIMIT)


def _conv_layer(x2, seq, gmix, wpw1, bpw1, wdw, bdw, lng, lnb, wpw2, bpw2, router):
    t, d = x2.shape
    ts = TS_CONV
    shapes, out_specs = _layer_out(t, d, ts)
    body = functools.partial(_conv_layer_body, seq // ts)
    return pl.pallas_call(
        body,
        grid=(t // ts,),
        in_specs=[_row_spec(ts, d), _const_spec((1, d)), _const_spec((d, 2 * d)),
                  _const_spec((1, 2 * d)), _const_spec((CONV_WIDTH, SUBLANES, d)),
                  _const_spec((SUBLANES, d)),
                  _const_spec((1, d)), _const_spec((1, d)), _const_spec((d, d)),
                  _const_spec((1, d))] + _router_specs(d, ts),
        out_specs=out_specs,
        out_shape=shapes,
        scratch_shapes=[pltpu.VMEM((ts, d), BF16),
                        pltpu.VMEM((ts + HIST, d), F32),
                        pltpu.VMEM((SUBLANES - 1, ts + HIST - SUBLANES, d), F32),
                        pltpu.VMEM((ts, d), F32),
                        pltpu.VMEM((ts, d), BF16)] + _router_scratch(ts, d),
        compiler_params=_layer_params(),
        name="conv_layer",
    )(x2, gmix, wpw1, bpw1, wdw, bdw, lng, lnb, wpw2, bpw2, *router)


def _gmlp_layer(h, g1, g2, rprev, gmix, win, bin_, vg, ws, bst, wout, bout, router):
    t, d = h.shape
    ts = TS_GMLP
    shapes, out_specs = _layer_out(t, d, ts)
    return pl.pallas_call(
        _gmlp_layer_body,
        grid=(t // ts,),
        in_specs=[_row_spec(ts, d), _row_spec(ts, d // 2), _row_spec(ts, d // 2),
                  _row_spec(ts, LANES),
                  _const_spec((1, d)), _const_spec((d, 2 * GMLP_INNER)),
                  _const_spec((1, 2 * GMLP_INNER)), _const_spec((1, GMLP_INNER)),
                  _const_spec((GMLP_HEADS, GMLP_BLOCK, GMLP_BLOCK)),
                  _const_spec((GMLP_BLOCK, GMLP_HEADS)), _const_spec((GMLP_INNER, d)),
                  _const_spec((1, d))] + _router_specs(d, ts),
        out_specs=out_specs,
        out_shape=shapes,
        scratch_shapes=[pltpu.VMEM((ts, d), BF16),
                        pltpu.VMEM((ts, GMLP_INNER), F32),
                        pltpu.VMEM((ts, GMLP_INNER), F32),
                        pltpu.VMEM((ts, GMLP_INNER), BF16)] + _router_scratch(ts, d),
        compiler_params=_layer_params(),
        name="gmlp_layer",
    )(h, g1, g2, rprev, gmix, win, bin_, vg, ws, bst, wout, bout, *router)


def _experts(rows, first_blk, blk_count, n_used, wg, wu, wd, layer):
    r, w = rows.shape
    d = wg.shape[2]

    def w_map(e, first, count, nu):
        return (layer, e, 0, 0)

    any_space = pl.BlockSpec(memory_space=pl.ANY)
    grid_spec = pltpu.PrefetchScalarGridSpec(
        num_scalar_prefetch=3,
        grid=(N_EXPERTS,),
        in_specs=[any_space,
                  pl.BlockSpec((1, 1, d, D_EXPERT), w_map),
                  pl.BlockSpec((1, 1, d, D_EXPERT), w_map),
                  pl.BlockSpec((1, 1, D_EXPERT, d), w_map)],
        out_specs=any_space,
        scratch_shapes=[pltpu.VMEM((IN_SLOTS, BM, w), rows.dtype),
                        pltpu.VMEM((OUT_SLOTS, BM, w), rows.dtype),
                        pltpu.VMEM((d, 2 * D_EXPERT), BF16),
                        pltpu.VMEM((D_EXPERT, d), BF16),
                        pltpu.SemaphoreType.DMA((IN_SLOTS,)),
                        pltpu.SemaphoreType.DMA((OUT_SLOTS,))],
    )
    return pl.pallas_call(
        _expert_body,
        grid_spec=grid_spec,
        out_shape=jax.ShapeDtypeStruct((r, w), rows.dtype),
        compiler_params=_layer_params(),
        name="experts",
    )(first_blk, blk_count, n_used, rows, wg, wu, wd)


def _final(h, g1, g2, rprev, gfin):
    t, d = h.shape
    ts = TS_FINAL
    return pl.pallas_call(
        _final_body,
        grid=(t // ts,),
        in_specs=[_row_spec(ts, d), _row_spec(ts, d // 2), _row_spec(ts, d // 2),
                  _row_spec(ts, LANES),
                  _const_spec((1, d))],
        out_specs=_row_spec(ts, d),
        out_shape=jax.ShapeDtypeStruct((t, d), F32),
        compiler_params=_layer_params(),
        name="final_norm",
    )(h, g1, g2, rprev, gfin)


def _plan_body(cnt_ref, rt_ref, d1_ref, d2_ref, first_ref, count_ref, nu_ref):
    e1 = rt_ref[R_E1:R_E1 + 1, :]
    e2 = rt_ref[R_E2:R_E2 + 1, :]
    d1 = rt_ref[R_RANK1:R_RANK1 + 1, :]
    d2 = rt_ref[R_RANK2:R_RANK2 + 1, :]
    pb = jnp.int32(0)
    for e in range(N_EXPERTS):
        nb = lax.shift_right_logical(cnt_ref[e] + (BM - 1), BM.bit_length() - 1)
        ps = (pb * BM).astype(F32)
        d1 = d1 + jnp.where(e1 == e, ps, 0.0)
        d2 = d2 + jnp.where(e2 == e, ps, 0.0)
        first_ref[e] = pb
        count_ref[e] = nb
        pb = pb + nb
    nu_ref[0] = pb
    d1_ref[...] = d1.astype(jnp.int32)
    d2_ref[...] = d2.astype(jnp.int32)


def _plan(route_t, counts):
    t = route_t.shape[1]
    smem = pl.BlockSpec(memory_space=pltpu.SMEM)
    vmem = pl.BlockSpec(memory_space=pltpu.VMEM)
    return pl.pallas_call(
        _plan_body,
        in_specs=[smem, vmem],
        out_specs=[vmem, vmem, smem, smem, smem],
        out_shape=[jax.ShapeDtypeStruct((1, t), jnp.int32),
                   jax.ShapeDtypeStruct((1, t), jnp.int32),
                   jax.ShapeDtypeStruct((N_EXPERTS,), jnp.int32),
                   jax.ShapeDtypeStruct((N_EXPERTS,), jnp.int32),
                   jax.ShapeDtypeStruct((1,), jnp.int32)],
        name="moe_plan",
    )(counts, route_t)


def _sc_workers():
    info = plsc.get_sparse_core_info()
    return info.num_cores, info.num_cores * info.num_subcores


def _sc_mesh():
    return plsc.VectorSubcoreMesh(core_axis_name="c", subcore_axis_name="s")


def _sc_worker_id(num_cores):
    return lax.axis_index("s") * num_cores + lax.axis_index("c")


def _dispatch(hn, dest1, dest2, n_rows):
    t, w = hn.shape
    num_cores, n_workers = _sc_workers()
    per_w = t // n_workers
    chunk = SC_DISPATCH_CHUNK

    def body(hn_hbm, d1_hbm, d2_hbm, rows_hbm, buf, i1, i2):
        base_w = _sc_worker_id(num_cores) * per_w

        @pl.loop(0, per_w // chunk)
        def _(j):
            base = pl.multiple_of(base_w + j * chunk, chunk)
            pltpu.sync_copy(hn_hbm.at[pl.ds(base, chunk)], buf)
            pltpu.sync_copy(d1_hbm.at[:, pl.ds(base, chunk)], i1)
            pltpu.sync_copy(d2_hbm.at[:, pl.ds(base, chunk)], i2)
            pltpu.sync_copy(buf, rows_hbm.at[i1.at[0]])
            pltpu.sync_copy(buf, rows_hbm.at[i2.at[0]])

    return pl.kernel(
        body,
        out_type=jax.ShapeDtypeStruct((n_rows, w), hn.dtype),
        mesh=_sc_mesh(),
        scratch_types=[pltpu.VMEM((chunk, w), hn.dtype),
                       pltpu.VMEM((1, chunk), jnp.int32),
                       pltpu.VMEM((1, chunk), jnp.int32)],
        name="moe_dispatch",
    )(hn, dest1, dest2)


def _combine_gather(y, dest1, dest2):
    d = y.shape[1]
    t = dest1.shape[1]
    num_cores, n_workers = _sc_workers()
    per_w = t // n_workers
    chunk = SC_COMBINE_CHUNK

    def body(y_hbm, d1_hbm, d2_hbm, g1_hbm, g2_hbm, buf, idx):
        base_w = _sc_worker_id(num_cores) * per_w

        @pl.loop(0, per_w // chunk)
        def _(j):
            base = pl.multiple_of(base_w + j * chunk, chunk)
            for d_hbm, g_hbm in ((d1_hbm, g1_hbm), (d2_hbm, g2_hbm)):
                pltpu.sync_copy(d_hbm.at[:, pl.ds(base, chunk)], idx)
                pltpu.sync_copy(y_hbm.at[idx.at[0]], buf)
                pltpu.sync_copy(buf, g_hbm.at[pl.ds(base, chunk)])

    out = jax.ShapeDtypeStruct((t, d), y.dtype)
    return pl.kernel(
        body,
        out_type=(out, out),
        mesh=_sc_mesh(),
        scratch_types=[pltpu.VMEM((chunk, d), y.dtype),
                       pltpu.VMEM((1, chunk), jnp.int32)],
        name="moe_combine_gather",
    )(y, dest1, dest2)


def _moe(hn, route_t, cnt, wg, wu, wd, layer):
    t = hn.shape[0]
    n_blk = (2 * t) // BM + N_EXPERTS
    counts = cnt[:, 0].astype(jnp.int32)
    dest1, dest2, first_blk, blk_count, n_used = _plan(route_t, counts)
    rows = _dispatch(hn, dest1, dest2, n_blk * BM)
    y = _experts(rows, first_blk, blk_count, n_used, wg, wu, wd, layer)
    return _combine_gather(y, dest1, dest2)


def _router_inputs(gffn, w_group, b_group, w_expert, b_expert, ts):
    d = w_group.shape[0]
    wr = jnp.zeros((ROUTER_ROWS, d), F32)
    wr = wr.at[:N_EXPERTS].set(w_expert.T).at[GROUP_ROW0:GROUP_ROW0 + N_GROUPS].set(w_group.T)
    br = jnp.zeros((ROUTER_ROWS,), F32)
    br = br.at[:N_EXPERTS].set(b_expert).at[GROUP_ROW0:GROUP_ROW0 + N_GROUPS].set(b_group)
    w_hi = wr.astype(BF16)
    w_lo = (wr - w_hi.astype(F32)).astype(BF16)
    idx = jnp.arange(ts)
    triu = (idx[:, None] < idx[None, :]).astype(BF16)
    return (gffn.reshape(1, -1), w_hi, w_lo, jnp.broadcast_to(br[:, None], (ROUTER_ROWS, ts)), triu)


def kernel(x, norm_mix_g, norm_ffn_g, cv_w_pw1, cv_b_pw1, cv_w_dw, cv_b_dw, cv_ln_g, cv_ln_b, cv_w_pw2, cv_b_pw2, gm_w_in, gm_b_in, gm_v_norm_g, gm_w_s, gm_b_s, gm_w_out, gm_b_out, moe_w_group, moe_b_group, moe_w_expert, moe_b_expert, moe_w_gate, moe_w_up, moe_w_down, final_g):
    bsz, seq, d = x.shape
    t = bsz * seq
    x2 = x.reshape(t, d)
    row = lambda a: a.reshape(1, -1)

    router0 = _router_inputs(norm_ffn_g[0], moe_w_group[0], moe_b_group[0], moe_w_expert[0],
                             moe_b_expert[0], TS_CONV)
    h1, hn1, route0, route_t0, cnt0 = _conv_layer(
        x2, seq, row(norm_mix_g[0]), cv_w_pw1[0].astype(BF16), row(cv_b_pw1[0]),
        jnp.broadcast_to(cv_w_dw[0][:, None, :], (CONV_WIDTH, SUBLANES, d)),
        jnp.broadcast_to(cv_b_dw[0][None, :], (SUBLANES, d)),
        row(cv_ln_g[0]), row(cv_ln_b[0]), cv_w_pw2[0].astype(BF16),
        row(cv_b_pw2[0]), router0)
    ga0, gb0 = _moe(hn1, route_t0, cnt0, moe_w_gate, moe_w_up, moe_w_down, 0)

    idx = jnp.arange(GMLP_BLOCK)
    mask = (idx[None, :] // GMLP_CHUNK) <= (idx[:, None] // GMLP_CHUNK)
    ws = jnp.where(mask[None], gm_w_s[0], 0.0).astype(BF16)
    router1 = _router_inputs(norm_ffn_g[1], moe_w_group[1], moe_b_group[1], moe_w_expert[1],
                             moe_b_expert[1], TS_GMLP)
    h2, hn2, route1, route_t1, cnt1 = _gmlp_layer(
        h1, ga0, gb0, route0, row(norm_mix_g[1]), gm_w_in[0].astype(BF16), row(gm_b_in[0]),
        row(gm_v_norm_g[0]), ws, jnp.transpose(gm_b_s[0]), gm_w_out[0].astype(BF16),
        row(gm_b_out[0]), router1)
    ga1, gb1 = _moe(hn2, route_t1, cnt1, moe_w_gate, moe_w_up, moe_w_down, 1)

    out = _final(h2, ga1, gb1, route1, row(final_g))
    return out.reshape(bsz, seq, d)
```

```python
import functools

import jax
import jax.numpy as jnp
from jax import lax
from jax.experimental import pallas as pl
from jax.experimental.pallas import tpu as pltpu
from jax.experimental.pallas import tpu_sc as plsc

D_MODEL = 1024
CONV_WIDTH = 31
GMLP_BLOCK = 128
GMLP_CHUNK = 64
GMLP_INNER = 2 * D_MODEL
GMLP_HEADS = 8
GMLP_HEAD_DIM = GMLP_INNER // GMLP_HEADS
N_GROUPS = 4
EXPERTS_PER_GROUP = 8
N_EXPERTS = N_GROUPS * EXPERTS_PER_GROUP
D_EXPERT = D_MODEL // 2
EPS = 1e-6

LANES = 128
SUBLANES = 8
HIST = 32
TS_CONV = 512
TS_GMLP = 512
TS_FINAL = 1024
BM = 512
IN_SLOTS = 4
OUT_SLOTS = 3
RC = 32
CONV_CW = 256
GLU_CW = 256
GMLP_CW = 512
GROUP_ROW0 = N_EXPERTS
ROUTER_ROWS = 48
VMEM_LIMIT = 56 * 1024 * 1024
SC_DISPATCH_CHUNK = 128
SC_COMBINE_CHUNK = 128

R_E1, R_E2, R_RANK1, R_RANK2, R_GATE1, R_GATE2 = range(6)

F32 = jnp.float32
BF16 = jnp.bfloat16


def _gelu_tanh(x):
    c = 0.7978845608028654
    t = jnp.tanh(x * (c + (c * 0.044715) * (x * x)))
    hx = 0.5 * x
    return hx + hx * t


def _rms(xf, g):
    return xf * lax.rsqrt(jnp.mean(xf * xf, axis=-1, keepdims=True) + EPS) * g


def _pack_bf16_pairs(xb):
    w = xb.shape[1] // 2
    bits = lax.bitcast_convert_type(xb.astype(F32), jnp.int32)
    return lax.shift_right_logical(bits[:, :w], 16) | bits[:, w:]


def _unpack_pairs_f32(p):
    lo = lax.bitcast_convert_type(lax.shift_left(p, 16), F32)
    hi = lax.bitcast_convert_type(p & jnp.int32(-65536), F32)
    return lo, hi


def _moe_combine(h, rp, p1, p2):
    w = h.shape[1] // 2
    g1 = rp[:, R_GATE1:R_GATE1 + 1]
    g2 = rp[:, R_GATE2:R_GATE2 + 1]
    lo1, hi1 = _unpack_pairs_f32(p1)
    lo2, hi2 = _unpack_pairs_f32(p2)
    return jnp.concatenate([h[:, :w] + g1 * lo1 + g2 * lo2,
                            h[:, w:] + g1 * hi1 + g2 * hi2], axis=1)


def _row_loop(n_rows, fn, unroll=True):
    def step(ci, carry):
        fn(pl.ds(pl.multiple_of(ci * RC, RC), RC))
        return carry

    lax.fori_loop(0, n_rows // RC, step, 0, unroll=unroll)


def _route_tail(h_ref, gffn_ref, wrh_ref, wrl_ref, br_ref, triu_ref, run_ref,
                hn_out_ref, route_ref, route_t_ref, cnt_ref, hi_s, lo_s, live=1.0):
    ts = h_ref.shape[0]

    def norm_rows(rows):
        hn2 = _rms(h_ref[rows, :], gffn_ref[...])
        hi = hn2.astype(BF16)
        hf = hi.astype(F32)
        hi_s[rows, :] = hi
        lo_s[rows, :] = (hn2 - hf).astype(BF16)
        hn_out_ref[rows, :] = _pack_bf16_pairs(hi)

    _row_loop(ts, norm_rows)

    nt = (((1,), (1,)), ((), ()))
    hi = hi_s[...]
    lt = (lax.dot_general(wrh_ref[...], hi, nt, preferred_element_type=F32)
          + lax.dot_general(wrl_ref[...], hi, nt, preferred_element_type=F32)
          + lax.dot_general(wrh_ref[...], lo_s[...], nt, preferred_element_type=F32)
          + br_ref[...])
    sub = lax.broadcasted_iota(jnp.int32, (SUBLANES, ts), 0).astype(F32)
    ninf = jnp.float32(-jnp.inf)
    big = jnp.float32(1e9)
    first_idx = lambda hit: jnp.min(jnp.where(hit, sub, big), axis=0, keepdims=True)

    g_ok = sub < N_GROUPS
    lg = jnp.where(g_ok, lt[GROUP_ROW0:GROUP_ROW0 + SUBLANES, :], ninf)
    gmax = jnp.max(lg, axis=0, keepdims=True)
    gsel = first_idx(lg == gmax)
    p_g = 1.0 / jnp.sum(jnp.where(g_ok, jnp.exp(lg - gmax), 0.0), axis=0, keepdims=True)
    le = lt[0:EXPERTS_PER_GROUP, :]
    for g in range(1, N_GROUPS):
        le = jnp.where(gsel == g, lt[g * EXPERTS_PER_GROUP:(g + 1) * EXPERTS_PER_GROUP, :], le)
    v1 = jnp.max(le, axis=0, keepdims=True)
    i1 = first_idx(le == v1)
    le2 = jnp.where(sub == i1, ninf, le)
    v2 = jnp.max(le2, axis=0, keepdims=True)
    i2 = first_idx(le2 == v2)
    e = jnp.exp(v2 - v1)
    den = 1.0 + e
    gate1 = p_g * (1.0 / den)
    gate2 = p_g * (e / den)
    e1 = gsel * EXPERTS_PER_GROUP + i1
    e2 = gsel * EXPERTS_PER_GROUP + i2

    eid = lax.broadcasted_iota(jnp.int32, (N_EXPERTS, ts), 0).astype(F32)
    oh1 = eid == e1
    oh2 = eid == e2
    oh = jnp.where(oh1 | oh2, 1.0, 0.0)
    run = run_ref[...]
    c = jnp.dot(oh.astype(BF16), triu_ref[...], preferred_element_type=F32) + run
    rank1 = jnp.sum(jnp.where(oh1, c, 0.0), axis=0, keepdims=True)
    rank2 = jnp.sum(jnp.where(oh2, c, 0.0), axis=0, keepdims=True)
    run = run + jnp.broadcast_to(jnp.sum(oh, axis=1, keepdims=True) * live, run.shape)
    run_ref[...] = run
    cnt_ref[...] = run[:, :LANES]

    rec_t = jnp.concatenate([e1, e2, rank1, rank2, gate1, gate2,
                             jnp.zeros((SUBLANES - 6, ts), F32)], axis=0)
    route_t_ref[...] = rec_t
    route_ref[...] = jnp.concatenate([rec_t, jnp.zeros((LANES - SUBLANES, ts), F32)], axis=0).T


def _conv_layer_body(tiles_per_seq,
                     x_ref, gmix_ref, wpw1_ref, bpw1_ref, wdw_ref, bdw_ref, lng_ref, lnb_ref,
                     wpw2_ref, bpw2_ref, gffn_ref, wrh_ref, wrl_ref, br_ref, triu_ref,
                     h_out_ref, hn_out_ref, route_ref, route_t_ref, cnt_ref,
                     hn_s, zext_ref, zs_ref, y_s, a_s, hi_s, lo_s, run_ref):
    i = pl.program_id(0)
    ts, d = x_ref.shape

    @pl.when(i == 0)
    def _():
        run_ref[...] = jnp.zeros_like(run_ref)

    @pl.when(i % tiles_per_seq == 0)
    def _():
        zext_ref[0:HIST, :] = jnp.zeros((HIST, d), F32)

    def norm_rows(rows):
        hn_s[rows, :] = _rms(x_ref[rows, :], gmix_ref[...]).astype(BF16)

    _row_loop(ts, norm_rows)

    hn = hn_s[...]
    for c0 in range(0, d, GLU_CW):
        ca = slice(c0, c0 + GLU_CW)
        cg = slice(d + c0, d + c0 + GLU_CW)
        pa = jnp.dot(hn, wpw1_ref[:, ca], preferred_element_type=F32) + bpw1_ref[:, ca]
        pg = jnp.dot(hn, wpw1_ref[:, cg], preferred_element_type=F32) + bpw1_ref[:, cg]
        zext_ref[HIST:HIST + ts, ca] = pa * jax.nn.sigmoid(pg)

    span = ts + HIST - SUBLANES
    sub_id = lax.broadcasted_iota(jnp.int32, (SUBLANES, LANES), 0)
    shifts = range(1, SUBLANES)
    for c0 in range(0, d, LANES):
        cols = slice(c0, c0 + LANES)

        def rotate(g):
            up = {0: g}
            for r in (4, 2, 6, 1, 3, 5, 7):
                step = r & -r
                up[r] = pltpu.roll(up[r - step], SUBLANES - step, axis=0)
            return [up[r] for r in shifts]

        cur = rotate(zext_ref[0:SUBLANES, cols])
        for m0 in range(0, span, SUBLANES):
            nxt = rotate(zext_ref[m0 + SUBLANES:m0 + 2 * SUBLANES, cols])
            for r in shifts:
                zs_ref[r - 1, m0:m0 + SUBLANES, cols] = jnp.where(
                    sub_id < SUBLANES - r, cur[r - 1], nxt[r - 1])
            cur = nxt

    first = HIST - (CONV_WIDTH - 1)

    def conv_rows(rows):
        r0 = rows.start
        groups = RC // SUBLANES
        for c0 in range(0, d, CONV_CW):
            cols = slice(c0, c0 + CONV_CW)
            accs = [bdw_ref[:, cols]] * groups
            for k in range(CONV_WIDTH):
                q, r = divmod(first + k, SUBLANES)
                w8 = wdw_ref[k, :, cols]
                for g in range(groups):
                    src = pl.ds(r0 + (q + g) * SUBLANES, SUBLANES)
                    slab = zext_ref[src, cols] if r == 0 else zs_ref[r - 1, src, cols]
                    accs[g] = accs[g] + w8 * slab
            for g in range(groups):
                y_s[pl.ds(r0 + g * SUBLANES, SUBLANES), cols] = accs[g]
        y = y_s[rows, :]
        mu = jnp.mean(y, axis=-1, keepdims=True)
        yc = y - mu
        yn = yc * lax.rsqrt(jnp.mean(yc * yc, axis=-1, keepdims=True) + EPS)
        yn = yn * lng_ref[...] + lnb_ref[...]
        a_s[rows, :] = (yn * jax.nn.sigmoid(yn)).astype(BF16)

    _row_loop(ts, conv_rows, unroll=True)
    zext_ref[0:HIST, :] = zext_ref[ts:ts + HIST, :]

    for r0 in range(0, ts, ts // 2):
        rows = slice(r0, r0 + ts // 2)
        m = jnp.dot(a_s[rows, :], wpw2_ref[...], preferred_element_type=F32) + bpw2_ref[...]
        h_out_ref[rows, :] = x_ref[rows, :] + m
    _route_tail(h_out_ref, gffn_ref, wrh_ref, wrl_ref, br_ref, triu_ref, run_ref,
                hn_out_ref, route_ref, route_t_ref, cnt_ref, hi_s, lo_s)


def _gmlp_layer_body(h_ref, g1_ref, g2_ref, rprev_ref, gmix_ref, win_ref, bin_ref, vg_ref,
                     ws_ref, bst_ref, wout_ref, bout_ref, gffn_ref, wrh_ref, wrl_ref, br_ref,
                     triu_ref,
                     h_out_ref, hn_out_ref, route_ref, route_t_ref, cnt_ref,
                     hn_s, u_s, v_s, gated_s, hprev_s, hi_s, lo_s, run_ref):
    i = pl.program_id(0)
    ts = h_ref.shape[0]

    @pl.when(i == 0)
    def _():
        run_ref[...] = jnp.zeros_like(run_ref)
        hprev_s[...] = jnp.zeros_like(hprev_s)

    _route_tail(hprev_s, gffn_ref, wrh_ref, wrl_ref, br_ref, triu_ref, run_ref,
                hn_out_ref, route_ref, route_t_ref, cnt_ref, hi_s, lo_s,
                live=jnp.where(i > 0, 1.0, 0.0))

    def norm_rows(rows):
        h = _moe_combine(h_ref[rows, :], rprev_ref[rows, :], g1_ref[rows, :], g2_ref[rows, :])
        h_out_ref[rows, :] = h
        hn_s[rows, :] = _rms(h, gmix_ref[...]).astype(BF16)

    _row_loop(ts, norm_rows)

    hn = hn_s[...]
    ssq = jnp.zeros((ts, 1), F32)
    for c0 in range(0, 2 * GMLP_INNER, GMLP_CW):
        cols = slice(c0, c0 + GMLP_CW)
        p = jnp.dot(hn, win_ref[:, cols], preferred_element_type=F32)
        zc = _gelu_tanh(p + bin_ref[:, cols])
        if c0 < GMLP_INNER:
            u_s[:, cols] = zc
        else:
            v_s[:, c0 - GMLP_INNER:c0 - GMLP_INNER + GMLP_CW] = zc
            ssq = ssq + jnp.sum(zc * zc, axis=-1, keepdims=True)
    rs = lax.rsqrt(ssq * (1.0 / GMLP_INNER) + EPS)

    for b0 in range(0, ts, GMLP_BLOCK):
        rows = slice(b0, b0 + GMLP_BLOCK)
        for hd in range(GMLP_HEADS):
            cols = slice(hd * GMLP_HEAD_DIM, (hd + 1) * GMLP_HEAD_DIM)
            vv = (v_s[rows, cols] * rs[rows] * vg_ref[:, cols]).astype(BF16)
            sv = jnp.dot(ws_ref[hd], vv, preferred_element_type=F32) + bst_ref[:, hd:hd + 1]
            gated_s[rows, cols] = (u_s[rows, cols] * sv).astype(BF16)

    out = jnp.dot(gated_s[...], wout_ref[...], preferred_element_type=F32) + bout_ref[...]
    hnew = h_out_ref[...] + out
    h_out_ref[...] = hnew
    hprev_s[...] = hnew


def _expert_body(first_ref, count_ref, nused_ref, rows_hbm, wg_ref, wu_ref, wd_ref, y_hbm,
                 xbuf, ybuf, wgu_s, wd_s, in_sem, out_sem):
    e = pl.program_id(0)
    first = first_ref[e]
    count = count_ref[e]
    n_used = nused_ref[0]

    def rows_of(g):
        return pl.ds(pl.multiple_of(g * BM, BM), BM)

    def in_copy(g):
        slot = lax.rem(g, IN_SLOTS)
        return pltpu.make_async_copy(rows_hbm.at[rows_of(g)], xbuf.at[slot], in_sem.at[slot])

    def out_copy(g):
        slot = lax.rem(g, OUT_SLOTS)
        return pltpu.make_async_copy(ybuf.at[slot], y_hbm.at[rows_of(g)], out_sem.at[slot])

    @pl.when(e == 0)
    def _():
        for g in range(IN_SLOTS - 1):
            @pl.when(g < n_used)
            def _():
                in_copy(g).start()

    @pl.when(count > 0)
    def _():
        wgu_s[:, :D_EXPERT] = wg_ref[0, 0].astype(BF16)
        wgu_s[:, D_EXPERT:] = wu_ref[0, 0].astype(BF16)
        wd_s[...] = wd_ref[0, 0].astype(BF16)

        def block(g, carry):
            in_copy(g).wait()

            @pl.when(g + IN_SLOTS - 1 < n_used)
            def _():
                in_copy(g + IN_SLOTS - 1).start()

            @pl.when(g >= OUT_SLOTS)
            def _():
                out_copy(g - OUT_SLOTS).wait()

            lo, hi = _unpack_pairs_f32(xbuf[lax.rem(g, IN_SLOTS)])
            half = lo.shape[1]
            gu = (jnp.dot(lo.astype(BF16), wgu_s[:half, :], preferred_element_type=F32)
                  + jnp.dot(hi.astype(BF16), wgu_s[half:, :], preferred_element_type=F32))
            gate = gu[:, :D_EXPERT]
            hb = (gate * jax.nn.sigmoid(gate)) * gu[:, D_EXPERT:]
            y = jnp.dot(hb.astype(BF16), wd_s[...], preferred_element_type=F32)
            ybuf[lax.rem(g, OUT_SLOTS)] = _pack_bf16_pairs(y.astype(BF16))
            out_copy(g).start()
            return carry

        lax.fori_loop(first, first + count, block, 0)

    @pl.when(e == pl.num_programs(0) - 1)
    def _():
        for back in range(OUT_SLOTS, 0, -1):
            @pl.when(n_used >= back)
            def _():
                out_copy(n_used - back).wait()


def _final_body(h_ref, g1_ref, g2_ref, rprev_ref, gfin_ref, o_ref):
    h = _moe_combine(h_ref[...], rprev_ref[...], g1_ref[...], g2_ref[...])
    o_ref[...] = _rms(h, gfin_ref[...])


def _const_spec(shape):
    return pl.BlockSpec(shape, lambda i: (0,) * len(shape), pipeline_mode=pl.Buffered(1))


def _row_spec(ts, width):
    return pl.BlockSpec((ts, width), lambda i: (i, 0))


def _layer_out(t, d, ts, lag=0):
    last = t // ts - 1
    mixed = lambda i: jnp.minimum(i, last)
    routed = lambda i: jnp.maximum(i - lag, 0)
    shapes = [jax.ShapeDtypeStruct((t, d), F32),
              jax.ShapeDtypeStruct((t, d // 2), jnp.int32),
              jax.ShapeDtypeStruct((t, LANES), F32),
              jax.ShapeDtypeStruct((SUBLANES, t), F32),
              jax.ShapeDtypeStruct((N_EXPERTS, LANES), F32)]
    specs = [pl.BlockSpec((ts, d), lambda i: (mixed(i), 0)),
             pl.BlockSpec((ts, d // 2), lambda i: (routed(i), 0)),
             pl.BlockSpec((ts, LANES), lambda i: (routed(i), 0)),
             pl.BlockSpec((SUBLANES, ts), lambda i: (0, routed(i))),
             pl.BlockSpec((N_EXPERTS, LANES), lambda i: (0, 0))]
    return shapes, specs


def _router_specs(d, ts):
    return [_const_spec((1, d)), _const_spec((ROUTER_ROWS, d)), _const_spec((ROUTER_ROWS, d)),
            _const_spec((ROUTER_ROWS, ts)), _const_spec((ts, ts))]


def _router_scratch(ts, d):
    return [pltpu.VMEM((ts, d), BF16), pltpu.VMEM((ts, d), BF16),
            pltpu.VMEM((N_EXPERTS, ts), F32)]


def _layer_params():
    return pltpu.CompilerParams(dimension_semantics=("arbitrary",), vmem_limit_bytes=VMEM_LIMIT)


def _conv_layer(x2, seq, gmix, wpw1, bpw1, wdw, bdw, lng, lnb, wpw2, bpw2, router):
    t, d = x2.shape
    ts = TS_CONV
    shapes, out_specs = _layer_out(t, d, ts)
    body = functools.partial(_conv_layer_body, seq // ts)
    return pl.pallas_call(
        body,
        grid=(t // ts,),
        in_specs=[_row_spec(ts, d), _const_spec((1, d)), _const_spec((d, 2 * d)),
                  _const_spec((1, 2 * d)), _const_spec((CONV_WIDTH, SUBLANES, d)),
                  _const_spec((SUBLANES, d)),
                  _const_spec((1, d)), _const_spec((1, d)), _const_spec((d, d)),
                  _const_spec((1, d))] + _router_specs(d, ts),
        out_specs=out_specs,
        out_shape=shapes,
        scratch_shapes=[pltpu.VMEM((ts, d), BF16),
                        pltpu.VMEM((ts + HIST, d), F32),
                        pltpu.VMEM((SUBLANES - 1, ts + HIST - SUBLANES, d), F32),
                        pltpu.VMEM((ts, d), F32),
                        pltpu.VMEM((ts, d), BF16)] + _router_scratch(ts, d),
        compiler_params=_layer_params(),
        name="conv_layer",
    )(x2, gmix, wpw1, bpw1, wdw, bdw, lng, lnb, wpw2, bpw2, *router)


def _gmlp_layer(h, g1, g2, rprev, gmix, win, bin_, vg, ws, bst, wout, bout, router):
    t, d = h.shape
    ts = TS_GMLP
    n_tiles = t // ts
    shapes, out_specs = _layer_out(t, d, ts, lag=1)
    tile_spec = lambda width: pl.BlockSpec((ts, width), lambda i: (jnp.minimum(i, n_tiles - 1), 0))
    return pl.pallas_call(
        _gmlp_layer_body,
        grid=(n_tiles + 1,),
        in_specs=[tile_spec(d), tile_spec(d // 2), tile_spec(d // 2), tile_spec(LANES),
                  _const_spec((1, d)), _const_spec((d, 2 * GMLP_INNER)),
                  _const_spec((1, 2 * GMLP_INNER)), _const_spec((1, GMLP_INNER)),
                  _const_spec((GMLP_HEADS, GMLP_BLOCK, GMLP_BLOCK)),
                  _const_spec((GMLP_BLOCK, GMLP_HEADS)), _const_spec((GMLP_INNER, d)),
                  _const_spec((1, d))] + _router_specs(d, ts),
        out_specs=out_specs,
        out_shape=shapes,
        scratch_shapes=[pltpu.VMEM((ts, d), BF16),
                        pltpu.VMEM((ts, GMLP_INNER), F32),
                        pltpu.VMEM((ts, GMLP_INNER), F32),
                        pltpu.VMEM((ts, GMLP_INNER), BF16),
                        pltpu.VMEM((ts, d), F32)] + _router_scratch(ts, d),
        compiler_params=_layer_params(),
        name="gmlp_layer",
    )(h, g1, g2, rprev, gmix, win, bin_, vg, ws, bst, wout, bout, *router)


def _experts(rows, first_blk, blk_count, n_used, wg, wu, wd, layer):
    r, w = rows.shape
    d = wg.shape[2]

    def w_map(e, first, count, nu):
        return (layer, e, 0, 0)

    any_space = pl.BlockSpec(memory_space=pl.ANY)
    grid_spec = pltpu.PrefetchScalarGridSpec(
        num_scalar_prefetch=3,
        grid=(N_EXPERTS,),
        in_specs=[any_space,
                  pl.BlockSpec((1, 1, d, D_EXPERT), w_map),
                  pl.BlockSpec((1, 1, d, D_EXPERT), w_map),
                  pl.BlockSpec((1, 1, D_EXPERT, d), w_map)],
        out_specs=any_space,
        scratch_shapes=[pltpu.VMEM((IN_SLOTS, BM, w), rows.dtype),
                        pltpu.VMEM((OUT_SLOTS, BM, w), rows.dtype),
                        pltpu.VMEM((d, 2 * D_EXPERT), BF16),
                        pltpu.VMEM((D_EXPERT, d), BF16),
                        pltpu.SemaphoreType.DMA((IN_SLOTS,)),
                        pltpu.SemaphoreType.DMA((OUT_SLOTS,))],
    )
    return pl.pallas_call(
        _expert_body,
        grid_spec=grid_spec,
        out_shape=jax.ShapeDtypeStruct((r, w), rows.dtype),
        compiler_params=_layer_params(),
        name="experts",
    )(first_blk, blk_count, n_used, rows, wg, wu, wd)


def _final(h, g1, g2, rprev, gfin):
    t, d = h.shape
    ts = TS_FINAL
    return pl.pallas_call(
        _final_body,
        grid=(t // ts,),
        in_specs=[_row_spec(ts, d), _row_spec(ts, d // 2), _row_spec(ts, d // 2),
                  _row_spec(ts, LANES),
                  _const_spec((1, d))],
        out_specs=_row_spec(ts, d),
        out_shape=jax.ShapeDtypeStruct((t, d), F32),
        compiler_params=_layer_params(),
        name="final_norm",
    )(h, g1, g2, rprev, gfin)


def _plan_body(cnt_ref, rt_ref, d1_ref, d2_ref, first_ref, count_ref, nu_ref):
    e1 = rt_ref[R_E1:R_E1 + 1, :]
    e2 = rt_ref[R_E2:R_E2 + 1, :]
    d1 = rt_ref[R_RANK1:R_RANK1 + 1, :]
    d2 = rt_ref[R_RANK2:R_RANK2 + 1, :]
    pb = jnp.int32(0)
    for e in range(N_EXPERTS):
        nb = lax.shift_right_logical(cnt_ref[e] + (BM - 1), BM.bit_length() - 1)
        ps = (pb * BM).astype(F32)
        d1 = d1 + jnp.where(e1 == e, ps, 0.0)
        d2 = d2 + jnp.where(e2 == e, ps, 0.0)
        first_ref[e] = pb
        count_ref[e] = nb
        pb = pb + nb
    nu_ref[0] = pb
    d1_ref[...] = d1.astype(jnp.int32)
    d2_ref[...] = d2.astype(jnp.int32)


def _plan(route_t, counts):
    t = route_t.shape[1]
    smem = pl.BlockSpec(memory_space=pltpu.SMEM)
    vmem = pl.BlockSpec(memory_space=pltpu.VMEM)
    return pl.pallas_call(
        _plan_body,
        in_specs=[smem, vmem],
        out_specs=[vmem, vmem, smem, smem, smem],
        out_shape=[jax.ShapeDtypeStruct((1, t), jnp.int32),
                   jax.ShapeDtypeStruct((1, t), jnp.int32),
                   jax.ShapeDtypeStruct((N_EXPERTS,), jnp.int32),
                   jax.ShapeDtypeStruct((N_EXPERTS,), jnp.int32),
                   jax.ShapeDtypeStruct((1,), jnp.int32)],
        name="moe_plan",
    )(counts, route_t)


def _sc_workers():
    info = plsc.get_sparse_core_info()
    return info.num_cores, info.num_cores * info.num_subcores


def _sc_mesh():
    return plsc.VectorSubcoreMesh(core_axis_name="c", subcore_axis_name="s")


def _sc_worker_id(num_cores):
    return lax.axis_index("s") * num_cores + lax.axis_index("c")


def _dispatch(hn, dest1, dest2, n_rows):
    t, w = hn.shape
    num_cores, n_workers = _sc_workers()
    per_w = t // n_workers
    chunk = SC_DISPATCH_CHUNK

    def body(hn_hbm, d1_hbm, d2_hbm, rows_hbm, buf, i1, i2):
        base_w = _sc_worker_id(num_cores) * per_w

        @pl.loop(0, per_w // chunk)
        def _(j):
            base = pl.multiple_of(base_w + j * chunk, chunk)
            pltpu.sync_copy(hn_hbm.at[pl.ds(base, chunk)], buf)
            pltpu.sync_copy(d1_hbm.at[:, pl.ds(base, chunk)], i1)
            pltpu.sync_copy(d2_hbm.at[:, pl.ds(base, chunk)], i2)
            pltpu.sync_copy(buf, rows_hbm.at[i1.at[0]])
            pltpu.sync_copy(buf, rows_hbm.at[i2.at[0]])

    return pl.kernel(
        body,
        out_type=jax.ShapeDtypeStruct((n_rows, w), hn.dtype),
        mesh=_sc_mesh(),
        scratch_types=[pltpu.VMEM((chunk, w), hn.dtype),
                       pltpu.VMEM((1, chunk), jnp.int32),
                       pltpu.VMEM((1, chunk), jnp.int32)],
        name="moe_dispatch",
    )(hn, dest1, dest2)


def _combine_gather(y, dest1, dest2):
    d = y.shape[1]
    t = dest1.shape[1]
    num_cores, n_workers = _sc_workers()
    per_w = t // n_workers
    chunk = SC_COMBINE_CHUNK

    def body(y_hbm, d1_hbm, d2_hbm, g1_hbm, g2_hbm, buf, idx):
        base_w = _sc_worker_id(num_cores) * per_w

        @pl.loop(0, per_w // chunk)
        def _(j):
            base = pl.multiple_of(base_w + j * chunk, chunk)
            for d_hbm, g_hbm in ((d1_hbm, g1_hbm), (d2_hbm, g2_hbm)):
                pltpu.sync_copy(d_hbm.at[:, pl.ds(base, chunk)], idx)
                pltpu.sync_copy(y_hbm.at[idx.at[0]], buf)
                pltpu.sync_copy(buf, g_hbm.at[pl.ds(base, chunk)])

    out = jax.ShapeDtypeStruct((t, d), y.dtype)
    return pl.kernel(
        body,
        out_type=(out, out),
        mesh=_sc_mesh(),
        scratch_types=[pltpu.VMEM((chunk, d), y.dtype),
                       pltpu.VMEM((1, chunk), jnp.int32)],
        name="moe_combine_gather",
    )(y, dest1, dest2)


def _moe(hn, route_t, cnt, wg, wu, wd, layer):
    t = hn.shape[0]
    n_blk = (2 * t) // BM + N_EXPERTS
    counts = cnt[:, 0].astype(jnp.int32)
    dest1, dest2, first_blk, blk_count, n_used = _plan(route_t, counts)
    rows = _dispatch(hn, dest1, dest2, n_blk * BM)
    y = _experts(rows, first_blk, blk_count, n_used, wg, wu, wd, layer)
    return _combine_gather(y, dest1, dest2)


def _router_inputs(gffn, w_group, b_group, w_expert, b_expert, ts):
    d = w_group.shape[0]
    wr = jnp.zeros((ROUTER_ROWS, d), F32)
    wr = wr.at[:N_EXPERTS].set(w_expert.T).at[GROUP_ROW0:GROUP_ROW0 + N_GROUPS].set(w_group.T)
    br = jnp.zeros((ROUTER_ROWS,), F32)
    br = br.at[:N_EXPERTS].set(b_expert).at[GROUP_ROW0:GROUP_ROW0 + N_GROUPS].set(b_group)
    w_hi = wr.astype(BF16)
    w_lo = (wr - w_hi.astype(F32)).astype(BF16)
    idx = jnp.arange(ts)
    triu = (idx[:, None] < idx[None, :]).astype(BF16)
    return (gffn.reshape(1, -1), w_hi, w_lo, jnp.broadcast_to(br[:, None], (ROUTER_ROWS, ts)), triu)


def kernel(x, norm_mix_g, norm_ffn_g, cv_w_pw1, cv_b_pw1, cv_w_dw, cv_b_dw, cv_ln_g, cv_ln_b, cv_w_pw2, cv_b_pw2, gm_w_in, gm_b_in, gm_v_norm_g, gm_w_s, gm_b_s, gm_w_out, gm_b_out, moe_w_group, moe_b_group, moe_w_expert, moe_b_expert, moe_w_gate, moe_w_up, moe_w_down, final_g):
    bsz, seq, d = x.shape
    t = bsz * seq
    x2 = x.reshape(t, d)
    row = lambda a: a.reshape(1, -1)

    router0 = _router_inputs(norm_ffn_g[0], moe_w_group[0], moe_b_group[0], moe_w_expert[0],
                             moe_b_expert[0], TS_CONV)
    h1, hn1, route0, route_t0, cnt0 = _conv_layer(
        x2, seq, row(norm_mix_g[0]), cv_w_pw1[0].astype(BF16), row(cv_b_pw1[0]),
        jnp.broadcast_to(cv_w_dw[0][:, None, :], (CONV_WIDTH, SUBLANES, d)),
        jnp.broadcast_to(cv_b_dw[0][None, :], (SUBLANES, d)),
        row(cv_ln_g[0]), row(cv_ln_b[0]), cv_w_pw2[0].astype(BF16),
        row(cv_b_pw2[0]), router0)
    ga0, gb0 = _moe(hn1, route_t0, cnt0, moe_w_gate, moe_w_up, moe_w_down, 0)

    idx = jnp.arange(GMLP_BLOCK)
    mask = (idx[None, :] // GMLP_CHUNK) <= (idx[:, None] // GMLP_CHUNK)
    ws = jnp.where(mask[None], gm_w_s[0], 0.0).astype(BF16)
    router1 = _router_inputs(norm_ffn_g[1], moe_w_group[1], moe_b_group[1], moe_w_expert[1],
                             moe_b_expert[1], TS_GMLP)
    h2, hn2, route1, route_t1, cnt1 = _gmlp_layer(
        h1, ga0, gb0, route0, row(norm_mix_g[1]), gm_w_in[0].astype(BF16), row(gm_b_in[0]),
        row(gm_v_norm_g[0]), ws, jnp.transpose(gm_b_s[0]), gm_w_out[0].astype(BF16),
        row(gm_b_out[0]), router1)
    ga1, gb1 = _moe(hn2, route_t1, cnt1, moe_w_gate, moe_w_up, moe_w_down, 1)

    out = _final(h2, ga1, gb1, route1, row(final_g))
    return out.reshape(bsz, seq, d)
```

```python
import functools

import jax
import jax.numpy as jnp
from jax import lax
from jax.experimental import pallas as pl
from jax.experimental.pallas import tpu as pltpu
from jax.experimental.pallas import tpu_sc as plsc

D_MODEL = 1024
CONV_WIDTH = 31
GMLP_BLOCK = 128
GMLP_CHUNK = 64
GMLP_INNER = 2 * D_MODEL
GMLP_HEADS = 8
GMLP_HEAD_DIM = GMLP_INNER // GMLP_HEADS
N_GROUPS = 4
EXPERTS_PER_GROUP = 8
N_EXPERTS = N_GROUPS * EXPERTS_PER_GROUP
D_EXPERT = D_MODEL // 2
EPS = 1e-6

LANES = 128
SUBLANES = 8
HIST = 32
TS_CONV = 512
TS_GMLP = 512
TS_FINAL = 1024
BM = 512
IN_SLOTS = 4
OUT_SLOTS = 3
RC = 32
CONV_CW = 256
GLU_CW = 256
GMLP_CW = 512
GROUP_ROW0 = N_EXPERTS
ROUTER_ROWS = 48
VMEM_LIMIT = 56 * 1024 * 1024
SC_DISPATCH_CHUNK = 128
SC_COMBINE_CHUNK = 128

R_E1, R_E2, R_RANK1, R_RANK2, R_GATE1, R_GATE2 = range(6)

F32 = jnp.float32
BF16 = jnp.bfloat16


def _gelu_tanh(x):
    c = 0.7978845608028654
    t = jnp.tanh(x * (c + (c * 0.044715) * (x * x)))
    hx = 0.5 * x
    return hx + hx * t


def _rms(xf, g):
    return xf * lax.rsqrt(jnp.mean(xf * xf, axis=-1, keepdims=True) + EPS) * g


def _pack_bf16_pairs(xb):
    w = xb.shape[1] // 2
    bits = lax.bitcast_convert_type(xb.astype(F32), jnp.int32)
    return lax.shift_right_logical(bits[:, :w], 16) | bits[:, w:]


def _unpack_pairs_f32(p):
    lo = lax.bitcast_convert_type(lax.shift_left(p, 16), F32)
    hi = lax.bitcast_convert_type(p & jnp.int32(-65536), F32)
    return lo, hi


def _moe_combine(h, rp, p1, p2):
    w = h.shape[1] // 2
    g1 = rp[:, R_GATE1:R_GATE1 + 1]
    g2 = rp[:, R_GATE2:R_GATE2 + 1]
    lo1, hi1 = _unpack_pairs_f32(p1)
    lo2, hi2 = _unpack_pairs_f32(p2)
    return jnp.concatenate([h[:, :w] + g1 * lo1 + g2 * lo2,
                            h[:, w:] + g1 * hi1 + g2 * hi2], axis=1)


def _row_loop(n_rows, fn, unroll=True):
    def step(ci, carry):
        fn(pl.ds(pl.multiple_of(ci * RC, RC), RC))
        return carry

    lax.fori_loop(0, n_rows // RC, step, 0, unroll=unroll)


def _route_tail(h_ref, gffn_ref, wrh_ref, wrl_ref, br_ref, triu_ref, run_ref,
                hn_out_ref, route_ref, route_t_ref, cnt_ref, hi_s, lo_s, live=1.0):
    ts = h_ref.shape[0]

    def norm_rows(rows):
        hn2 = _rms(h_ref[rows, :], gffn_ref[...])
        hi = hn2.astype(BF16)
        hf = hi.astype(F32)
        hi_s[rows, :] = hi
        lo_s[rows, :] = (hn2 - hf).astype(BF16)
        hn_out_ref[rows, :] = _pack_bf16_pairs(hi)

    _row_loop(ts, norm_rows)

    nt = (((1,), (1,)), ((), ()))
    hi = hi_s[...]
    lt = (lax.dot_general(wrh_ref[...], hi, nt, preferred_element_type=F32)
          + lax.dot_general(wrl_ref[...], hi, nt, preferred_element_type=F32)
          + lax.dot_general(wrh_ref[...], lo_s[...], nt, preferred_element_type=F32)
          + br_ref[...])
    sub = lax.broadcasted_iota(jnp.int32, (SUBLANES, ts), 0).astype(F32)
    ninf = jnp.float32(-jnp.inf)
    big = jnp.float32(1e9)
    first_idx = lambda hit: jnp.min(jnp.where(hit, sub, big), axis=0, keepdims=True)

    g_ok = sub < N_GROUPS
    lg = jnp.where(g_ok, lt[GROUP_ROW0:GROUP_ROW0 + SUBLANES, :], ninf)
    gmax = jnp.max(lg, axis=0, keepdims=True)
    gsel = first_idx(lg == gmax)
    p_g = 1.0 / jnp.sum(jnp.where(g_ok, jnp.exp(lg - gmax), 0.0), axis=0, keepdims=True)
    le = lt[0:EXPERTS_PER_GROUP, :]
    for g in range(1, N_GROUPS):
        le = jnp.where(gsel == g, lt[g * EXPERTS_PER_GROUP:(g + 1) * EXPERTS_PER_GROUP, :], le)
    v1 = jnp.max(le, axis=0, keepdims=True)
    i1 = first_idx(le == v1)
    le2 = jnp.where(sub == i1, ninf, le)
    v2 = jnp.max(le2, axis=0, keepdims=True)
    i2 = first_idx(le2 == v2)
    e = jnp.exp(v2 - v1)
    den = 1.0 + e
    gate1 = p_g * (1.0 / den)
    gate2 = p_g * (e / den)
    e1 = gsel * EXPERTS_PER_GROUP + i1
    e2 = gsel * EXPERTS_PER_GROUP + i2

    eid = lax.broadcasted_iota(jnp.int32, (N_EXPERTS, ts), 0).astype(F32)
    oh1 = eid == e1
    oh2 = eid == e2
    oh = jnp.where(oh1 | oh2, 1.0, 0.0)
    run = run_ref[...]
    c = jnp.dot(oh.astype(BF16), triu_ref[...], preferred_element_type=F32) + run
    rank1 = jnp.sum(jnp.where(oh1, c, 0.0), axis=0, keepdims=True)
    rank2 = jnp.sum(jnp.where(oh2, c, 0.0), axis=0, keepdims=True)
    run = run + jnp.broadcast_to(jnp.sum(oh, axis=1, keepdims=True) * live, run.shape)
    run_ref[...] = run
    cnt_ref[...] = run[:, :LANES]

    rec_t = jnp.concatenate([e1, e2, rank1, rank2, gate1, gate2,
                             jnp.zeros((SUBLANES - 6, ts), F32)], axis=0)
    route_t_ref[...] = rec_t
    route_ref[...] = jnp.concatenate([rec_t, jnp.zeros((LANES - SUBLANES, ts), F32)], axis=0).T


def _conv_layer_body(tiles_per_seq,
                     x_ref, gmix_ref, wpw1_ref, bpw1_ref, wdw_ref, bdw_ref, lng_ref, lnb_ref,
                     wpw2_ref, bpw2_ref, gffn_ref, wrh_ref, wrl_ref, br_ref, triu_ref,
                     h_out_ref, hn_out_ref, route_ref, route_t_ref, cnt_ref,
                     hn_s, zext_ref, zs_ref, y_s, a_s, hi_s, lo_s, run_ref):
    i = pl.program_id(0)
    ts, d = x_ref.shape

    @pl.when(i == 0)
    def _():
        run_ref[...] = jnp.zeros_like(run_ref)

    @pl.when(i % tiles_per_seq == 0)
    def _():
        zext_ref[0:HIST, :] = jnp.zeros((HIST, d), F32)

    def norm_rows(rows):
        hn_s[rows, :] = _rms(x_ref[rows, :], gmix_ref[...]).astype(BF16)

    _row_loop(ts, norm_rows)

    hn = hn_s[...]
    for c0 in range(0, d, GLU_CW):
        ca = slice(c0, c0 + GLU_CW)
        cg = slice(d + c0, d + c0 + GLU_CW)
        pa = jnp.dot(hn, wpw1_ref[:, ca], preferred_element_type=F32) + bpw1_ref[:, ca]
        pg = jnp.dot(hn, wpw1_ref[:, cg], preferred_element_type=F32) + bpw1_ref[:, cg]
        zext_ref[HIST:HIST + ts, ca] = pa * jax.nn.sigmoid(pg)

    span = ts + HIST - SUBLANES
    sub_id = lax.broadcasted_iota(jnp.int32, (SUBLANES, LANES), 0)
    shifts = range(1, SUBLANES)
    for c0 in range(0, d, LANES):
        cols = slice(c0, c0 + LANES)

        def rotate(g):
            up = {0: g}
            for r in (4, 2, 6, 1, 3, 5, 7):
                step = r & -r
                up[r] = pltpu.roll(up[r - step], SUBLANES - step, axis=0)
            return [up[r] for r in shifts]

        cur = rotate(zext_ref[0:SUBLANES, cols])
        for m0 in range(0, span, SUBLANES):
            nxt = rotate(zext_ref[m0 + SUBLANES:m0 + 2 * SUBLANES, cols])
            for r in shifts:
                zs_ref[r - 1, m0:m0 + SUBLANES, cols] = jnp.where(
                    sub_id < SUBLANES - r, cur[r - 1], nxt[r - 1])
            cur = nxt

    first = HIST - (CONV_WIDTH - 1)

    def conv_rows(rows):
        r0 = rows.start
        groups = RC // SUBLANES
        for c0 in range(0, d, CONV_CW):
            cols = slice(c0, c0 + CONV_CW)
            accs = [bdw_ref[:, cols]] * groups
            for k in range(CONV_WIDTH):
                q, r = divmod(first + k, SUBLANES)
                w8 = wdw_ref[k, :, cols]
                for g in range(groups):
                    src = pl.ds(r0 + (q + g) * SUBLANES, SUBLANES)
                    slab = zext_ref[src, cols] if r == 0 else zs_ref[r - 1, src, cols]
                    accs[g] = accs[g] + w8 * slab
            for g in range(groups):
                y_s[pl.ds(r0 + g * SUBLANES, SUBLANES), cols] = accs[g]
        y = y_s[rows, :]
        mu = jnp.mean(y, axis=-1, keepdims=True)
        yc = y - mu
        yn = yc * lax.rsqrt(jnp.mean(yc * yc, axis=-1, keepdims=True) + EPS)
        yn = yn * lng_ref[...] + lnb_ref[...]
        a_s[rows, :] = (yn * jax.nn.sigmoid(yn)).astype(BF16)

    _row_loop(ts, conv_rows, unroll=True)
    zext_ref[0:HIST, :] = zext_ref[ts:ts + HIST, :]

    for r0 in range(0, ts, ts // 2):
        rows = slice(r0, r0 + ts // 2)
        m = jnp.dot(a_s[rows, :], wpw2_ref[...], preferred_element_type=F32) + bpw2_ref[...]
        h_out_ref[rows, :] = x_ref[rows, :] + m
    _route_tail(h_out_ref, gffn_ref, wrh_ref, wrl_ref, br_ref, triu_ref, run_ref,
                hn_out_ref, route_ref, route_t_ref, cnt_ref, hi_s, lo_s)


def _gmlp_layer_body(h_ref, g1_ref, g2_ref, rprev_ref, gmix_ref, win_ref, bin_ref, vg_ref,
                     ws_ref, bst_ref, wout_ref, bout_ref, gffn_ref, wrh_ref, wrl_ref, br_ref,
                     triu_ref,
                     h_out_ref, hn_out_ref, route_ref, route_t_ref, cnt_ref,
                     hn_s, u_s, v_s, gated_s, hprev_s, hi_s, lo_s, run_ref):
    i = pl.program_id(0)
    ts = h_ref.shape[0]

    @pl.when(i == 0)
    def _():
        run_ref[...] = jnp.zeros_like(run_ref)
        hprev_s[...] = jnp.zeros_like(hprev_s)

    _route_tail(hprev_s, gffn_ref, wrh_ref, wrl_ref, br_ref, triu_ref, run_ref,
                hn_out_ref, route_ref, route_t_ref, cnt_ref, hi_s, lo_s,
                live=jnp.where(i > 0, 1.0, 0.0))

    def norm_rows(rows):
        h = _moe_combine(h_ref[rows, :], rprev_ref[rows, :], g1_ref[rows, :], g2_ref[rows, :])
        h_out_ref[rows, :] = h
        hn_s[rows, :] = _rms(h, gmix_ref[...]).astype(BF16)

    _row_loop(ts, norm_rows)

    hn = hn_s[...]
    ssq = jnp.zeros((ts, 1), F32)
    for c0 in range(0, 2 * GMLP_INNER, GMLP_CW):
        cols = slice(c0, c0 + GMLP_CW)
        p = jnp.dot(hn, win_ref[:, cols], preferred_element_type=F32)
        zc = _gelu_tanh(p + bin_ref[:, cols])
        if c0 < GMLP_INNER:
            u_s[:, cols] = zc
        else:
            v_s[:, c0 - GMLP_INNER:c0 - GMLP_INNER + GMLP_CW] = zc
            ssq = ssq + jnp.sum(zc * zc, axis=-1, keepdims=True)
    rs = lax.rsqrt(ssq * (1.0 / GMLP_INNER) + EPS)

    for b0 in range(0, ts, GMLP_BLOCK):
        rows = slice(b0, b0 + GMLP_BLOCK)
        for hd in range(GMLP_HEADS):
            cols = slice(hd * GMLP_HEAD_DIM, (hd + 1) * GMLP_HEAD_DIM)
            vv = (v_s[rows, cols] * rs[rows] * vg_ref[:, cols]).astype(BF16)
            sv = jnp.dot(ws_ref[hd], vv, preferred_element_type=F32) + bst_ref[:, hd:hd + 1]
            gated_s[rows, cols] = (u_s[rows, cols] * sv).astype(BF16)

    out = jnp.dot(gated_s[...], wout_ref[...], preferred_element_type=F32) + bout_ref[...]
    hnew = h_out_ref[...] + out
    h_out_ref[...] = hnew
    hprev_s[...] = hnew


def _expert_body(first_ref, count_ref, nused_ref, rows_hbm, wg_ref, wu_ref, wd_ref, y_hbm,
                 xbuf, ybuf, wgu_s, wd_s, in_sem, out_sem):
    e = pl.program_id(0)
    first = first_ref[e]
    count = count_ref[e]
    n_used = nused_ref[0]

    def rows_of(g):
        return pl.ds(pl.multiple_of(g * BM, BM), BM)

    def in_copy(g):
        slot = lax.rem(g, IN_SLOTS)
        return pltpu.make_async_copy(rows_hbm.at[rows_of(g)], xbuf.at[slot], in_sem.at[slot])

    def out_copy(g):
        slot = lax.rem(g, OUT_SLOTS)
        return pltpu.make_async_copy(ybuf.at[slot], y_hbm.at[rows_of(g)], out_sem.at[slot])

    @pl.when(e == 0)
    def _():
        for g in range(IN_SLOTS - 1):
            @pl.when(g < n_used)
            def _():
                in_copy(g).start()

    @pl.when(count > 0)
    def _():
        wgu_s[:, :D_EXPERT] = wg_ref[0, 0].astype(BF16)
        wgu_s[:, D_EXPERT:] = wu_ref[0, 0].astype(BF16)
        wd_s[...] = wd_ref[0, 0].astype(BF16)

        def block(g, carry):
            in_copy(g).wait()

            @pl.when(g + IN_SLOTS - 1 < n_used)
            def _():
                in_copy(g + IN_SLOTS - 1).start()

            @pl.when(g >= OUT_SLOTS)
            def _():
                out_copy(g - OUT_SLOTS).wait()

            lo, hi = _unpack_pairs_f32(xbuf[lax.rem(g, IN_SLOTS)])
            half = lo.shape[1]
            gu = (jnp.dot(lo.astype(BF16), wgu_s[:half, :], preferred_element_type=F32)
                  + jnp.dot(hi.astype(BF16), wgu_s[half:, :], preferred_element_type=F32))
            gate = gu[:, :D_EXPERT]
            hb = (gate * jax.nn.sigmoid(gate)) * gu[:, D_EXPERT:]
            y = jnp.dot(hb.astype(BF16), wd_s[...], preferred_element_type=F32)
            ybuf[lax.rem(g, OUT_SLOTS)] = _pack_bf16_pairs(y.astype(BF16))
            out_copy(g).start()
            return carry

        lax.fori_loop(first, first + count, block, 0)

    @pl.when(e == pl.num_programs(0) - 1)
    def _():
        for back in range(OUT_SLOTS, 0, -1):
            @pl.when(n_used >= back)
            def _():
                out_copy(n_used - back).wait()


def _final_body(h_ref, g1_ref, g2_ref, rprev_ref, gfin_ref, o_ref):
    h = _moe_combine(h_ref[...], rprev_ref[...], g1_ref[...], g2_ref[...])
    o_ref[...] = _rms(h, gfin_ref[...])


def _const_spec(shape):
    return pl.BlockSpec(shape, lambda i: (0,) * len(shape), pipeline_mode=pl.Buffered(1))


def _row_spec(ts, width):
    return pl.BlockSpec((ts, width), lambda i: (i, 0))


def _layer_out(t, d, ts, lag=0):
    last = t // ts - 1
    mixed = lambda i: jnp.minimum(i, last)
    routed = lambda i: jnp.maximum(i - lag, 0)
    shapes = [jax.ShapeDtypeStruct((t, d), F32),
              jax.ShapeDtypeStruct((t, d // 2), jnp.int32),
              jax.ShapeDtypeStruct((t, LANES), F32),
              jax.ShapeDtypeStruct((SUBLANES, t), F32),
              jax.ShapeDtypeStruct((N_EXPERTS, LANES), F32)]
    specs = [pl.BlockSpec((ts, d), lambda i: (mixed(i), 0)),
             pl.BlockSpec((ts, d // 2), lambda i: (routed(i), 0)),
             pl.BlockSpec((ts, LANES), lambda i: (routed(i), 0)),
             pl.BlockSpec((SUBLANES, ts), lambda i: (0, routed(i))),
             pl.BlockSpec((N_EXPERTS, LANES), lambda i: (0, 0))]
    return shapes, specs


def _router_specs(d, ts):
    return [_const_spec((1, d)), _const_spec((ROUTER_ROWS, d)), _const_spec((ROUTER_ROWS, d)),
            _const_spec((ROUTER_ROWS, ts)), _const_spec((ts, ts))]


def _router_scratch(ts, d):
    return [pltpu.VMEM((ts, d), BF16), pltpu.VMEM((ts, d), BF16),
            pltpu.VMEM((N_EXPERTS, ts), F32)]


def _layer_params():
    return pltpu.CompilerParams(dimension_semantics=("arbitrary",), vmem_limit_bytes=VMEM_LIMIT)


def _conv_layer(x2, seq, gmix, wpw1, bpw1, wdw, bdw, lng, lnb, wpw2, bpw2, router):
    t, d = x2.shape
    ts = TS_CONV
    shapes, out_specs = _layer_out(t, d, ts)
    body = functools.partial(_conv_layer_body, seq // ts)
    return pl.pallas_call(
        body,
        grid=(t // ts,),
        in_specs=[_row_spec(ts, d), _const_spec((1, d)), _const_spec((d, 2 * d)),
                  _const_spec((1, 2 * d)), _const_spec((CONV_WIDTH, SUBLANES, d)),
                  _const_spec((SUBLANES, d)),
                  _const_spec((1, d)), _const_spec((1, d)), _const_spec((d, d)),
                  _const_spec((1, d))] + _router_specs(d, ts),
        out_specs=out_specs,
        out_shape=shapes,
        scratch_shapes=[pltpu.VMEM((ts, d), BF16),
                        pltpu.VMEM((ts + HIST, d), F32),
                        pltpu.VMEM((SUBLANES - 1, ts + HIST - SUBLANES, d), F32),
                        pltpu.VMEM((ts, d), F32),
                        pltpu.VMEM((ts, d), BF16)] + _router_scratch(ts, d),
        compiler_params=_layer_params(),
        name="conv_layer",
    )(x2, gmix, wpw1, bpw1, wdw, bdw, lng, lnb, wpw2, bpw2, *router)


def _gmlp_layer(h, g1, g2, rprev, gmix, win, bin_, vg, ws, bst, wout, bout, router):
    t, d = h.shape
    ts = TS_GMLP
    n_tiles = t // ts
    shapes, out_specs = _layer_out(t, d, ts, lag=1)
    tile_spec = lambda width: pl.BlockSpec((ts, width), lambda i: (jnp.minimum(i, n_tiles - 1), 0))
    return pl.pallas_call(
        _gmlp_layer_body,
        grid=(n_tiles + 1,),
        in_specs=[tile_spec(d), tile_spec(d // 2), tile_spec(d // 2), tile_spec(LANES),
                  _const_spec((1, d)), _const_spec((d, 2 * GMLP_INNER)),
                  _const_spec((1, 2 * GMLP_INNER)), _const_spec((1, GMLP_INNER)),
                  _const_spec((GMLP_HEADS, GMLP_BLOCK, GMLP_BLOCK)),
                  _const_spec((GMLP_BLOCK, GMLP_HEADS)), _const_spec((GMLP_INNER, d)),
                  _const_spec((1, d))] + _router_specs(d, ts),
        out_specs=out_specs,
        out_shape=shapes,
        scratch_shapes=[pltpu.VMEM((ts, d), BF16),
                        pltpu.VMEM((ts, GMLP_INNER), F32),
                        pltpu.VMEM((ts, GMLP_INNER), F32),
                        pltpu.VMEM((ts, GMLP_INNER), BF16),
                        pltpu.VMEM((ts, d), F32)] + _router_scratch(ts, d),
        compiler_params=_layer_params(),
        name="gmlp_layer",
    )(h, g1, g2, rprev, gmix, win, bin_, vg, ws, bst, wout, bout, *router)


def _experts(rows, first_blk, blk_count, n_used, wg, wu, wd, layer):
    r, w = rows.shape
    d = wg.shape[2]

    def w_map(e, first, count, nu):
        return (layer, e, 0, 0)

    any_space = pl.BlockSpec(memory_space=pl.ANY)
    grid_spec = pltpu.PrefetchScalarGridSpec(
        num_scalar_prefetch=3,
        grid=(N_EXPERTS,),
        in_specs=[any_space,
                  pl.BlockSpec((1, 1, d, D_EXPERT), w_map),
                  pl.BlockSpec((1, 1, d, D_EXPERT), w_map),
                  pl.BlockSpec((1, 1, D_EXPERT, d), w_map)],
        out_specs=any_space,
        scratch_shapes=[pltpu.VMEM((IN_SLOTS, BM, w), rows.dtype),
                        pltpu.VMEM((OUT_SLOTS, BM, w), rows.dtype),
                        pltpu.VMEM((d, 2 * D_EXPERT), BF16),
                        pltpu.VMEM((D_EXPERT, d), BF16),
                        pltpu.SemaphoreType.DMA((IN_SLOTS,)),
                        pltpu.SemaphoreType.DMA((OUT_SLOTS,))],
    )
    return pl.pallas_call(
        _expert_body,
        grid_spec=grid_spec,
        out_shape=jax.ShapeDtypeStruct((r, w), rows.dtype),
        compiler_params=_layer_params(),
        name="experts",
    )(first_blk, blk_count, n_used, rows, wg, wu, wd)


def _final(h, g1, g2, rprev, gfin):
    t, d = h.shape
    ts = TS_FINAL
    return pl.pallas_call(
        _final_body,
        grid=(t // ts,),
        in_specs=[_row_spec(ts, d), _row_spec(ts, d // 2), _row_spec(ts, d // 2),
                  _row_spec(ts, LANES),
                  _const_spec((1, d))],
        out_specs=_row_spec(ts, d),
        out_shape=jax.ShapeDtypeStruct((t, d), F32),
        compiler_params=_layer_params(),
        name="final_norm",
    )(h, g1, g2, rprev, gfin)


def _plan_body(cnt_ref, rt_ref, d1_ref, d2_ref, first_ref, count_ref, nu_ref):
    e1 = rt_ref[R_E1:R_E1 + 1, :]
    e2 = rt_ref[R_E2:R_E2 + 1, :]
    d1 = rt_ref[R_RANK1:R_RANK1 + 1, :]
    d2 = rt_ref[R_RANK2:R_RANK2 + 1, :]
    pb = jnp.int32(0)
    for e in range(N_EXPERTS):
        nb = lax.shift_right_logical(cnt_ref[e] + (BM - 1), BM.bit_length() - 1)
        ps = (pb * BM).astype(F32)
        d1 = d1 + jnp.where(e1 == e, ps, 0.0)
        d2 = d2 + jnp.where(e2 == e, ps, 0.0)
        first_ref[e] = pb
        count_ref[e] = nb
        pb = pb + nb
    nu_ref[0] = pb
    d1_ref[...] = d1.astype(jnp.int32)
    d2_ref[...] = d2.astype(jnp.int32)


def _plan(route_t, counts):
    t = route_t.shape[1]
    smem = pl.BlockSpec(memory_space=pltpu.SMEM)
    vmem = pl.BlockSpec(memory_space=pltpu.VMEM)
    return pl.pallas_call(
        _plan_body,
        in_specs=[smem, vmem],
        out_specs=[vmem, vmem, smem, smem, smem],
        out_shape=[jax.ShapeDtypeStruct((1, t), jnp.int32),
                   jax.ShapeDtypeStruct((1, t), jnp.int32),
                   jax.ShapeDtypeStruct((N_EXPERTS,), jnp.int32),
                   jax.ShapeDtypeStruct((N_EXPERTS,), jnp.int32),
                   jax.ShapeDtypeStruct((1,), jnp.int32)],
        name="moe_plan",
    )(counts, route_t)


def _sc_workers():
    info = plsc.get_sparse_core_info()
    return info.num_cores, info.num_cores * info.num_subcores


def _sc_mesh():
    return plsc.VectorSubcoreMesh(core_axis_name="c", subcore_axis_name="s")


def _sc_worker_id(num_cores):
    return lax.axis_index("s") * num_cores + lax.axis_index("c")


def _dispatch(hn, dest1, dest2, n_rows):
    t, w = hn.shape
    num_cores, n_workers = _sc_workers()
    per_w = t // n_workers
    chunk = SC_DISPATCH_CHUNK

    def body(hn_hbm, d1_hbm, d2_hbm, rows_hbm, buf, i1, i2, sem_rows, sem_1, sem_2):
        base_w = _sc_worker_id(num_cores) * per_w

        @pl.loop(0, per_w // chunk)
        def _(j):
            base = pl.multiple_of(base_w + j * chunk, chunk)
            loads = [pltpu.async_copy(hn_hbm.at[pl.ds(base, chunk)], buf, sem_rows),
                     pltpu.async_copy(d1_hbm.at[:, pl.ds(base, chunk)], i1, sem_1),
                     pltpu.async_copy(d2_hbm.at[:, pl.ds(base, chunk)], i2, sem_2)]
            for c in loads:
                c.wait()
            scatters = [pltpu.async_copy(buf, rows_hbm.at[i1.at[0]], sem_1),
                        pltpu.async_copy(buf, rows_hbm.at[i2.at[0]], sem_2)]
            for c in scatters:
                c.wait()

    return pl.kernel(
        body,
        out_type=jax.ShapeDtypeStruct((n_rows, w), hn.dtype),
        mesh=_sc_mesh(),
        scratch_types=[pltpu.VMEM((chunk, w), hn.dtype),
                       pltpu.VMEM((1, chunk), jnp.int32),
                       pltpu.VMEM((1, chunk), jnp.int32),
                       pltpu.SemaphoreType.DMA, pltpu.SemaphoreType.DMA,
                       pltpu.SemaphoreType.DMA],
        name="moe_dispatch",
    )(hn, dest1, dest2)


def _combine_gather(y, dest1, dest2):
    d = y.shape[1]
    t = dest1.shape[1]
    num_cores, n_workers = _sc_workers()
    per_w = t // n_workers
    chunk = SC_COMBINE_CHUNK

    half = chunk // 2

    def body(y_hbm, d1_hbm, d2_hbm, g1_hbm, g2_hbm, buf_a, buf_b, idx,
             gsem_a, gsem_b, wsem_a, wsem_b):
        base_w = _sc_worker_id(num_cores) * per_w
        halves = ((buf_a, gsem_a, wsem_a, 0), (buf_b, gsem_b, wsem_b, half))

        def wait_write(buf, wsem):
            pltpu.make_async_copy(buf, g1_hbm.at[pl.ds(0, half)], wsem).wait()

        @pl.loop(0, per_w // chunk)
        def _(j):
            base = pl.multiple_of(base_w + j * chunk, chunk)
            for table, (d_hbm, g_hbm) in enumerate(((d1_hbm, g1_hbm), (d2_hbm, g2_hbm))):
                pltpu.sync_copy(d_hbm.at[:, pl.ds(base, chunk)], idx)
                gathers = []
                for buf, gsem, wsem, off in halves:
                    if table == 0:
                        @pl.when(j > 0)
                        def _():
                            wait_write(buf, wsem)
                    else:
                        wait_write(buf, wsem)
                    gathers.append(pltpu.async_copy(
                        y_hbm.at[idx.at[0, pl.ds(off, half)]], buf, gsem))
                for (buf, gsem, wsem, off), gather in zip(halves, gathers):
                    gather.wait()
                    pltpu.async_copy(buf, g_hbm.at[pl.ds(base + off, half)], wsem)

        for buf, gsem, wsem, off in halves:
            wait_write(buf, wsem)

    out = jax.ShapeDtypeStruct((t, d), y.dtype)
    return pl.kernel(
        body,
        out_type=(out, out),
        mesh=_sc_mesh(),
        scratch_types=[pltpu.VMEM((half, d), y.dtype),
                       pltpu.VMEM((half, d), y.dtype),
                       pltpu.VMEM((1, chunk), jnp.int32),
                       pltpu.SemaphoreType.DMA, pltpu.SemaphoreType.DMA,
                       pltpu.SemaphoreType.DMA, pltpu.SemaphoreType.DMA],
        name="moe_combine_gather",
    )(y, dest1, dest2)


def _moe(hn, route_t, cnt, wg, wu, wd, layer):
    t = hn.shape[0]
    n_blk = (2 * t) // BM + N_EXPERTS
    counts = cnt[:, 0].astype(jnp.int32)
    dest1, dest2, first_blk, blk_count, n_used = _plan(route_t, counts)
    rows = _dispatch(hn, dest1, dest2, n_blk * BM)
    y = _experts(rows, first_blk, blk_count, n_used, wg, wu, wd, layer)
    return _combine_gather(y, dest1, dest2)


def _router_inputs(gffn, w_group, b_group, w_expert, b_expert, ts):
    d = w_group.shape[0]
    wr = jnp.zeros((ROUTER_ROWS, d), F32)
    wr = wr.at[:N_EXPERTS].set(w_expert.T).at[GROUP_ROW0:GROUP_ROW0 + N_GROUPS].set(w_group.T)
    br = jnp.zeros((ROUTER_ROWS,), F32)
    br = br.at[:N_EXPERTS].set(b_expert).at[GROUP_ROW0:GROUP_ROW0 + N_GROUPS].set(b_group)
    w_hi = wr.astype(BF16)
    w_lo = (wr - w_hi.astype(F32)).astype(BF16)
    idx = jnp.arange(ts)
    triu = (idx[:, None] < idx[None, :]).astype(BF16)
    return (gffn.reshape(1, -1), w_hi, w_lo, jnp.broadcast_to(br[:, None], (ROUTER_ROWS, ts)), triu)


def kernel(x, norm_mix_g, norm_ffn_g, cv_w_pw1, cv_b_pw1, cv_w_dw, cv_b_dw, cv_ln_g, cv_ln_b, cv_w_pw2, cv_b_pw2, gm_w_in, gm_b_in, gm_v_norm_g, gm_w_s, gm_b_s, gm_w_out, gm_b_out, moe_w_group, moe_b_group, moe_w_expert, moe_b_expert, moe_w_gate, moe_w_up, moe_w_down, final_g):
    bsz, seq, d = x.shape
    t = bsz * seq
    x2 = x.reshape(t, d)
    row = lambda a: a.reshape(1, -1)

    router0 = _router_inputs(norm_ffn_g[0], moe_w_group[0], moe_b_group[0], moe_w_expert[0],
                             moe_b_expert[0], TS_CONV)
    h1, hn1, route0, route_t0, cnt0 = _conv_layer(
        x2, seq, row(norm_mix_g[0]), cv_w_pw1[0].astype(BF16), row(cv_b_pw1[0]),
        jnp.broadcast_to(cv_w_dw[0][:, None, :], (CONV_WIDTH, SUBLANES, d)),
        jnp.broadcast_to(cv_b_dw[0][None, :], (SUBLANES, d)),
        row(cv_ln_g[0]), row(cv_ln_b[0]), cv_w_pw2[0].astype(BF16),
        row(cv_b_pw2[0]), router0)
    ga0, gb0 = _moe(hn1, route_t0, cnt0, moe_w_gate, moe_w_up, moe_w_down, 0)

    idx = jnp.arange(GMLP_BLOCK)
    mask = (idx[None, :] // GMLP_CHUNK) <= (idx[:, None] // GMLP_CHUNK)
    ws = jnp.where(mask[None], gm_w_s[0], 0.0).astype(BF16)
    router1 = _router_inputs(norm_ffn_g[1], moe_w_group[1], moe_b_group[1], moe_w_expert[1],
                             moe_b_expert[1], TS_GMLP)
    h2, hn2, route1, route_t1, cnt1 = _gmlp_layer(
        h1, ga0, gb0, route0, row(norm_mix_g[1]), gm_w_in[0].astype(BF16), row(gm_b_in[0]),
        row(gm_v_norm_g[0]), ws, jnp.transpose(gm_b_s[0]), gm_w_out[0].astype(BF16),
        row(gm_b_out[0]), router1)
    ga1, gb1 = _moe(hn2, route_t1, cnt1, moe_w_gate, moe_w_up, moe_w_down, 1)

    out = _final(h2, ga1, gb1, route1, row(final_g))
    return out.reshape(bsz, seq, d)
```

```python
import functools

import jax
import jax.numpy as jnp
from jax import lax
from jax.experimental import pallas as pl
from jax.experimental.pallas import tpu as pltpu
from jax.experimental.pallas import tpu_sc as plsc

D_MODEL = 1024
CONV_WIDTH = 31
GMLP_BLOCK = 128
GMLP_CHUNK = 64
GMLP_INNER = 2 * D_MODEL
GMLP_HEADS = 8
GMLP_HEAD_DIM = GMLP_INNER // GMLP_HEADS
N_GROUPS = 4
EXPERTS_PER_GROUP = 8
N_EXPERTS = N_GROUPS * EXPERTS_PER_GROUP
D_EXPERT = D_MODEL // 2
EPS = 1e-6

LANES = 128
SUBLANES = 8
HIST = 32
TS_CONV = 512
TS_GMLP = 512
TS_FINAL = 1024
BM = 512
IN_SLOTS = 4
OUT_SLOTS = 3
CAST_ROWS = 256
RC = 32
CONV_CW = 256
GLU_CW = 256
GMLP_CW = 512
GROUP_ROW0 = N_EXPERTS
ROUTER_ROWS = 48
VMEM_LIMIT = 56 * 1024 * 1024
SC_DISPATCH_CHUNK = 128
SC_COMBINE_CHUNK = 128

R_E1, R_E2, R_RANK1, R_RANK2, R_GATE1, R_GATE2 = range(6)

F32 = jnp.float32
BF16 = jnp.bfloat16


def _gelu_tanh(x):
    c = 0.7978845608028654
    t = jnp.tanh(x * (c + (c * 0.044715) * (x * x)))
    hx = 0.5 * x
    return hx + hx * t


def _rms(xf, g):
    return xf * lax.rsqrt(jnp.mean(xf * xf, axis=-1, keepdims=True) + EPS) * g


def _pack_bf16_pairs(xb):
    w = xb.shape[1] // 2
    bits = lax.bitcast_convert_type(xb.astype(F32), jnp.int32)
    return lax.shift_right_logical(bits[:, :w], 16) | bits[:, w:]


def _unpack_pairs_f32(p):
    lo = lax.bitcast_convert_type(lax.shift_left(p, 16), F32)
    hi = lax.bitcast_convert_type(p & jnp.int32(-65536), F32)
    return lo, hi


def _moe_combine(h, rp, p1, p2):
    w = h.shape[1] // 2
    g1 = rp[:, R_GATE1:R_GATE1 + 1]
    g2 = rp[:, R_GATE2:R_GATE2 + 1]
    lo1, hi1 = _unpack_pairs_f32(p1)
    lo2, hi2 = _unpack_pairs_f32(p2)
    return jnp.concatenate([h[:, :w] + g1 * lo1 + g2 * lo2,
                            h[:, w:] + g1 * hi1 + g2 * hi2], axis=1)


def _row_loop(n_rows, fn, unroll=True):
    def step(ci, carry):
        fn(pl.ds(pl.multiple_of(ci * RC, RC), RC))
        return carry

    lax.fori_loop(0, n_rows // RC, step, 0, unroll=unroll)


def _route_tail(h_ref, gffn_ref, wrh_ref, wrl_ref, br_ref, triu_ref, run_ref,
                hn_out_ref, route_ref, route_t_ref, cnt_ref, hi_s, lo_s, live=1.0):
    ts = h_ref.shape[0]

    def norm_rows(rows):
        hn2 = _rms(h_ref[rows, :], gffn_ref[...])
        hi = hn2.astype(BF16)
        hf = hi.astype(F32)
        hi_s[rows, :] = hi
        lo_s[rows, :] = (hn2 - hf).astype(BF16)
        hn_out_ref[rows, :] = _pack_bf16_pairs(hi)

    _row_loop(ts, norm_rows)

    nt = (((1,), (1,)), ((), ()))
    hi = hi_s[...]
    lt = (lax.dot_general(wrh_ref[...], hi, nt, preferred_element_type=F32)
          + lax.dot_general(wrl_ref[...], hi, nt, preferred_element_type=F32)
          + lax.dot_general(wrh_ref[...], lo_s[...], nt, preferred_element_type=F32)
          + br_ref[...])
    sub = lax.broadcasted_iota(jnp.int32, (SUBLANES, ts), 0).astype(F32)
    ninf = jnp.float32(-jnp.inf)
    big = jnp.float32(1e9)
    first_idx = lambda hit: jnp.min(jnp.where(hit, sub, big), axis=0, keepdims=True)

    g_ok = sub < N_GROUPS
    lg = jnp.where(g_ok, lt[GROUP_ROW0:GROUP_ROW0 + SUBLANES, :], ninf)
    gmax = jnp.max(lg, axis=0, keepdims=True)
    gsel = first_idx(lg == gmax)
    p_g = 1.0 / jnp.sum(jnp.where(g_ok, jnp.exp(lg - gmax), 0.0), axis=0, keepdims=True)
    le = lt[0:EXPERTS_PER_GROUP, :]
    for g in range(1, N_GROUPS):
        le = jnp.where(gsel == g, lt[g * EXPERTS_PER_GROUP:(g + 1) * EXPERTS_PER_GROUP, :], le)
    v1 = jnp.max(le, axis=0, keepdims=True)
    i1 = first_idx(le == v1)
    le2 = jnp.where(sub == i1, ninf, le)
    v2 = jnp.max(le2, axis=0, keepdims=True)
    i2 = first_idx(le2 == v2)
    e = jnp.exp(v2 - v1)
    den = 1.0 + e
    gate1 = p_g * (1.0 / den)
    gate2 = p_g * (e / den)
    e1 = gsel * EXPERTS_PER_GROUP + i1
    e2 = gsel * EXPERTS_PER_GROUP + i2

    eid = lax.broadcasted_iota(jnp.int32, (N_EXPERTS, ts), 0).astype(F32)
    oh1 = eid == e1
    oh2 = eid == e2
    oh = jnp.where(oh1 | oh2, 1.0, 0.0)
    run = run_ref[...]
    c = jnp.dot(oh.astype(BF16), triu_ref[...], preferred_element_type=F32) + run
    rank1 = jnp.sum(jnp.where(oh1, c, 0.0), axis=0, keepdims=True)
    rank2 = jnp.sum(jnp.where(oh2, c, 0.0), axis=0, keepdims=True)
    run = run + jnp.broadcast_to(jnp.sum(oh, axis=1, keepdims=True) * live, run.shape)
    run_ref[...] = run
    cnt_ref[...] = run[:, :LANES]

    rec_t = jnp.concatenate([e1, e2, rank1, rank2, gate1, gate2,
                             jnp.zeros((SUBLANES - 6, ts), F32)], axis=0)
    route_t_ref[...] = rec_t
    route_ref[...] = jnp.concatenate([rec_t, jnp.zeros((LANES - SUBLANES, ts), F32)], axis=0).T


def _conv_layer_body(tiles_per_seq,
                     x_ref, gmix_ref, wpw1_ref, bpw1_ref, wdw_ref, bdw_ref, lng_ref, lnb_ref,
                     wpw2_ref, bpw2_ref, gffn_ref, wrh_ref, wrl_ref, br_ref, triu_ref,
                     h_out_ref, hn_out_ref, route_ref, route_t_ref, cnt_ref,
                     hn_s, zext_ref, zs_ref, y_s, a_s, hi_s, lo_s, run_ref):
    i = pl.program_id(0)
    ts, d = x_ref.shape

    @pl.when(i == 0)
    def _():
        run_ref[...] = jnp.zeros_like(run_ref)

    @pl.when(i % tiles_per_seq == 0)
    def _():
        zext_ref[0:HIST, :] = jnp.zeros((HIST, d), F32)

    def norm_rows(rows):
        hn_s[rows, :] = _rms(x_ref[rows, :], gmix_ref[...]).astype(BF16)

    _row_loop(ts, norm_rows)

    hn = hn_s[...]
    for c0 in range(0, d, GLU_CW):
        ca = slice(c0, c0 + GLU_CW)
        cg = slice(d + c0, d + c0 + GLU_CW)
        pa = jnp.dot(hn, wpw1_ref[:, ca], preferred_element_type=F32) + bpw1_ref[:, ca]
        pg = jnp.dot(hn, wpw1_ref[:, cg], preferred_element_type=F32) + bpw1_ref[:, cg]
        zext_ref[HIST:HIST + ts, ca] = pa * jax.nn.sigmoid(pg)

    span = ts + HIST - SUBLANES
    sub_id = lax.broadcasted_iota(jnp.int32, (SUBLANES, LANES), 0)
    shifts = range(1, SUBLANES)
    for c0 in range(0, d, LANES):
        cols = slice(c0, c0 + LANES)

        def rotate(g):
            up = {0: g}
            for r in (4, 2, 6, 1, 3, 5, 7):
                step = r & -r
                up[r] = pltpu.roll(up[r - step], SUBLANES - step, axis=0)
            return [up[r] for r in shifts]

        cur = rotate(zext_ref[0:SUBLANES, cols])
        for m0 in range(0, span, SUBLANES):
            nxt = rotate(zext_ref[m0 + SUBLANES:m0 + 2 * SUBLANES, cols])
            for r in shifts:
                zs_ref[r - 1, m0:m0 + SUBLANES, cols] = jnp.where(
                    sub_id < SUBLANES - r, cur[r - 1], nxt[r - 1])
            cur = nxt

    first = HIST - (CONV_WIDTH - 1)

    def conv_rows(rows):
        r0 = rows.start
        groups = RC // SUBLANES
        for c0 in range(0, d, CONV_CW):
            cols = slice(c0, c0 + CONV_CW)
            accs = [bdw_ref[:, cols]] * groups
            for k in range(CONV_WIDTH):
                q, r = divmod(first + k, SUBLANES)
                w8 = wdw_ref[k, :, cols]
                for g in range(groups):
                    src = pl.ds(r0 + (q + g) * SUBLANES, SUBLANES)
                    slab = zext_ref[src, cols] if r == 0 else zs_ref[r - 1, src, cols]
                    accs[g] = accs[g] + w8 * slab
            for g in range(groups):
                y_s[pl.ds(r0 + g * SUBLANES, SUBLANES), cols] = accs[g]
        y = y_s[rows, :]
        mu = jnp.mean(y, axis=-1, keepdims=True)
        yc = y - mu
        yn = yc * lax.rsqrt(jnp.mean(yc * yc, axis=-1, keepdims=True) + EPS)
        yn = yn * lng_ref[...] + lnb_ref[...]
        a_s[rows, :] = (yn * jax.nn.sigmoid(yn)).astype(BF16)

    _row_loop(ts, conv_rows, unroll=True)
    zext_ref[0:HIST, :] = zext_ref[ts:ts + HIST, :]

    for r0 in range(0, ts, ts // 2):
        rows = slice(r0, r0 + ts // 2)
        m = jnp.dot(a_s[rows, :], wpw2_ref[...], preferred_element_type=F32) + bpw2_ref[...]
        h_out_ref[rows, :] = x_ref[rows, :] + m
    _route_tail(h_out_ref, gffn_ref, wrh_ref, wrl_ref, br_ref, triu_ref, run_ref,
                hn_out_ref, route_ref, route_t_ref, cnt_ref, hi_s, lo_s)


def _gmlp_layer_body(h_ref, g1_ref, g2_ref, rprev_ref, gmix_ref, win_ref, bin_ref, vg_ref,
                     ws_ref, bst_ref, wout_ref, bout_ref, gffn_ref, wrh_ref, wrl_ref, br_ref,
                     triu_ref,
                     h_out_ref, hn_out_ref, route_ref, route_t_ref, cnt_ref,
                     hn_s, u_s, v_s, gated_s, hprev_s, hi_s, lo_s, run_ref):
    i = pl.program_id(0)
    ts = h_ref.shape[0]

    @pl.when(i == 0)
    def _():
        run_ref[...] = jnp.zeros_like(run_ref)
        hprev_s[...] = jnp.zeros_like(hprev_s)

    _route_tail(hprev_s, gffn_ref, wrh_ref, wrl_ref, br_ref, triu_ref, run_ref,
                hn_out_ref, route_ref, route_t_ref, cnt_ref, hi_s, lo_s,
                live=jnp.where(i > 0, 1.0, 0.0))

    def norm_rows(rows):
        h = _moe_combine(h_ref[rows, :], rprev_ref[rows, :], g1_ref[rows, :], g2_ref[rows, :])
        h_out_ref[rows, :] = h
        hn_s[rows, :] = _rms(h, gmix_ref[...]).astype(BF16)

    _row_loop(ts, norm_rows)

    hn = hn_s[...]
    ssq = jnp.zeros((ts, 1), F32)
    for c0 in range(0, 2 * GMLP_INNER, GMLP_CW):
        cols = slice(c0, c0 + GMLP_CW)
        p = jnp.dot(hn, win_ref[:, cols], preferred_element_type=F32)
        zc = _gelu_tanh(p + bin_ref[:, cols])
        if c0 < GMLP_INNER:
            u_s[:, cols] = zc
        else:
            v_s[:, c0 - GMLP_INNER:c0 - GMLP_INNER + GMLP_CW] = zc
            ssq = ssq + jnp.sum(zc * zc, axis=-1, keepdims=True)
    rs = lax.rsqrt(ssq * (1.0 / GMLP_INNER) + EPS)

    for b0 in range(0, ts, GMLP_BLOCK):
        rows = slice(b0, b0 + GMLP_BLOCK)
        for hd in range(GMLP_HEADS):
            cols = slice(hd * GMLP_HEAD_DIM, (hd + 1) * GMLP_HEAD_DIM)
            vv = (v_s[rows, cols] * rs[rows] * vg_ref[:, cols]).astype(BF16)
            sv = jnp.dot(ws_ref[hd], vv, preferred_element_type=F32) + bst_ref[:, hd:hd + 1]
            gated_s[rows, cols] = (u_s[rows, cols] * sv).astype(BF16)

    out = jnp.dot(gated_s[...], wout_ref[...], preferred_element_type=F32) + bout_ref[...]
    hnew = h_out_ref[...] + out
    h_out_ref[...] = hnew
    hprev_s[...] = hnew


def _expert_body(first_ref, count_ref, nused_ref, rows_hbm, wg_ref, wu_ref, wd_ref, y_hbm,
                 xbuf, ybuf, wgu_s, wd_s, in_sem, out_sem):
    e = pl.program_id(0)
    first = first_ref[e]
    count = count_ref[e]
    n_used = nused_ref[0]

    def rows_of(g):
        return pl.ds(pl.multiple_of(g * BM, BM), BM)

    def in_copy(g):
        slot = lax.rem(g, IN_SLOTS)
        return pltpu.make_async_copy(rows_hbm.at[rows_of(g)], xbuf.at[slot], in_sem.at[slot])

    def out_copy(g):
        slot = lax.rem(g, OUT_SLOTS)
        return pltpu.make_async_copy(ybuf.at[slot], y_hbm.at[rows_of(g)], out_sem.at[slot])

    @pl.when(e == 0)
    def _():
        for g in range(IN_SLOTS - 1):
            @pl.when(g < n_used)
            def _():
                in_copy(g).start()

    @pl.when(count > 0)
    def _():
        wgu_s[:, :D_EXPERT] = wg_ref[0, 0].astype(BF16)
        wgu_s[:, D_EXPERT:] = wu_ref[0, 0].astype(BF16)
        wd_s[...] = wd_ref[0, 0].astype(BF16)

        def block(g, carry):
            in_copy(g).wait()

            @pl.when(g + IN_SLOTS - 1 < n_used)
            def _():
                in_copy(g + IN_SLOTS - 1).start()

            @pl.when(g >= OUT_SLOTS)
            def _():
                out_copy(g - OUT_SLOTS).wait()

            lo, hi = _unpack_pairs_f32(xbuf[lax.rem(g, IN_SLOTS)])
            half = lo.shape[1]
            gu = (jnp.dot(lo.astype(BF16), wgu_s[:half, :], preferred_element_type=F32)
                  + jnp.dot(hi.astype(BF16), wgu_s[half:, :], preferred_element_type=F32))
            gate = gu[:, :D_EXPERT]
            hb = (gate * jax.nn.sigmoid(gate)) * gu[:, D_EXPERT:]
            y = jnp.dot(hb.astype(BF16), wd_s[...], preferred_element_type=F32)
            ybuf[lax.rem(g, OUT_SLOTS)] = _pack_bf16_pairs(y.astype(BF16))
            out_copy(g).start()
            return carry

        lax.fori_loop(first, first + count, block, 0)

    @pl.when(e == pl.num_programs(0) - 1)
    def _():
        for back in range(OUT_SLOTS, 0, -1):
            @pl.when(n_used >= back)
            def _():
                out_copy(n_used - back).wait()


def _final_body(h_ref, g1_ref, g2_ref, rprev_ref, gfin_ref, o_ref):
    h = _moe_combine(h_ref[...], rprev_ref[...], g1_ref[...], g2_ref[...])
    o_ref[...] = _rms(h, gfin_ref[...])


def _const_spec(shape):
    return pl.BlockSpec(shape, lambda i: (0,) * len(shape), pipeline_mode=pl.Buffered(1))


def _row_spec(ts, width):
    return pl.BlockSpec((ts, width), lambda i: (i, 0))


def _layer_out(t, d, ts, lag=0):
    last = t // ts - 1
    mixed = lambda i: jnp.minimum(i, last)
    routed = lambda i: jnp.maximum(i - lag, 0)
    shapes = [jax.ShapeDtypeStruct((t, d), F32),
              jax.ShapeDtypeStruct((t, d // 2), jnp.int32),
              jax.ShapeDtypeStruct((t, LANES), F32),
              jax.ShapeDtypeStruct((SUBLANES, t), F32),
              jax.ShapeDtypeStruct((N_EXPERTS, LANES), F32)]
    specs = [pl.BlockSpec((ts, d), lambda i: (mixed(i), 0)),
             pl.BlockSpec((ts, d // 2), lambda i: (routed(i), 0)),
             pl.BlockSpec((ts, LANES), lambda i: (routed(i), 0)),
             pl.BlockSpec((SUBLANES, ts), lambda i: (0, routed(i))),
             pl.BlockSpec((N_EXPERTS, LANES), lambda i: (0, 0))]
    return shapes, specs


def _router_specs(d, ts):
    return [_const_spec((1, d)), _const_spec((ROUTER_ROWS, d)), _const_spec((ROUTER_ROWS, d)),
            _const_spec((ROUTER_ROWS, ts)), _const_spec((ts, ts))]


def _router_scratch(ts, d):
    return [pltpu.VMEM((ts, d), BF16), pltpu.VMEM((ts, d), BF16),
            pltpu.VMEM((N_EXPERTS, ts), F32)]


def _layer_params():
    return pltpu.CompilerParams(dimension_semantics=("arbitrary",), vmem_limit_bytes=VMEM_LIMIT)


def _conv_layer(x2, seq, gmix, wpw1, bpw1, wdw, bdw, lng, lnb, wpw2, bpw2, router):
    t, d = x2.shape
    ts = TS_CONV
    shapes, out_specs = _layer_out(t, d, ts)
    body = functools.partial(_conv_layer_body, seq // ts)
    return pl.pallas_call(
        body,
        grid=(t // ts,),
        in_specs=[_row_spec(ts, d), _const_spec((1, d)), _const_spec((d, 2 * d)),
                  _const_spec((1, 2 * d)), _const_spec((CONV_WIDTH, SUBLANES, d)),
                  _const_spec((SUBLANES, d)),
                  _const_spec((1, d)), _const_spec((1, d)), _const_spec((d, d)),
                  _const_spec((1, d))] + _router_specs(d, ts),
        out_specs=out_specs,
        out_shape=shapes,
        scratch_shapes=[pltpu.VMEM((ts, d), BF16),
                        pltpu.VMEM((ts + HIST, d), F32),
                        pltpu.VMEM((SUBLANES - 1, ts + HIST - SUBLANES, d), F32),
                        pltpu.VMEM((ts, d), F32),
                        pltpu.VMEM((ts, d), BF16)] + _router_scratch(ts, d),
        compiler_params=_layer_params(),
        name="conv_layer",
    )(x2, gmix, wpw1, bpw1, wdw, bdw, lng, lnb, wpw2, bpw2, *router)


def _gmlp_layer(h, g1, g2, rprev, gmix, win, bin_, vg, ws, bst, wout, bout, router):
    t, d = h.shape
    ts = TS_GMLP
    n_tiles = t // ts
    shapes, out_specs = _layer_out(t, d, ts, lag=1)
    tile_spec = lambda width: pl.BlockSpec((ts, width), lambda i: (jnp.minimum(i, n_tiles - 1), 0))
    return pl.pallas_call(
        _gmlp_layer_body,
        grid=(n_tiles + 1,),
        in_specs=[tile_spec(d), tile_spec(d // 2), tile_spec(d // 2), tile_spec(LANES),
                  _const_spec((1, d)), _const_spec((d, 2 * GMLP_INNER)),
                  _const_spec((1, 2 * GMLP_INNER)), _const_spec((1, GMLP_INNER)),
                  _const_spec((GMLP_HEADS, GMLP_BLOCK, GMLP_BLOCK)),
                  _const_spec((GMLP_BLOCK, GMLP_HEADS)), _const_spec((GMLP_INNER, d)),
                  _const_spec((1, d))] + _router_specs(d, ts),
        out_specs=out_specs,
        out_shape=shapes,
        scratch_shapes=[pltpu.VMEM((ts, d), BF16),
                        pltpu.VMEM((ts, GMLP_INNER), F32),
                        pltpu.VMEM((ts, GMLP_INNER), F32),
                        pltpu.VMEM((ts, GMLP_INNER), BF16),
                        pltpu.VMEM((ts, d), F32)] + _router_scratch(ts, d),
        compiler_params=_layer_params(),
        name="gmlp_layer",
    )(h, g1, g2, rprev, gmix, win, bin_, vg, ws, bst, wout, bout, *router)


def _experts(rows, first_blk, blk_count, n_used, wg, wu, wd, layer):
    r, w = rows.shape
    d = wg.shape[2]

    def w_map(e, first, count, nu):
        return (layer, e, 0, 0)

    any_space = pl.BlockSpec(memory_space=pl.ANY)
    grid_spec = pltpu.PrefetchScalarGridSpec(
        num_scalar_prefetch=3,
        grid=(N_EXPERTS,),
        in_specs=[any_space,
                  pl.BlockSpec((1, 1, d, D_EXPERT), w_map),
                  pl.BlockSpec((1, 1, d, D_EXPERT), w_map),
                  pl.BlockSpec((1, 1, D_EXPERT, d), w_map)],
        out_specs=any_space,
        scratch_shapes=[pltpu.VMEM((IN_SLOTS, BM, w), rows.dtype),
                        pltpu.VMEM((OUT_SLOTS, BM, w), rows.dtype),
                        pltpu.VMEM((d, 2 * D_EXPERT), BF16),
                        pltpu.VMEM((D_EXPERT, d), BF16),
                        pltpu.SemaphoreType.DMA((IN_SLOTS,)),
                        pltpu.SemaphoreType.DMA((OUT_SLOTS,))],
    )
    return pl.pallas_call(
        _expert_body,
        grid_spec=grid_spec,
        out_shape=jax.ShapeDtypeStruct((r, w), rows.dtype),
        compiler_params=_layer_params(),
        name="experts",
    )(first_blk, blk_count, n_used, rows, wg, wu, wd)


def _final(h, g1, g2, rprev, gfin):
    t, d = h.shape
    ts = TS_FINAL
    return pl.pallas_call(
        _final_body,
        grid=(t // ts,),
        in_specs=[_row_spec(ts, d), _row_spec(ts, d // 2), _row_spec(ts, d // 2),
                  _row_spec(ts, LANES),
                  _const_spec((1, d))],
        out_specs=_row_spec(ts, d),
        out_shape=jax.ShapeDtypeStruct((t, d), F32),
        compiler_params=_layer_params(),
        name="final_norm",
    )(h, g1, g2, rprev, gfin)


def _plan_body(cnt_ref, rt_ref, d1_ref, d2_ref, first_ref, count_ref, nu_ref):
    e1 = rt_ref[R_E1:R_E1 + 1, :]
    e2 = rt_ref[R_E2:R_E2 + 1, :]
    d1 = rt_ref[R_RANK1:R_RANK1 + 1, :]
    d2 = rt_ref[R_RANK2:R_RANK2 + 1, :]
    pb = jnp.int32(0)
    for e in range(N_EXPERTS):
        nb = lax.shift_right_logical(cnt_ref[e] + (BM - 1), BM.bit_length() - 1)
        ps = (pb * BM).astype(F32)
        d1 = d1 + jnp.where(e1 == e, ps, 0.0)
        d2 = d2 + jnp.where(e2 == e, ps, 0.0)
        first_ref[e] = pb
        count_ref[e] = nb
        pb = pb + nb
    nu_ref[0] = pb
    d1_ref[...] = d1.astype(jnp.int32)
    d2_ref[...] = d2.astype(jnp.int32)


def _plan(route_t, counts):
    t = route_t.shape[1]
    smem = pl.BlockSpec(memory_space=pltpu.SMEM)
    vmem = pl.BlockSpec(memory_space=pltpu.VMEM)
    return pl.pallas_call(
        _plan_body,
        in_specs=[smem, vmem],
        out_specs=[vmem, vmem, smem, smem, smem],
        out_shape=[jax.ShapeDtypeStruct((1, t), jnp.int32),
                   jax.ShapeDtypeStruct((1, t), jnp.int32),
                   jax.ShapeDtypeStruct((N_EXPERTS,), jnp.int32),
                   jax.ShapeDtypeStruct((N_EXPERTS,), jnp.int32),
                   jax.ShapeDtypeStruct((1,), jnp.int32)],
        name="moe_plan",
    )(counts, route_t)


def _sc_workers():
    info = plsc.get_sparse_core_info()
    return info.num_cores, info.num_cores * info.num_subcores


def _sc_mesh():
    return plsc.VectorSubcoreMesh(core_axis_name="c", subcore_axis_name="s")


def _sc_worker_id(num_cores):
    return lax.axis_index("s") * num_cores + lax.axis_index("c")


def _dispatch(hn, dest1, dest2, n_rows):
    t, w = hn.shape
    num_cores, n_workers = _sc_workers()
    per_w = t // n_workers
    chunk = SC_DISPATCH_CHUNK

    def body(hn_hbm, d1_hbm, d2_hbm, rows_hbm, buf, i1, i2, sem_rows, sem_1, sem_2):
        base_w = _sc_worker_id(num_cores) * per_w

        @pl.loop(0, per_w // chunk)
        def _(j):
            base = pl.multiple_of(base_w + j * chunk, chunk)
            loads = [pltpu.async_copy(hn_hbm.at[pl.ds(base, chunk)], buf, sem_rows),
                     pltpu.async_copy(d1_hbm.at[:, pl.ds(base, chunk)], i1, sem_1),
                     pltpu.async_copy(d2_hbm.at[:, pl.ds(base, chunk)], i2, sem_2)]
            for c in loads:
                c.wait()
            scatters = [pltpu.async_copy(buf, rows_hbm.at[i1.at[0]], sem_1),
                        pltpu.async_copy(buf, rows_hbm.at[i2.at[0]], sem_2)]
            for c in scatters:
                c.wait()

    return pl.kernel(
        body,
        out_type=jax.ShapeDtypeStruct((n_rows, w), hn.dtype),
        mesh=_sc_mesh(),
        scratch_types=[pltpu.VMEM((chunk, w), hn.dtype),
                       pltpu.VMEM((1, chunk), jnp.int32),
                       pltpu.VMEM((1, chunk), jnp.int32),
                       pltpu.SemaphoreType.DMA, pltpu.SemaphoreType.DMA,
                       pltpu.SemaphoreType.DMA],
        name="moe_dispatch",
    )(hn, dest1, dest2)


def _combine_gather(y, dest1, dest2):
    d = y.shape[1]
    t = dest1.shape[1]
    num_cores, n_workers = _sc_workers()
    per_w = t // n_workers
    chunk = SC_COMBINE_CHUNK

    half = chunk // 2

    def body(y_hbm, d1_hbm, d2_hbm, g1_hbm, g2_hbm, buf_a, buf_b, idx,
             gsem_a, gsem_b, wsem_a, wsem_b):
        base_w = _sc_worker_id(num_cores) * per_w
        halves = ((buf_a, gsem_a, wsem_a, 0), (buf_b, gsem_b, wsem_b, half))

        def wait_write(buf, wsem):
            pltpu.make_async_copy(buf, g1_hbm.at[pl.ds(0, half)], wsem).wait()

        @pl.loop(0, per_w // chunk)
        def _(j):
            base = pl.multiple_of(base_w + j * chunk, chunk)
            for table, (d_hbm, g_hbm) in enumerate(((d1_hbm, g1_hbm), (d2_hbm, g2_hbm))):
                pltpu.sync_copy(d_hbm.at[:, pl.ds(base, chunk)], idx)
                gathers = []
                for buf, gsem, wsem, off in halves:
                    if table == 0:
                        @pl.when(j > 0)
                        def _():
                            wait_write(buf, wsem)
                    else:
                        wait_write(buf, wsem)
                    gathers.append(pltpu.async_copy(
                        y_hbm.at[idx.at[0, pl.ds(off, half)]], buf, gsem))
                for (buf, gsem, wsem, off), gather in zip(halves, gathers):
                    gather.wait()
                    pltpu.async_copy(buf, g_hbm.at[pl.ds(base + off, half)], wsem)

        for buf, gsem, wsem, off in halves:
            wait_write(buf, wsem)

    out = jax.ShapeDtypeStruct((t, d), y.dtype)
    return pl.kernel(
        body,
        out_type=(out, out),
        mesh=_sc_mesh(),
        scratch_types=[pltpu.VMEM((half, d), y.dtype),
                       pltpu.VMEM((half, d), y.dtype),
                       pltpu.VMEM((1, chunk), jnp.int32),
                       pltpu.SemaphoreType.DMA, pltpu.SemaphoreType.DMA,
                       pltpu.SemaphoreType.DMA, pltpu.SemaphoreType.DMA],
        name="moe_combine_gather",
    )(y, dest1, dest2)


def _moe(hn, route_t, cnt, wg, wu, wd, layer):
    t = hn.shape[0]
    n_blk = (2 * t) // BM + N_EXPERTS
    counts = cnt[:, 0].astype(jnp.int32)
    dest1, dest2, first_blk, blk_count, n_used = _plan(route_t, counts)
    rows = _dispatch(hn, dest1, dest2, n_blk * BM)
    y = _experts(rows, first_blk, blk_count, n_used, wg, wu, wd, layer)
    return _combine_gather(y, dest1, dest2)


def _cast_body(w_ref, o_ref):
    o_ref[...] = w_ref[...].astype(o_ref.dtype)


def _to_bf16(w):
    rows, cols = w.shape
    return pl.pallas_call(
        _cast_body,
        grid=(rows // CAST_ROWS,),
        in_specs=[_row_spec(CAST_ROWS, cols)],
        out_specs=_row_spec(CAST_ROWS, cols),
        out_shape=jax.ShapeDtypeStruct((rows, cols), BF16),
        compiler_params=_layer_params(),
        name="weights_to_bf16",
    )(w)


def _router_inputs(gffn, w_group, b_group, w_expert, b_expert, ts):
    d = w_group.shape[0]
    wr = jnp.zeros((ROUTER_ROWS, d), F32)
    wr = wr.at[:N_EXPERTS].set(w_expert.T).at[GROUP_ROW0:GROUP_ROW0 + N_GROUPS].set(w_group.T)
    br = jnp.zeros((ROUTER_ROWS,), F32)
    br = br.at[:N_EXPERTS].set(b_expert).at[GROUP_ROW0:GROUP_ROW0 + N_GROUPS].set(b_group)
    w_hi = wr.astype(BF16)
    w_lo = (wr - w_hi.astype(F32)).astype(BF16)
    idx = jnp.arange(ts)
    triu = (idx[:, None] < idx[None, :]).astype(BF16)
    return (gffn.reshape(1, -1), w_hi, w_lo, jnp.broadcast_to(br[:, None], (ROUTER_ROWS, ts)), triu)


def kernel(x, norm_mix_g, norm_ffn_g, cv_w_pw1, cv_b_pw1, cv_w_dw, cv_b_dw, cv_ln_g, cv_ln_b, cv_w_pw2, cv_b_pw2, gm_w_in, gm_b_in, gm_v_norm_g, gm_w_s, gm_b_s, gm_w_out, gm_b_out, moe_w_group, moe_b_group, moe_w_expert, moe_b_expert, moe_w_gate, moe_w_up, moe_w_down, final_g):
    bsz, seq, d = x.shape
    t = bsz * seq
    x2 = x.reshape(t, d)
    row = lambda a: a.reshape(1, -1)

    router0 = _router_inputs(norm_ffn_g[0], moe_w_group[0], moe_b_group[0], moe_w_expert[0],
                             moe_b_expert[0], TS_CONV)
    h1, hn1, route0, route_t0, cnt0 = _conv_layer(
        x2, seq, row(norm_mix_g[0]), _to_bf16(cv_w_pw1[0]), row(cv_b_pw1[0]),
        jnp.broadcast_to(cv_w_dw[0][:, None, :], (CONV_WIDTH, SUBLANES, d)),
        jnp.broadcast_to(cv_b_dw[0][None, :], (SUBLANES, d)),
        row(cv_ln_g[0]), row(cv_ln_b[0]), _to_bf16(cv_w_pw2[0]),
        row(cv_b_pw2[0]), router0)
    ga0, gb0 = _moe(hn1, route_t0, cnt0, moe_w_gate, moe_w_up, moe_w_down, 0)

    idx = jnp.arange(GMLP_BLOCK)
    mask = (idx[None, :] // GMLP_CHUNK) <= (idx[:, None] // GMLP_CHUNK)
    ws = jnp.where(mask[None], gm_w_s[0], 0.0).astype(BF16)
    router1 = _router_inputs(norm_ffn_g[1], moe_w_group[1], moe_b_group[1], moe_w_expert[1],
                             moe_b_expert[1], TS_GMLP)
    h2, hn2, route1, route_t1, cnt1 = _gmlp_layer(
        h1, ga0, gb0, route0, row(norm_mix_g[1]), _to_bf16(gm_w_in[0]), row(gm_b_in[0]),
        row(gm_v_norm_g[0]), ws, jnp.transpose(gm_b_s[0]), _to_bf16(gm_w_out[0]),
        row(gm_b_out[0]), router1)
    ga1, gb1 = _moe(hn2, route_t1, cnt1, moe_w_gate, moe_w_up, moe_w_down, 1)

    out = _final(h2, ga1, gb1, route1, row(final_g))
    return out.reshape(bsz, seq, d)
```

```python
import functools

import jax
import jax.numpy as jnp
from jax import lax
from jax.experimental import pallas as pl
from jax.experimental.pallas import tpu as pltpu
from jax.experimental.pallas import tpu_sc as plsc

D_MODEL = 1024
CONV_WIDTH = 31
GMLP_BLOCK = 128
GMLP_CHUNK = 64
GMLP_INNER = 2 * D_MODEL
GMLP_HEADS = 8
GMLP_HEAD_DIM = GMLP_INNER // GMLP_HEADS
N_GROUPS = 4
EXPERTS_PER_GROUP = 8
N_EXPERTS = N_GROUPS * EXPERTS_PER_GROUP
D_EXPERT = D_MODEL // 2
EPS = 1e-6

LANES = 128
SUBLANES = 8
HIST = 32
TS_CONV = 512
TS_GMLP = 512
TS_FINAL = 1024
FINAL_PARTS = 2
BM = 512
IN_SLOTS = 4
OUT_SLOTS = 3
RC = 32
CONV_CW = 256
GLU_CW = 256
GMLP_CW = 512
GROUP_ROW0 = N_EXPERTS
ROUTER_ROWS = 48
VMEM_LIMIT = 56 * 1024 * 1024
SC_DISPATCH_CHUNK = 128
SC_COMBINE_CHUNK = 128

R_E1, R_E2, R_RANK1, R_RANK2, R_GATE1, R_GATE2 = range(6)

F32 = jnp.float32
BF16 = jnp.bfloat16


def _gelu_tanh(x):
    c = 0.7978845608028654
    t = jnp.tanh(x * (c + (c * 0.044715) * (x * x)))
    hx = 0.5 * x
    return hx + hx * t


def _rms(xf, g):
    return xf * lax.rsqrt(jnp.mean(xf * xf, axis=-1, keepdims=True) + EPS) * g


def _pack_bf16_pairs(xb):
    w = xb.shape[1] // 2
    bits = lax.bitcast_convert_type(xb.astype(F32), jnp.int32)
    return lax.shift_right_logical(bits[:, :w], 16) | bits[:, w:]


def _unpack_pairs_f32(p):
    lo = lax.bitcast_convert_type(lax.shift_left(p, 16), F32)
    hi = lax.bitcast_convert_type(p & jnp.int32(-65536), F32)
    return lo, hi


def _moe_combine(h, rp, p1, p2):
    w = h.shape[1] // 2
    g1 = rp[:, R_GATE1:R_GATE1 + 1]
    g2 = rp[:, R_GATE2:R_GATE2 + 1]
    lo1, hi1 = _unpack_pairs_f32(p1)
    lo2, hi2 = _unpack_pairs_f32(p2)
    return jnp.concatenate([h[:, :w] + g1 * lo1 + g2 * lo2,
                            h[:, w:] + g1 * hi1 + g2 * hi2], axis=1)


def _row_loop(n_rows, fn, unroll=True):
    def step(ci, carry):
        fn(pl.ds(pl.multiple_of(ci * RC, RC), RC))
        return carry

    lax.fori_loop(0, n_rows // RC, step, 0, unroll=unroll)


def _route_tail(h_ref, gffn_ref, wrh_ref, wrl_ref, br_ref, triu_ref, run_ref,
                hn_out_ref, route_ref, route_t_ref, cnt_ref, hi_s, lo_s, live=1.0):
    ts = h_ref.shape[0]

    def norm_rows(rows):
        hn2 = _rms(h_ref[rows, :], gffn_ref[...])
        hi = hn2.astype(BF16)
        hf = hi.astype(F32)
        hi_s[rows, :] = hi
        lo_s[rows, :] = (hn2 - hf).astype(BF16)
        hn_out_ref[rows, :] = _pack_bf16_pairs(hi)

    _row_loop(ts, norm_rows)

    nt = (((1,), (1,)), ((), ()))
    hi = hi_s[...]
    lt = (lax.dot_general(wrh_ref[...], hi, nt, preferred_element_type=F32)
          + lax.dot_general(wrl_ref[...], hi, nt, preferred_element_type=F32)
          + lax.dot_general(wrh_ref[...], lo_s[...], nt, preferred_element_type=F32)
          + br_ref[...])
    sub = lax.broadcasted_iota(jnp.int32, (SUBLANES, ts), 0).astype(F32)
    ninf = jnp.float32(-jnp.inf)
    big = jnp.float32(1e9)
    first_idx = lambda hit: jnp.min(jnp.where(hit, sub, big), axis=0, keepdims=True)

    g_ok = sub < N_GROUPS
    lg = jnp.where(g_ok, lt[GROUP_ROW0:GROUP_ROW0 + SUBLANES, :], ninf)
    gmax = jnp.max(lg, axis=0, keepdims=True)
    gsel = first_idx(lg == gmax)
    p_g = 1.0 / jnp.sum(jnp.where(g_ok, jnp.exp(lg - gmax), 0.0), axis=0, keepdims=True)
    le = lt[0:EXPERTS_PER_GROUP, :]
    for g in range(1, N_GROUPS):
        le = jnp.where(gsel == g, lt[g * EXPERTS_PER_GROUP:(g + 1) * EXPERTS_PER_GROUP, :], le)
    v1 = jnp.max(le, axis=0, keepdims=True)
    i1 = first_idx(le == v1)
    le2 = jnp.where(sub == i1, ninf, le)
    v2 = jnp.max(le2, axis=0, keepdims=True)
    i2 = first_idx(le2 == v2)
    e = jnp.exp(v2 - v1)
    den = 1.0 + e
    gate1 = p_g * (1.0 / den)
    gate2 = p_g * (e / den)
    e1 = gsel * EXPERTS_PER_GROUP + i1
    e2 = gsel * EXPERTS_PER_GROUP + i2

    eid = lax.broadcasted_iota(jnp.int32, (N_EXPERTS, ts), 0).astype(F32)
    oh1 = eid == e1
    oh2 = eid == e2
    oh = jnp.where(oh1 | oh2, 1.0, 0.0)
    run = run_ref[...]
    c = jnp.dot(oh.astype(BF16), triu_ref[...], preferred_element_type=F32) + run
    rank1 = jnp.sum(jnp.where(oh1, c, 0.0), axis=0, keepdims=True)
    rank2 = jnp.sum(jnp.where(oh2, c, 0.0), axis=0, keepdims=True)
    run = run + jnp.broadcast_to(jnp.sum(oh, axis=1, keepdims=True) * live, run.shape)
    run_ref[...] = run
    cnt_ref[...] = run[:, :LANES]

    rec_t = jnp.concatenate([e1, e2, rank1, rank2, gate1, gate2,
                             jnp.zeros((SUBLANES - 6, ts), F32)], axis=0)
    route_t_ref[...] = rec_t
    route_ref[...] = jnp.concatenate([rec_t, jnp.zeros((LANES - SUBLANES, ts), F32)], axis=0).T


def _conv_layer_body(tiles_per_seq,
                     x_ref, gmix_ref, wpw1_ref, bpw1_ref, wdw_ref, bdw_ref, lng_ref, lnb_ref,
                     wpw2_ref, bpw2_ref, gffn_ref, wrh_ref, wrl_ref, br_ref, triu_ref,
                     h_out_ref, hn_out_ref, route_ref, route_t_ref, cnt_ref,
                     hn_s, zext_ref, zs_ref, y_s, a_s, hi_s, lo_s, run_ref):
    i = pl.program_id(0)
    ts, d = x_ref.shape

    @pl.when(i == 0)
    def _():
        run_ref[...] = jnp.zeros_like(run_ref)

    @pl.when(i % tiles_per_seq == 0)
    def _():
        zext_ref[0:HIST, :] = jnp.zeros((HIST, d), F32)

    def norm_rows(rows):
        hn_s[rows, :] = _rms(x_ref[rows, :], gmix_ref[...]).astype(BF16)

    _row_loop(ts, norm_rows)

    hn = hn_s[...]
    for c0 in range(0, d, GLU_CW):
        ca = slice(c0, c0 + GLU_CW)
        cg = slice(d + c0, d + c0 + GLU_CW)
        pa = jnp.dot(hn, wpw1_ref[:, ca], preferred_element_type=F32) + bpw1_ref[:, ca]
        pg = jnp.dot(hn, wpw1_ref[:, cg], preferred_element_type=F32) + bpw1_ref[:, cg]
        zext_ref[HIST:HIST + ts, ca] = pa * jax.nn.sigmoid(pg)

    span = ts + HIST - SUBLANES
    sub_id = lax.broadcasted_iota(jnp.int32, (SUBLANES, LANES), 0)
    shifts = range(1, SUBLANES)
    for c0 in range(0, d, LANES):
        cols = slice(c0, c0 + LANES)

        def rotate(g):
            up = {0: g}
            for r in (4, 2, 6, 1, 3, 5, 7):
                step = r & -r
                up[r] = pltpu.roll(up[r - step], SUBLANES - step, axis=0)
            return [up[r] for r in shifts]

        cur = rotate(zext_ref[0:SUBLANES, cols])
        for m0 in range(0, span, SUBLANES):
            nxt = rotate(zext_ref[m0 + SUBLANES:m0 + 2 * SUBLANES, cols])
            for r in shifts:
                zs_ref[r - 1, m0:m0 + SUBLANES, cols] = jnp.where(
                    sub_id < SUBLANES - r, cur[r - 1], nxt[r - 1])
            cur = nxt

    first = HIST - (CONV_WIDTH - 1)

    def conv_rows(rows):
        r0 = rows.start
        groups = RC // SUBLANES
        for c0 in range(0, d, CONV_CW):
            cols = slice(c0, c0 + CONV_CW)
            accs = [bdw_ref[:, cols]] * groups
            for k in range(CONV_WIDTH):
                q, r = divmod(first + k, SUBLANES)
                w8 = wdw_ref[k, :, cols]
                for g in range(groups):
                    src = pl.ds(r0 + (q + g) * SUBLANES, SUBLANES)
                    slab = zext_ref[src, cols] if r == 0 else zs_ref[r - 1, src, cols]
                    accs[g] = accs[g] + w8 * slab
            for g in range(groups):
                y_s[pl.ds(r0 + g * SUBLANES, SUBLANES), cols] = accs[g]
        y = y_s[rows, :]
        mu = jnp.mean(y, axis=-1, keepdims=True)
        yc = y - mu
        yn = yc * lax.rsqrt(jnp.mean(yc * yc, axis=-1, keepdims=True) + EPS)
        yn = yn * lng_ref[...] + lnb_ref[...]
        a_s[rows, :] = (yn * jax.nn.sigmoid(yn)).astype(BF16)

    _row_loop(ts, conv_rows, unroll=True)
    zext_ref[0:HIST, :] = zext_ref[ts:ts + HIST, :]

    for r0 in range(0, ts, ts // 2):
        rows = slice(r0, r0 + ts // 2)
        m = jnp.dot(a_s[rows, :], wpw2_ref[...], preferred_element_type=F32) + bpw2_ref[...]
        h_out_ref[rows, :] = x_ref[rows, :] + m
    _route_tail(h_out_ref, gffn_ref, wrh_ref, wrl_ref, br_ref, triu_ref, run_ref,
                hn_out_ref, route_ref, route_t_ref, cnt_ref, hi_s, lo_s)


def _gmlp_layer_body(h_ref, g1_ref, g2_ref, rprev_ref, gmix_ref, win_ref, bin_ref, vg_ref,
                     ws_ref, bst_ref, wout_ref, bout_ref, gffn_ref, wrh_ref, wrl_ref, br_ref,
                     triu_ref,
                     h_out_ref, hn_out_ref, route_ref, route_t_ref, cnt_ref,
                     hn_s, u_s, v_s, gated_s, hprev_s, hi_s, lo_s, run_ref):
    i = pl.program_id(0)
    ts = h_ref.shape[0]

    @pl.when(i == 0)
    def _():
        run_ref[...] = jnp.zeros_like(run_ref)
        hprev_s[...] = jnp.zeros_like(hprev_s)

    _route_tail(hprev_s, gffn_ref, wrh_ref, wrl_ref, br_ref, triu_ref, run_ref,
                hn_out_ref, route_ref, route_t_ref, cnt_ref, hi_s, lo_s,
                live=jnp.where(i > 0, 1.0, 0.0))

    def norm_rows(rows):
        h = _moe_combine(h_ref[rows, :], rprev_ref[rows, :], g1_ref[rows, :], g2_ref[rows, :])
        h_out_ref[rows, :] = h
        hn_s[rows, :] = _rms(h, gmix_ref[...]).astype(BF16)

    _row_loop(ts, norm_rows)

    hn = hn_s[...]
    ssq = jnp.zeros((ts, 1), F32)
    for c0 in range(0, 2 * GMLP_INNER, GMLP_CW):
        cols = slice(c0, c0 + GMLP_CW)
        p = jnp.dot(hn, win_ref[:, cols], preferred_element_type=F32)
        zc = _gelu_tanh(p + bin_ref[:, cols])
        if c0 < GMLP_INNER:
            u_s[:, cols] = zc
        else:
            v_s[:, c0 - GMLP_INNER:c0 - GMLP_INNER + GMLP_CW] = zc
            ssq = ssq + jnp.sum(zc * zc, axis=-1, keepdims=True)
    rs = lax.rsqrt(ssq * (1.0 / GMLP_INNER) + EPS)

    for b0 in range(0, ts, GMLP_BLOCK):
        rows = slice(b0, b0 + GMLP_BLOCK)
        for hd in range(GMLP_HEADS):
            cols = slice(hd * GMLP_HEAD_DIM, (hd + 1) * GMLP_HEAD_DIM)
            vv = (v_s[rows, cols] * rs[rows] * vg_ref[:, cols]).astype(BF16)
            sv = jnp.dot(ws_ref[hd], vv, preferred_element_type=F32) + bst_ref[:, hd:hd + 1]
            gated_s[rows, cols] = (u_s[rows, cols] * sv).astype(BF16)

    out = jnp.dot(gated_s[...], wout_ref[...], preferred_element_type=F32) + bout_ref[...]
    hnew = h_out_ref[...] + out
    h_out_ref[...] = hnew
    hprev_s[...] = hnew


def _expert_body(first_ref, count_ref, nused_ref, rows_hbm, wg_ref, wu_ref, wd_ref, y_hbm,
                 xbuf, ybuf, wgu_s, wd_s, in_sem, out_sem):
    e = pl.program_id(0)
    first = first_ref[e]
    count = count_ref[e]
    n_used = nused_ref[0]

    def rows_of(g):
        return pl.ds(pl.multiple_of(g * BM, BM), BM)

    def in_copy(g):
        slot = lax.rem(g, IN_SLOTS)
        return pltpu.make_async_copy(rows_hbm.at[rows_of(g)], xbuf.at[slot], in_sem.at[slot])

    def out_copy(g):
        slot = lax.rem(g, OUT_SLOTS)
        return pltpu.make_async_copy(ybuf.at[slot], y_hbm.at[rows_of(g)], out_sem.at[slot])

    @pl.when(e == 0)
    def _():
        for g in range(IN_SLOTS - 1):
            @pl.when(g < n_used)
            def _():
                in_copy(g).start()

    @pl.when(count > 0)
    def _():
        wgu_s[:, :D_EXPERT] = wg_ref[0, 0].astype(BF16)
        wgu_s[:, D_EXPERT:] = wu_ref[0, 0].astype(BF16)
        wd_s[...] = wd_ref[0, 0].astype(BF16)

        def block(g, carry):
            in_copy(g).wait()

            @pl.when(g + IN_SLOTS - 1 < n_used)
            def _():
                in_copy(g + IN_SLOTS - 1).start()

            @pl.when(g >= OUT_SLOTS)
            def _():
                out_copy(g - OUT_SLOTS).wait()

            lo, hi = _unpack_pairs_f32(xbuf[lax.rem(g, IN_SLOTS)])
            half = lo.shape[1]
            gu = (jnp.dot(lo.astype(BF16), wgu_s[:half, :], preferred_element_type=F32)
                  + jnp.dot(hi.astype(BF16), wgu_s[half:, :], preferred_element_type=F32))
            gate = gu[:, :D_EXPERT]
            hb = (gate * jax.nn.sigmoid(gate)) * gu[:, D_EXPERT:]
            y = jnp.dot(hb.astype(BF16), wd_s[...], preferred_element_type=F32)
            ybuf[lax.rem(g, OUT_SLOTS)] = _pack_bf16_pairs(y.astype(BF16))
            out_copy(g).start()
            return carry

        lax.fori_loop(first, first + count, block, 0)

    @pl.when(e == pl.num_programs(0) - 1)
    def _():
        for back in range(OUT_SLOTS, 0, -1):
            @pl.when(n_used >= back)
            def _():
                out_copy(n_used - back).wait()


def _final_body(h_ref, g1_ref, g2_ref, rprev_ref, gfin_ref, *rest):
    o_ref = rest[-1]
    h = _moe_combine(h_ref[...], rprev_ref[...], g1_ref[...], g2_ref[...])
    o_ref[...] = _rms(h, gfin_ref[...])


def _const_spec(shape):
    return pl.BlockSpec(shape, lambda i: (0,) * len(shape), pipeline_mode=pl.Buffered(1))


def _row_spec(ts, width):
    return pl.BlockSpec((ts, width), lambda i: (i, 0))


def _layer_out(t, d, ts, lag=0):
    last = t // ts - 1
    mixed = lambda i: jnp.minimum(i, last)
    routed = lambda i: jnp.maximum(i - lag, 0)
    shapes = [jax.ShapeDtypeStruct((t, d), F32),
              jax.ShapeDtypeStruct((t, d // 2), jnp.int32),
              jax.ShapeDtypeStruct((t, LANES), F32),
              jax.ShapeDtypeStruct((SUBLANES, t), F32),
              jax.ShapeDtypeStruct((N_EXPERTS, LANES), F32)]
    specs = [pl.BlockSpec((ts, d), lambda i: (mixed(i), 0)),
             pl.BlockSpec((ts, d // 2), lambda i: (routed(i), 0)),
             pl.BlockSpec((ts, LANES), lambda i: (routed(i), 0)),
             pl.BlockSpec((SUBLANES, ts), lambda i: (0, routed(i))),
             pl.BlockSpec((N_EXPERTS, LANES), lambda i: (0, 0))]
    return shapes, specs


def _router_specs(d, ts):
    return [_const_spec((1, d)), _const_spec((ROUTER_ROWS, d)), _const_spec((ROUTER_ROWS, d)),
            _const_spec((ROUTER_ROWS, ts)), _const_spec((ts, ts))]


def _router_scratch(ts, d):
    return [pltpu.VMEM((ts, d), BF16), pltpu.VMEM((ts, d), BF16),
            pltpu.VMEM((N_EXPERTS, ts), F32)]


def _layer_params():
    return pltpu.CompilerParams(dimension_semantics=("arbitrary",), vmem_limit_bytes=VMEM_LIMIT)


def _conv_layer(x2, seq, gmix, wpw1, bpw1, wdw, bdw, lng, lnb, wpw2, bpw2, router):
    t, d = x2.shape
    ts = TS_CONV
    shapes, out_specs = _layer_out(t, d, ts)
    body = functools.partial(_conv_layer_body, seq // ts)
    return pl.pallas_call(
        body,
        grid=(t // ts,),
        in_specs=[_row_spec(ts, d), _const_spec((1, d)), _const_spec((d, 2 * d)),
                  _const_spec((1, 2 * d)), _const_spec((CONV_WIDTH, SUBLANES, d)),
                  _const_spec((SUBLANES, d)),
                  _const_spec((1, d)), _const_spec((1, d)), _const_spec((d, d)),
                  _const_spec((1, d))] + _router_specs(d, ts),
        out_specs=out_specs,
        out_shape=shapes,
        scratch_shapes=[pltpu.VMEM((ts, d), BF16),
                        pltpu.VMEM((ts + HIST, d), F32),
                        pltpu.VMEM((SUBLANES - 1, ts + HIST - SUBLANES, d), F32),
                        pltpu.VMEM((ts, d), F32),
                        pltpu.VMEM((ts, d), BF16)] + _router_scratch(ts, d),
        compiler_params=_layer_params(),
        name="conv_layer",
    )(x2, gmix, wpw1, bpw1, wdw, bdw, lng, lnb, wpw2, bpw2, *router)


def _gmlp_layer(h, g1, g2, rprev, gmix, win, bin_, vg, ws, bst, wout, bout, router):
    t, d = h.shape
    ts = TS_GMLP
    n_tiles = t // ts
    shapes, out_specs = _layer_out(t, d, ts, lag=1)
    tile_spec = lambda width: pl.BlockSpec((ts, width), lambda i: (jnp.minimum(i, n_tiles - 1), 0))
    return pl.pallas_call(
        _gmlp_layer_body,
        grid=(n_tiles + 1,),
        in_specs=[tile_spec(d), tile_spec(d // 2), tile_spec(d // 2), tile_spec(LANES),
                  _const_spec((1, d)), _const_spec((d, 2 * GMLP_INNER)),
                  _const_spec((1, 2 * GMLP_INNER)), _const_spec((1, GMLP_INNER)),
                  _const_spec((GMLP_HEADS, GMLP_BLOCK, GMLP_BLOCK)),
                  _const_spec((GMLP_BLOCK, GMLP_HEADS)), _const_spec((GMLP_INNER, d)),
                  _const_spec((1, d))] + _router_specs(d, ts),
        out_specs=out_specs,
        out_shape=shapes,
        scratch_shapes=[pltpu.VMEM((ts, d), BF16),
                        pltpu.VMEM((ts, GMLP_INNER), F32),
                        pltpu.VMEM((ts, GMLP_INNER), F32),
                        pltpu.VMEM((ts, GMLP_INNER), BF16),
                        pltpu.VMEM((ts, d), F32)] + _router_scratch(ts, d),
        compiler_params=_layer_params(),
        name="gmlp_layer",
    )(h, g1, g2, rprev, gmix, win, bin_, vg, ws, bst, wout, bout, *router)


def _experts(rows, first_blk, blk_count, n_used, wg, wu, wd, layer):
    r, w = rows.shape
    d = wg.shape[2]

    def w_map(e, first, count, nu):
        return (layer, e, 0, 0)

    any_space = pl.BlockSpec(memory_space=pl.ANY)
    grid_spec = pltpu.PrefetchScalarGridSpec(
        num_scalar_prefetch=3,
        grid=(N_EXPERTS,),
        in_specs=[any_space,
                  pl.BlockSpec((1, 1, d, D_EXPERT), w_map),
                  pl.BlockSpec((1, 1, d, D_EXPERT), w_map),
                  pl.BlockSpec((1, 1, D_EXPERT, d), w_map)],
        out_specs=any_space,
        scratch_shapes=[pltpu.VMEM((IN_SLOTS, BM, w), rows.dtype),
                        pltpu.VMEM((OUT_SLOTS, BM, w), rows.dtype),
                        pltpu.VMEM((d, 2 * D_EXPERT), BF16),
                        pltpu.VMEM((D_EXPERT, d), BF16),
                        pltpu.SemaphoreType.DMA((IN_SLOTS,)),
                        pltpu.SemaphoreType.DMA((OUT_SLOTS,))],
    )
    return pl.pallas_call(
        _expert_body,
        grid_spec=grid_spec,
        out_shape=jax.ShapeDtypeStruct((r, w), rows.dtype),
        compiler_params=_layer_params(),
        name="experts",
    )(first_blk, blk_count, n_used, rows, wg, wu, wd)


def _final(h, g1, g2, rprev, gfin, part, n_parts, out_so_far=None):
    t, d = h.shape
    ts = TS_FINAL
    steps = t // n_parts // ts
    first = part * steps
    full_rows = lambda width: pl.BlockSpec((ts, width), lambda i: (i + first, 0))
    in_specs = [full_rows(d), _row_spec(ts, d // 2), _row_spec(ts, d // 2), full_rows(LANES),
                _const_spec((1, d))]
    args = [h, g1, g2, rprev, gfin]
    aliases = {}
    if out_so_far is not None:
        in_specs.append(pl.BlockSpec(memory_space=pl.ANY))
        args.append(out_so_far)
        aliases = {len(args) - 1: 0}
    return pl.pallas_call(
        _final_body,
        grid=(steps,),
        in_specs=in_specs,
        out_specs=full_rows(d),
        out_shape=jax.ShapeDtypeStruct((t, d), F32),
        input_output_aliases=aliases,
        compiler_params=_layer_params(),
        name="final_norm",
    )(*args)


def _plan_body(cnt_ref, rt_ref, d1_ref, d2_ref, first_ref, count_ref, nu_ref):
    e1 = rt_ref[R_E1:R_E1 + 1, :]
    e2 = rt_ref[R_E2:R_E2 + 1, :]
    d1 = rt_ref[R_RANK1:R_RANK1 + 1, :]
    d2 = rt_ref[R_RANK2:R_RANK2 + 1, :]
    pb = jnp.int32(0)
    for e in range(N_EXPERTS):
        nb = lax.shift_right_logical(cnt_ref[e] + (BM - 1), BM.bit_length() - 1)
        ps = (pb * BM).astype(F32)
        d1 = d1 + jnp.where(e1 == e, ps, 0.0)
        d2 = d2 + jnp.where(e2 == e, ps, 0.0)
        first_ref[e] = pb
        count_ref[e] = nb
        pb = pb + nb
    nu_ref[0] = pb
    d1_ref[...] = d1.astype(jnp.int32)
    d2_ref[...] = d2.astype(jnp.int32)


def _plan(route_t, counts):
    t = route_t.shape[1]
    smem = pl.BlockSpec(memory_space=pltpu.SMEM)
    vmem = pl.BlockSpec(memory_space=pltpu.VMEM)
    return pl.pallas_call(
        _plan_body,
        in_specs=[smem, vmem],
        out_specs=[vmem, vmem, smem, smem, smem],
        out_shape=[jax.ShapeDtypeStruct((1, t), jnp.int32),
                   jax.ShapeDtypeStruct((1, t), jnp.int32),
                   jax.ShapeDtypeStruct((N_EXPERTS,), jnp.int32),
                   jax.ShapeDtypeStruct((N_EXPERTS,), jnp.int32),
                   jax.ShapeDtypeStruct((1,), jnp.int32)],
        name="moe_plan",
    )(counts, route_t)


def _sc_workers():
    info = plsc.get_sparse_core_info()
    return info.num_cores, info.num_cores * info.num_subcores


def _sc_mesh():
    return plsc.VectorSubcoreMesh(core_axis_name="c", subcore_axis_name="s")


def _sc_worker_id(num_cores):
    return lax.axis_index("s") * num_cores + lax.axis_index("c")


def _dispatch(hn, dest1, dest2, n_rows):
    t, w = hn.shape
    num_cores, n_workers = _sc_workers()
    per_w = t // n_workers
    chunk = SC_DISPATCH_CHUNK

    def body(hn_hbm, d1_hbm, d2_hbm, rows_hbm, buf, i1, i2, sem_rows, sem_1, sem_2):
        base_w = _sc_worker_id(num_cores) * per_w

        @pl.loop(0, per_w // chunk)
        def _(j):
            base = pl.multiple_of(base_w + j * chunk, chunk)
            loads = [pltpu.async_copy(hn_hbm.at[pl.ds(base, chunk)], buf, sem_rows),
                     pltpu.async_copy(d1_hbm.at[:, pl.ds(base, chunk)], i1, sem_1),
                     pltpu.async_copy(d2_hbm.at[:, pl.ds(base, chunk)], i2, sem_2)]
            for c in loads:
                c.wait()
            scatters = [pltpu.async_copy(buf, rows_hbm.at[i1.at[0]], sem_1),
                        pltpu.async_copy(buf, rows_hbm.at[i2.at[0]], sem_2)]
            for c in scatters:
                c.wait()

    return pl.kernel(
        body,
        out_type=jax.ShapeDtypeStruct((n_rows, w), hn.dtype),
        mesh=_sc_mesh(),
        scratch_types=[pltpu.VMEM((chunk, w), hn.dtype),
                       pltpu.VMEM((1, chunk), jnp.int32),
                       pltpu.VMEM((1, chunk), jnp.int32),
                       pltpu.SemaphoreType.DMA, pltpu.SemaphoreType.DMA,
                       pltpu.SemaphoreType.DMA],
        name="moe_dispatch",
    )(hn, dest1, dest2)


def _combine_gather(y, dest1, dest2):
    d = y.shape[1]
    t = dest1.shape[1]
    num_cores, n_workers = _sc_workers()
    per_w = t // n_workers
    chunk = SC_COMBINE_CHUNK

    half = chunk // 2

    def body(y_hbm, d1_hbm, d2_hbm, g1_hbm, g2_hbm, buf_a, buf_b, idx,
             gsem_a, gsem_b, wsem_a, wsem_b):
        base_w = _sc_worker_id(num_cores) * per_w
        halves = ((buf_a, gsem_a, wsem_a, 0), (buf_b, gsem_b, wsem_b, half))

        def wait_write(buf, wsem):
            pltpu.make_async_copy(buf, g1_hbm.at[pl.ds(0, half)], wsem).wait()

        @pl.loop(0, per_w // chunk)
        def _(j):
            base = pl.multiple_of(base_w + j * chunk, chunk)
            for table, (d_hbm, g_hbm) in enumerate(((d1_hbm, g1_hbm), (d2_hbm, g2_hbm))):
                pltpu.sync_copy(d_hbm.at[:, pl.ds(base, chunk)], idx)
                gathers = []
                for buf, gsem, wsem, off in halves:
                    if table == 0:
                        @pl.when(j > 0)
                        def _():
                            wait_write(buf, wsem)
                    else:
                        wait_write(buf, wsem)
                    gathers.append(pltpu.async_copy(
                        y_hbm.at[idx.at[0, pl.ds(off, half)]], buf, gsem))
                for (buf, gsem, wsem, off), gather in zip(halves, gathers):
                    gather.wait()
                    pltpu.async_copy(buf, g_hbm.at[pl.ds(base + off, half)], wsem)

        for buf, gsem, wsem, off in halves:
            wait_write(buf, wsem)

    out = jax.ShapeDtypeStruct((t, d), y.dtype)
    return pl.kernel(
        body,
        out_type=(out, out),
        mesh=_sc_mesh(),
        scratch_types=[pltpu.VMEM((half, d), y.dtype),
                       pltpu.VMEM((half, d), y.dtype),
                       pltpu.VMEM((1, chunk), jnp.int32),
                       pltpu.SemaphoreType.DMA, pltpu.SemaphoreType.DMA,
                       pltpu.SemaphoreType.DMA, pltpu.SemaphoreType.DMA],
        name="moe_combine_gather",
    )(y, dest1, dest2)


def _moe(hn, route_t, cnt, wg, wu, wd, layer, n_parts=1):
    t = hn.shape[0]
    n_blk = (2 * t) // BM + N_EXPERTS
    counts = cnt[:, 0].astype(jnp.int32)
    dest1, dest2, first_blk, blk_count, n_used = _plan(route_t, counts)
    rows = _dispatch(hn, dest1, dest2, n_blk * BM)
    y = _experts(rows, first_blk, blk_count, n_used, wg, wu, wd, layer)
    tp = t // n_parts
    return [_combine_gather(y, dest1[:, p * tp:(p + 1) * tp], dest2[:, p * tp:(p + 1) * tp])
            for p in range(n_parts)]


def _router_inputs(gffn, w_group, b_group, w_expert, b_expert, ts):
    d = w_group.shape[0]
    wr = jnp.zeros((ROUTER_ROWS, d), F32)
    wr = wr.at[:N_EXPERTS].set(w_expert.T).at[GROUP_ROW0:GROUP_ROW0 + N_GROUPS].set(w_group.T)
    br = jnp.zeros((ROUTER_ROWS,), F32)
    br = br.at[:N_EXPERTS].set(b_expert).at[GROUP_ROW0:GROUP_ROW0 + N_GROUPS].set(b_group)
    w_hi = wr.astype(BF16)
    w_lo = (wr - w_hi.astype(F32)).astype(BF16)
    idx = jnp.arange(ts)
    triu = (idx[:, None] < idx[None, :]).astype(BF16)
    return (gffn.reshape(1, -1), w_hi, w_lo, jnp.broadcast_to(br[:, None], (ROUTER_ROWS, ts)), triu)


def kernel(x, norm_mix_g, norm_ffn_g, cv_w_pw1, cv_b_pw1, cv_w_dw, cv_b_dw, cv_ln_g, cv_ln_b, cv_w_pw2, cv_b_pw2, gm_w_in, gm_b_in, gm_v_norm_g, gm_w_s, gm_b_s, gm_w_out, gm_b_out, moe_w_group, moe_b_group, moe_w_expert, moe_b_expert, moe_w_gate, moe_w_up, moe_w_down, final_g):
    bsz, seq, d = x.shape
    t = bsz * seq
    x2 = x.reshape(t, d)
    row = lambda a: a.reshape(1, -1)

    router0 = _router_inputs(norm_ffn_g[0], moe_w_group[0], moe_b_group[0], moe_w_expert[0],
                             moe_b_expert[0], TS_CONV)
    h1, hn1, route0, route_t0, cnt0 = _conv_layer(
        x2, seq, row(norm_mix_g[0]), cv_w_pw1[0].astype(BF16), row(cv_b_pw1[0]),
        jnp.broadcast_to(cv_w_dw[0][:, None, :], (CONV_WIDTH, SUBLANES, d)),
        jnp.broadcast_to(cv_b_dw[0][None, :], (SUBLANES, d)),
        row(cv_ln_g[0]), row(cv_ln_b[0]), cv_w_pw2[0].astype(BF16),
        row(cv_b_pw2[0]), router0)
    (ga0, gb0), = _moe(hn1, route_t0, cnt0, moe_w_gate, moe_w_up, moe_w_down, 0)

    idx = jnp.arange(GMLP_BLOCK)
    mask = (idx[None, :] // GMLP_CHUNK) <= (idx[:, None] // GMLP_CHUNK)
    ws = jnp.where(mask[None], gm_w_s[0], 0.0).astype(BF16)
    router1 = _router_inputs(norm_ffn_g[1], moe_w_group[1], moe_b_group[1], moe_w_expert[1],
                             moe_b_expert[1], TS_GMLP)
    h2, hn2, route1, route_t1, cnt1 = _gmlp_layer(
        h1, ga0, gb0, route0, row(norm_mix_g[1]), gm_w_in[0].astype(BF16), row(gm_b_in[0]),
        row(gm_v_norm_g[0]), ws, jnp.transpose(gm_b_s[0]), gm_w_out[0].astype(BF16),
        row(gm_b_out[0]), router1)
    parts = _moe(hn2, route_t1, cnt1, moe_w_gate, moe_w_up, moe_w_down, 1, n_parts=FINAL_PARTS)
    out = None
    for p, (ga1, gb1) in enumerate(parts):
        out = _final(h2, ga1, gb1, route1, row(final_g), p, FINAL_PARTS, out)
    return out.reshape(bsz, seq, d)
```
